```python
import functools
import jax, jax.numpy as jnp
from jax import lax
import numpy as np

D_MODEL = 1024
BATCH = 32
SEQ = 256
DEPTH = 4
DEC_BATCH = 4
DEC_SEQ = 4096
PAST_LEN = 512

GRID_W = 64
D_MIX = 1024
W_GRP = 256
HEAD_DIM = 64
N_HEADS_A = W_GRP // HEAD_DIM
WIN_H = 8
WIN_W = 16
CONV_W = 3
N_FOURIER = 4
N_HEADS_D = 4
DK = 64
DV = 64
GATE_RANK = 16
GLA_TAU = 16.0
CHUNK = 64
RMS_EPS = 1e-6
N_IN = 14 * W_GRP + 2 * GATE_RANK

kernel_name = 'hybrid_dit_na_conv_fnet_gla_step'


def rms_norm(x, g):
    xf = x.astype(jnp.float32)
    y = xf * lax.rsqrt(jnp.mean(xf * xf, axis=-1, keepdims=True) + RMS_EPS)
    return (y * g.astype(jnp.float32)).astype(x.dtype)


def to_heads(x, n):
    b, s, w = x.shape
    return x.reshape(b, s, n, w // n).transpose(0, 2, 1, 3)


def from_heads(x):
    b, n, s, d = x.shape
    return x.transpose(0, 2, 1, 3).reshape(b, s, n * d)


def adaln(cvec, w_mod_l, b_mod_l):
    m = jax.nn.silu(cvec) @ w_mod_l + b_mod_l
    shift, scale, gate = jnp.split(m, 3, axis=-1)
    return shift, scale, gate


def split_in(p):
    sizes = [W_GRP] * 14 + [GATE_RANK, GATE_RANK]
    idx = [int(i) for i in np.cumsum(sizes)[:-1]]
    return jnp.split(p, idx, axis=-1)


def attn_ctx(q, k, v):
    s = jnp.einsum('bhqd,bhkd->bhqk', q, k).astype(jnp.float32)
    p = jax.nn.softmax(s, axis=-1).astype(v.dtype)
    return jnp.einsum('bhqk,bhkd->bhqd', p, v)


def neighbourhood_attn(q, k, v, ck, cv, rpb_l):
    b, h, n, d = q.shape
    rows = n // GRID_W
    kh = min(WIN_H, rows)
    r = np.arange(rows)
    rs = np.clip(r - kh // 2, 0, rows - kh)
    row_idx = rs[:, None] + np.arange(kh)[None, :]
    dr = row_idx - r[:, None] + (WIN_H - 1)
    col = np.arange(GRID_W)
    cs = np.clip(col - WIN_W // 2, 0, GRID_W - WIN_W)
    col_ok = (col[None, :] >= cs[:, None]) & (col[None, :] < cs[:, None] + WIN_W)
    dc = np.clip(col[None, :] - col[:, None] + (WIN_W - 1), 0, 2 * WIN_W - 2)
    qg = q.reshape(b, h, rows, GRID_W, d)
    kg = k.reshape(b, h, rows, GRID_W, d)[:, :, row_idx]
    vg = v.reshape(b, h, rows, GRID_W, d)[:, :, row_idx]
    s_lat = jnp.einsum('bhrqd,bhrikd->bhrqik', qg, kg).astype(jnp.float32)
    bias = rpb_l[:, dr[:, None, :, None], dc[None, :, None, :]]
    s_lat = jnp.where(col_ok[:, None, :], s_lat + bias.astype(jnp.float32), -jnp.inf)
    s_ctx = jnp.einsum('bhrqd,bhcd->bhrqc', qg, ck).astype(jnp.float32)
    n_lat = kh * GRID_W
    s_all = jnp.concatenate([s_lat.reshape(b, h, rows, GRID_W, n_lat), s_ctx], axis=-1)
    p = jax.nn.softmax(s_all, axis=-1).astype(v.dtype)
    p_lat = p[..., :n_lat].reshape(b, h, rows, GRID_W, kh, GRID_W)
    o = (jnp.einsum('bhrqik,bhrikd->bhrqd', p_lat, vg)
         + jnp.einsum('bhrqc,bhcd->bhrqd', p[..., n_lat:], cv))
    return o.reshape(b, h, n, d)


def short_conv(bB, cB, hB, w):
    u = cB * hB
    up = jnp.pad(u, ((0, 0), (1, 1), (0, 0)))
    y = up[:, :-2] * w[0] + up[:, 1:-1] * w[1] + up[:, 2:] * w[2]
    return bB * y


def fourier_mix(u):
    b, s, w = u.shape
    ug = u.reshape(b, s, N_FOURIER, w // N_FOURIER).astype(jnp.float32)
    f = jnp.fft.fftn(ug, axes=(1, 3), norm='ortho').real
    return f.reshape(b, s, w).astype(u.dtype)


def gla_dir(q, k, v, log_a, s0):
    b, h, s, _ = q.shape
    nc = s // CHUNK
    rc = lambda t: t.reshape(b, h, nc, CHUNK, t.shape[-1])
    qc, kc, vc, gc = rc(q), rc(k), rc(v), rc(log_a)
    cum = jnp.cumsum(gc, axis=3)
    total = cum[:, :, :, -1:]
    q_dec = qc * jnp.exp(cum)
    k_inv = kc * jnp.exp(-cum)
    k_end = kc * jnp.exp(total - cum)
    causal = np.tril(np.ones((CHUNK, CHUNK), dtype=bool))
    att = jnp.where(causal, jnp.einsum('bhntk,bhnsk->bhnts', q_dec, k_inv), 0.0)
    o_intra = jnp.einsum('bhnts,bhnsv->bhntv', att, vc)
    upd = jnp.einsum('bhnsk,bhnsv->bhnkv', k_end, vc)
    decay = jnp.exp(total[:, :, :, 0])

    def step(state, inp):
        d_c, u_c = inp
        return d_c[..., None] * state + u_c, state

    s_fin, s_start = lax.scan(step, s0.astype(jnp.float32),
                              (jnp.moveaxis(decay, 2, 0), jnp.moveaxis(upd, 2, 0)))
    s_start = jnp.moveaxis(s_start, 0, 2)
    o_inter = jnp.einsum('bhntk,bhnkv->bhntv', q_dec, s_start)
    return (o_intra + o_inter).reshape(b, h, s, DV), s_fin


def gla_mixer(qD, kD, vD, aF, aB, up_f, b_f, up_b, b_b, s0_f, s0_b, gn):
    f32 = jnp.float32
    q = to_heads(qD, N_HEADS_D).astype(f32) * (DK ** -0.5)
    k = to_heads(kD, N_HEADS_D).astype(f32)
    v = to_heads(vD, N_HEADS_D).astype(f32)
    la_f = to_heads(jax.nn.log_sigmoid((aF @ up_f + b_f).astype(f32)) / GLA_TAU, N_HEADS_D)
    la_b = to_heads(jax.nn.log_sigmoid((aB @ up_b + b_b).astype(f32)) / GLA_TAU, N_HEADS_D)
    flip = lambda t: jnp.flip(t, axis=2)
    o_f, s_f = gla_dir(q, k, v, la_f, s0_f)
    o_b, s_b = gla_dir(flip(q), flip(k), flip(v), flip(la_b), s0_b)
    o = rms_norm(o_f + flip(o_b), gn)
    return from_heads(o).astype(qD.dtype), s_f, s_b


def trunk_layer(x, shift, scale, gate, attn_fn, s0_f, s0_b,
                norm_g_l, w_in_l, qn_l, kn_l, conv_l, up_f, b_f, up_b, b_b, gn_l, w_out_l):
    hN = rms_norm(x, norm_g_l) * (1.0 + scale) + shift
    p = hN @ w_in_l
    (qA, kA, vA, gA, bB, cB, hB, gB, uC, gC, qD, kD, vD, gD, aF, aB) = split_in(p)
    q = rms_norm(to_heads(qA, N_HEADS_A), qn_l) * (HEAD_DIM ** -0.5)
    k = rms_norm(to_heads(kA, N_HEADS_A), kn_l)
    v = to_heads(vA, N_HEADS_A)
    yA = from_heads(attn_fn(q, k, v)) * jax.nn.silu(gA)
    yB = short_conv(bB, cB, hB, conv_l) * jax.nn.silu(gB)
    yC = fourier_mix(uC) * jax.nn.silu(gC)
    oD, s_f, s_b = gla_mixer(qD, kD, vD, aF, aB, up_f, b_f, up_b, b_b, s0_f, s0_b, gn_l)
    yD = oD * jax.nn.silu(gD)
    y = jnp.concatenate([yA, yB, yC, yD], axis=-1) @ w_out_l
    return x + gate * y, k, v, s_f, s_b


def setup_inputs(seed: int = 0) -> dict:
    key = jax.random.key(seed)
    ks = jax.random.split(key, 24)
    nrm = lambda kk, shape, s: jax.random.normal(kk, shape, jnp.float32) * s
    return {
        'x_prompt': nrm(ks[0], (BATCH, SEQ, D_MODEL), 1.0),
        'x_sample': nrm(ks[1], (DEC_BATCH, DEC_SEQ, D_MODEL), 1.0),
        'cache_k': nrm(ks[2], (DEC_BATCH, DEPTH, N_HEADS_A, PAST_LEN, HEAD_DIM), 1.0),
        'cache_v': nrm(ks[3], (DEC_BATCH, DEPTH, N_HEADS_A, PAST_LEN, HEAD_DIM), 1.0),
        'state_gla': nrm(ks[4], (DEC_BATCH, DEPTH, 2, N_HEADS_D, DK, DV), 0.5),
        'c': nrm(ks[5], (DEC_BATCH, D_MODEL), 1.0),
        'c_ctx': nrm(ks[6], (D_MODEL,), 1.0),
        'norm_g': 1.0 + nrm(ks[7], (DEPTH, D_MODEL), 0.02),
        'w_mod': nrm(ks[8], (DEPTH, D_MODEL, 3 * D_MODEL), 0.5 * D_MODEL ** -0.5),
        'b_mod': nrm(ks[9], (DEPTH, 3 * D_MODEL), 0.02),
        'w_in': nrm(ks[10], (DEPTH, D_MODEL, N_IN), D_MODEL ** -0.5),
        'q_norm_g': 1.0 + nrm(ks[11], (DEPTH, HEAD_DIM), 0.02),
        'k_norm_g': 1.0 + nrm(ks[12], (DEPTH, HEAD_DIM), 0.02),
        'rpb': nrm(ks[13], (DEPTH, N_HEADS_A, 2 * WIN_H - 1, 2 * WIN_W - 1), 0.1),
        'conv_w': nrm(ks[14], (DEPTH, CONV_W, W_GRP), CONV_W ** -0.5),
        'gla_up_f': nrm(ks[15], (DEPTH, GATE_RANK, W_GRP), GATE_RANK ** -0.5),
        'gla_bias_f': nrm(ks[16], (DEPTH, W_GRP), 0.1),
        'gla_up_b': nrm(ks[17], (DEPTH, GATE_RANK, W_GRP), GATE_RANK ** -0.5),
        'gla_bias_b': nrm(ks[18], (DEPTH, W_GRP), 0.1),
        'gla_norm_g': 1.0 + nrm(ks[19], (DEPTH, DV), 0.02),
        'w_out': nrm(ks[20], (DEPTH, D_MIX, D_MODEL), D_MIX ** -0.5),
    }


def reference(x_prompt, x_sample, cache_k, cache_v, state_gla, c, c_ctx, norm_g, w_mod, b_mod,
              w_in, q_norm_g, k_norm_g, rpb, conv_w, gla_up_f, gla_bias_f, gla_up_b, gla_bias_b,
              gla_norm_g, w_out):
    def layer_params(l):
        return (norm_g[l], w_in[l], q_norm_g[l], k_norm_g[l], conv_w[l], gla_up_f[l],
                gla_bias_f[l], gla_up_b[l], gla_bias_b[l], gla_norm_g[l], w_out[l])

    x = x_prompt
    zero_state = jnp.zeros((x_prompt.shape[0], N_HEADS_D, DK, DV), jnp.float32)
    ks_, vs_, ss_ = [], [], []
    for l in range(DEPTH):
        shift, scale, gate = adaln(c_ctx, w_mod[l], b_mod[l])
        x, k, v, s_f, s_b = trunk_layer(x, shift, scale, gate, attn_ctx, zero_state, zero_state,
                                        *layer_params(l))
        ks_.append(k)
        vs_.append(v)
        ss_.append(jnp.stack([s_f, s_b], axis=1))
    y_prompt = x
    new_cache_k = jnp.stack(ks_, axis=1)
    new_cache_v = jnp.stack(vs_, axis=1)
    new_state_gla = jnp.stack(ss_, axis=1)

    xs = x_sample
    for l in range(DEPTH):
        shift, scale, gate = adaln(c, w_mod[l], b_mod[l])
        attn_fn = functools.partial(neighbourhood_attn, ck=cache_k[:, l], cv=cache_v[:, l],
                                    rpb_l=rpb[l])
        xs, _, _, _, _ = trunk_layer(xs, shift[:, None, :], scale[:, None, :], gate[:, None, :],
                                     attn_fn, state_gla[:, l, 0], state_gla[:, l, 1],
                                     *layer_params(l))
    y_sample = xs
    return (y_prompt, y_sample, new_cache_k, new_cache_v, new_state_gla)
```

```python
import functools
import math

import numpy as np
import jax
import jax.numpy as jnp
from jax import lax
from jax.experimental import pallas as pl
from jax.experimental.pallas import tpu as pltpu

F32 = jnp.float32
BF16 = jnp.bfloat16

D_MODEL = 1024
DEPTH = 4
GRID_W = 64
W_GRP = 256
HEAD_DIM = 64
N_HEADS = 4
WIN_H = 8
WIN_W = 16
N_FOURIER = 4
GATE_RANK = 16
GLA_TAU = 16.0
CHUNK = 64
RMS_EPS = 1e-6
N_IN = 14 * W_GRP + 2 * GATE_RANK
LANES = 128
N_IN_PAD = -(-N_IN // LANES) * LANES
G_QA, G_KA, G_VA, G_GA, G_BB, G_CB, G_HB, G_GB, G_UC, G_GC, G_QD, G_KD, G_VD, G_GD = range(14)
GATE_BLK = 14 * W_GRP // LANES
NEG_BIG = -1e30
VMEM_LIMIT = 56 * 1024 * 1024


def _cparams(*sem):
    return pltpu.CompilerParams(dimension_semantics=sem, vmem_limit_bytes=VMEM_LIMIT)


def _bdot(a, b):
    return jnp.dot(a.astype(BF16), b.astype(BF16), preferred_element_type=F32)


def _bdot_nt(a, b):
    return lax.dot_general(a.astype(BF16), b.astype(BF16), (((1,), (1,)), ((), ())),
                           preferred_element_type=F32)


def _split2(x):
    hi = x.astype(BF16)
    lo = (x - hi.astype(F32)).astype(BF16)
    return hi, lo


def _split3(x):
    h1 = x.astype(BF16)
    r = x - h1.astype(F32)
    h2 = r.astype(BF16)
    h3 = (r - h2.astype(F32)).astype(BF16)
    return h1, h2, h3


def _dot3_cl(m_hi, m_lo, x):
    x_hi, x_lo = _split2(x)
    d = functools.partial(jnp.dot, preferred_element_type=F32)
    return d(m_hi, x_hi) + (d(m_hi, x_lo) + d(m_lo, x_hi))


def _dot3_cr(x, m_hi, m_lo):
    x_hi, x_lo = _split2(x)
    d = functools.partial(jnp.dot, preferred_element_type=F32)
    return d(x_hi, m_hi) + (d(x_lo, m_hi) + d(x_hi, m_lo))


def _silu(x):
    return x / (1.0 + jnp.exp(-x))


def _head_rms(t, seg, g):
    hi, lo = _split2(t * t)
    ss = jnp.dot(hi, seg, preferred_element_type=F32) + jnp.dot(lo, seg, preferred_element_type=F32)
    return t * lax.rsqrt(ss * (1.0 / HEAD_DIM) + RMS_EPS) * g


def _hi_lo(a):
    a = jnp.asarray(a, F32)
    hi = a.astype(BF16)
    return hi, (a - hi.astype(F32)).astype(BF16)


def _dft_cos_sin(n):
    k = np.arange(n)
    ang = 2.0 * np.pi * ((k[:, None] * k[None, :]) % n) / n
    return np.cos(ang), np.sin(ang)


def _block_diag(m, reps):
    n = m.shape[0]
    out = np.zeros((n * reps, n * reps), m.dtype)
    for i in range(reps):
        out[i * n:(i + 1) * n, i * n:(i + 1) * n] = m
    return out


def _seg_ones():
    return jnp.asarray(_block_diag(np.ones((HEAD_DIM, HEAD_DIM)), N_HEADS), BF16)


def _mod_kernel(c_ref, w_ref, b_ref, o_ref):
    o_ref[0] = _bdot(_silu(c_ref[...]), w_ref[0]) + b_ref[0]


def _modulation(cvec, w_mod, b_mod):
    rows = cvec.shape[0]
    return pl.pallas_call(
        _mod_kernel,
        grid=(DEPTH, 3),
        in_specs=[pl.BlockSpec((rows, D_MODEL), lambda l, j: (0, 0)),
                  pl.BlockSpec((1, D_MODEL, D_MODEL), lambda l, j: (l, 0, j)),
                  pl.BlockSpec((1, 1, D_MODEL), lambda l, j: (l, 0, j))],
        out_specs=pl.BlockSpec((1, rows, D_MODEL), lambda l, j: (l, 0, j)),
        out_shape=jax.ShapeDtypeStruct((DEPTH, rows, 3 * D_MODEL), F32),
        compiler_params=_cparams("arbitrary", "arbitrary"),
        name="adaln_mod",
    )(cvec, w_mod, b_mod.reshape(DEPTH, 1, 3 * D_MODEL))


def _inproj_kernel(x_ref, mod_ref, g_ref, w_ref, qn_ref, kn_ref, seg_ref, p_ref, *kv_refs, h_scr):
    x = x_ref[0]
    ms = jnp.mean(x * x, axis=-1, keepdims=True)
    y = x * lax.rsqrt(ms + RMS_EPS) * g_ref[0]
    shift = mod_ref[0, :, 0:D_MODEL]
    scale = mod_ref[0, :, D_MODEL:2 * D_MODEL]
    h_scr[...] = (y * (1.0 + scale) + shift).astype(BF16)
    seg = seg_ref[...]
    for j in range(14):
        cs = slice(j * W_GRP, (j + 1) * W_GRP)
        t = jnp.dot(h_scr[...], w_ref[0, :, cs], preferred_element_type=F32)
        if j == G_QA:
            t = _head_rms(t, seg, qn_ref[0]) * (HEAD_DIM ** -0.5)
        elif j == G_KA:
            t = _head_rms(t, seg, kn_ref[0])
        p_ref[0, :, cs] = t
        if kv_refs and j in (G_KA, G_VA):
            ref = kv_refs[0] if j == G_KA else kv_refs[1]
            for h in range(N_HEADS):
                ref[0, h] = t[:, h * HEAD_DIM:(h + 1) * HEAD_DIM]
    cs = slice(14 * W_GRP, N_IN_PAD)
    p_ref[0, :, cs] = jnp.dot(h_scr[...], w_ref[0, :, cs], preferred_element_type=F32)


def _inproj(x, mod, layer, norm_g, w_in, qn, kn, seg, *, tm, with_kv):
    b, s, _ = x.shape
    per_batch = mod.shape[0] > 1
    out_shape = [jax.ShapeDtypeStruct((b, s, N_IN_PAD), F32)]
    out_specs = [pl.BlockSpec((1, tm, N_IN_PAD), lambda i, j: (i, j, 0))]
    if with_kv:
        for _ in range(2):
            out_shape.append(jax.ShapeDtypeStruct((b, N_HEADS, s, HEAD_DIM), F32))
            out_specs.append(pl.BlockSpec((1, N_HEADS, tm, HEAD_DIM), lambda i, j: (i, 0, j, 0)))
    def body(*refs):
        *io, h_scr = refs
        _inproj_kernel(*io, h_scr=h_scr)

    return pl.pallas_call(
        body,
        grid=(b, s // tm),
        in_specs=[pl.BlockSpec((1, tm, D_MODEL), lambda i, j: (i, j, 0)),
                  pl.BlockSpec((1, 1, 3 * D_MODEL), (lambda i, j: (i, 0, 0)) if per_batch else (lambda i, j: (0, 0, 0))),
                  pl.BlockSpec((1, 1, D_MODEL), lambda i, j: (layer, 0, 0)),
                  pl.BlockSpec((1, D_MODEL, N_IN_PAD), lambda i, j: (layer, 0, 0)),
                  pl.BlockSpec((1, 1, W_GRP), lambda i, j: (layer, 0, 0)),
                  pl.BlockSpec((1, 1, W_GRP), lambda i, j: (layer, 0, 0)),
                  pl.BlockSpec((W_GRP, W_GRP), lambda i, j: (0, 0))],
        out_specs=out_specs,
        out_shape=out_shape,
        scratch_shapes=[pltpu.VMEM((tm, D_MODEL), BF16)],
        compiler_params=_cparams("arbitrary", "arbitrary"),
        name="inproj_kv" if with_kv else "inproj",
    )(x, mod, norm_g, w_in, qn, kn, seg)


def _attn_ctx_kernel(q_ref, k_ref, v_ref, o_ref):
    for h in range(N_HEADS):
        hs = slice(h * HEAD_DIM, (h + 1) * HEAD_DIM)
        s = _bdot_nt(q_ref[0, :, hs], k_ref[0, :, hs])
        e = jnp.exp(s - jnp.max(s, axis=-1, keepdims=True))
        l = jnp.sum(e, axis=-1, keepdims=True)
        o_ref[0, :, hs] = _bdot(e, v_ref[0, :, hs]) / l


def _attn_ctx(p):
    b, s, _ = p.shape
    col = lambda g: pl.BlockSpec((1, s, W_GRP), lambda i, g=g: (i, 0, g))
    return pl.pallas_call(
        _attn_ctx_kernel,
        grid=(b,),
        in_specs=[col(G_QA), col(G_KA), col(G_VA)],
        out_specs=pl.BlockSpec((1, s, W_GRP), lambda i: (i, 0, 0)),
        out_shape=jax.ShapeDtypeStruct((b, s, W_GRP), F32),
        compiler_params=_cparams("arbitrary"),
        name="attn_ctx",
    )(p, p, p)


def _bias_kernel(rpb_ref, o_ref):
    l = pl.program_id(0)
    h = pl.program_id(1)
    dl = pl.program_id(2)
    q = lax.broadcasted_iota(jnp.int32, (GRID_W, GRID_W), 0)
    kc = lax.broadcasted_iota(jnp.int32, (GRID_W, GRID_W), 1)
    dc = jnp.clip(kc - q + (WIN_W - 1), 0, 2 * WIN_W - 2)
    cs = jnp.clip(q - WIN_W // 2, 0, GRID_W - WIN_W)
    in_win = jnp.where(kc >= cs, jnp.where(kc < cs + WIN_W, 1, 0), 0) == 1
    n_dr, n_dc = 2 * WIN_H - 1, 2 * WIN_W - 1
    for i in range(WIN_H):
        base = ((l * N_HEADS + h) * n_dr + dl + i) * n_dc
        t = jnp.zeros((GRID_W, GRID_W), F32)
        for d in range(n_dc):
            t = jnp.where(dc == d, rpb_ref[base + d], t)
        o_ref[0, 0, 0, :, i * GRID_W:(i + 1) * GRID_W] = jnp.where(in_win, t, NEG_BIG)


def _bias_tables(rpb):
    return pl.pallas_call(
        _bias_kernel,
        grid=(DEPTH, N_HEADS, WIN_H),
        in_specs=[pl.BlockSpec(memory_space=pltpu.SMEM)],
        out_specs=pl.BlockSpec((1, 1, 1, GRID_W, WIN_H * GRID_W), lambda l, h, d: (l, h, d, 0, 0)),
        out_shape=jax.ShapeDtypeStruct((DEPTH, N_HEADS, WIN_H, GRID_W, WIN_H * GRID_W), F32),
        compiler_params=_cparams("arbitrary", "arbitrary", "arbitrary"),
        name="rpb_tables",
    )(rpb.reshape(-1))


def _attn_nbr_kernel(q_ref, k_ref, v_ref, ck_ref, cv_ref, bias_ref, o_ref, *, rows_per_step):
    rb = pl.program_id(1)
    n_rows = k_ref.shape[1] // GRID_W
    kh = min(WIN_H, n_rows)
    n_lat = kh * GRID_W

    def row_body(j, carry):
        r = rb * rows_per_step + j
        rs = jnp.clip(r - kh // 2, 0, n_rows - kh)
        dl = rs - r + (WIN_H - 1)
        k0 = pl.multiple_of(rs * GRID_W, GRID_W)
        q0 = pl.multiple_of(j * GRID_W, GRID_W)
        for h in range(N_HEADS):
            hs = slice(h * HEAD_DIM, (h + 1) * HEAD_DIM)
            qh = q_ref[0, pl.ds(q0, GRID_W), hs]
            s_lat = _bdot_nt(qh, k_ref[0, pl.ds(k0, n_lat), hs]) + bias_ref[0, h, dl]
            s_ctx = _bdot_nt(qh, ck_ref[0, 0, h])
            m = jnp.maximum(jnp.max(s_lat, axis=-1, keepdims=True), jnp.max(s_ctx, axis=-1, keepdims=True))
            e_lat = jnp.exp(s_lat - m)
            e_ctx = jnp.exp(s_ctx - m)
            l = jnp.sum(e_lat, axis=-1, keepdims=True) + jnp.sum(e_ctx, axis=-1, keepdims=True)
            o = _bdot(e_lat, v_ref[0, pl.ds(k0, n_lat), hs]) + _bdot(e_ctx, cv_ref[0, 0, h])
            o_ref[0, pl.ds(q0, GRID_W), hs] = o / l
        return carry

    lax.fori_loop(0, rows_per_step, row_body, 0)


def _attn_nbr(p, cache_k, cache_v, bias, layer, *, rows_per_step=8):
    b, s, _ = p.shape
    tq = rows_per_step * GRID_W
    past = cache_k.shape[3]
    ctx_spec = pl.BlockSpec((1, 1, N_HEADS, past, HEAD_DIM), lambda i, j: (i, layer, 0, 0, 0))
    return pl.pallas_call(
        functools.partial(_attn_nbr_kernel, rows_per_step=rows_per_step),
        grid=(b, s // tq),
        in_specs=[pl.BlockSpec((1, tq, W_GRP), lambda i, j: (i, j, G_QA)),
                  pl.BlockSpec((1, s, W_GRP), lambda i, j: (i, 0, G_KA)),
                  pl.BlockSpec((1, s, W_GRP), lambda i, j: (i, 0, G_VA)),
                  ctx_spec, ctx_spec,
                  pl.BlockSpec((1, N_HEADS, WIN_H, GRID_W, WIN_H * GRID_W), lambda i, j: (layer, 0, 0, 0, 0))],
        out_specs=pl.BlockSpec((1, tq, W_GRP), lambda i, j: (i, j, 0)),
        out_shape=jax.ShapeDtypeStruct((b, s, W_GRP), F32),
        compiler_params=_cparams("arbitrary", "arbitrary"),
        name="attn_nbr",
    )(p, p, p, cache_k, cache_v, bias)


def _fft_ctx_consts(s):
    c, sn = _dft_cos_sin(s)
    cc, sc = _dft_cos_sin(W_GRP // N_FOURIER)
    m1 = np.concatenate([c, -sn], axis=0)
    m2 = np.concatenate([_block_diag(cc, N_FOURIER), _block_diag(sc, N_FOURIER)], axis=0)
    return _hi_lo(m1) + _hi_lo(m2)


def _fft_ctx_kernel(u_ref, m1h_ref, m1l_ref, m2h_ref, m2l_ref, o_ref, *, scale):
    s = u_ref.shape[1]
    t = _dot3_cl(m1h_ref[...], m1l_ref[...], u_ref[0])
    x = jnp.concatenate([t[:s], t[s:]], axis=1)
    o_ref[0] = _dot3_cr(x, m2h_ref[...], m2l_ref[...]) * scale


def _fft_ctx(p):
    b, s, _ = p.shape
    consts = _fft_ctx_consts(s)
    full = lambda a: pl.BlockSpec(a.shape, lambda i: (0,) * a.ndim)
    return pl.pallas_call(
        functools.partial(_fft_ctx_kernel, scale=1.0 / math.sqrt(s * (W_GRP // N_FOURIER))),
        grid=(b,),
        in_specs=[pl.BlockSpec((1, s, W_GRP), lambda i: (i, 0, G_UC))] + [full(a) for a in consts],
        out_specs=pl.BlockSpec((1, s, W_GRP), lambda i: (i, 0, 0)),
        out_shape=jax.ShapeDtypeStruct((b, s, W_GRP), F32),
        compiler_params=_cparams("arbitrary"),
        name="fft_ctx",
    )(p, *consts)


def _fft_grid_consts():
    n = GRID_W
    c, sn = _dft_cos_sin(n)
    cc, sc = _dft_cos_sin(W_GRP // N_FOURIER)
    m1 = np.concatenate([c, -sn], axis=0)
    m2 = np.block([[c, sn], [-sn, c]])
    reps = LANES // (W_GRP // N_FOURIER)
    m3 = np.concatenate([_block_diag(cc, reps), _block_diag(sc, reps)], axis=0)
    k = np.arange(n)
    ang = 2.0 * np.pi * (k[:, None] * k[None, :]) / (n * n)
    twr = np.broadcast_to(np.cos(ang)[:, :, None], (n, n, LANES))
    twi = np.broadcast_to(-np.sin(ang)[:, :, None], (n, n, LANES))
    return _hi_lo(m1) + _hi_lo(m2) + _hi_lo(m3) + (jnp.asarray(twr, F32), jnp.asarray(twi, F32))


def _fft_grid_kernel(u_ref, m1h_ref, m1l_ref, m2h_ref, m2l_ref, m3h_ref, m3l_ref, twr_ref, twi_ref,
                     o_ref, tr_scr, ti_scr, *, scale):
    n = GRID_W

    def stage1(b, carry):
        a_rows = u_ref[0, pl.ds(b, n, stride=n), :]
        t = _dot3_cl(m1h_ref[...], m1l_ref[...], a_rows)
        tr, ti = t[:n], t[n:]
        wr, wi = twr_ref[b], twi_ref[b]
        tr_scr[pl.ds(b, n, stride=n), :] = tr * wr - ti * wi
        ti_scr[pl.ds(b, n, stride=n), :] = tr * wi + ti * wr
        return carry

    lax.fori_loop(0, n, stage1, 0)

    def stage2(d, carry):
        r0 = pl.multiple_of(d * n, n)
        g = jnp.concatenate([tr_scr[pl.ds(r0, n), :], ti_scr[pl.ds(r0, n), :]], axis=0)
        x = _dot3_cl(m2h_ref[...], m2l_ref[...], g)
        xc = jnp.concatenate([x[:n], x[n:]], axis=1)
        o_ref[0, pl.ds(d, n, stride=n), :] = _dot3_cr(xc, m3h_ref[...], m3l_ref[...]) * scale
        return carry

    lax.fori_loop(0, n, stage2, 0)


def _fft_grid(p):
    b, s, _ = p.shape
    assert s == GRID_W * GRID_W
    consts = _fft_grid_consts()
    full = lambda a: pl.BlockSpec(a.shape, lambda i, j: (0,) * a.ndim)
    blk0 = G_UC * W_GRP // LANES
    return pl.pallas_call(
        functools.partial(_fft_grid_kernel, scale=1.0 / math.sqrt(s * (W_GRP // N_FOURIER))),
        grid=(b, W_GRP // LANES),
        in_specs=[pl.BlockSpec((1, s, LANES), lambda i, j: (i, 0, blk0 + j))] + [full(a) for a in consts],
        out_specs=pl.BlockSpec((1, s, LANES), lambda i, j: (i, 0, j)),
        out_shape=jax.ShapeDtypeStruct((b, s, W_GRP), F32),
        scratch_shapes=[pltpu.VMEM((s, LANES), F32), pltpu.VMEM((s, LANES), F32)],
        compiler_params=_cparams("arbitrary", "arbitrary"),
        name="fft_grid",
    )(p, *consts)


def _gla_kernel(*refs, n_chunks, zero_init):
    if zero_init:
        (q_ref, k_ref, v_ref, a_ref, upf_ref, bf_ref, upb_ref, bb_ref, gn_ref, seg_ref,
         o_ref, sfin_ref) = refs
        s0_ref = None
    else:
        (q_ref, k_ref, v_ref, a_ref, upf_ref, bf_ref, upb_ref, bb_ref, gn_ref, seg_ref, s0_ref,
         o_ref, sfin_ref) = refs
    c = CHUNK
    row = lax.broadcasted_iota(jnp.int32, (c, c), 0)
    col = lax.broadcasted_iota(jnp.int32, (c, c), 1)
    eye = jnp.where(row == col, 1.0, 0.0).astype(BF16)

    def transpose_exact(x):
        return sum(_bdot_nt(eye, part) for part in _split3(x))

    def chunk(n, st, direction):
        fwd = direction == 0
        keep = (row >= col) if fwd else (row <= col)
        tri = jnp.where(keep, 1.0, 0.0).astype(BF16)
        up_ref, b_ref = (upf_ref, bf_ref) if fwd else (upb_ref, bb_ref)
        r0 = pl.multiple_of(n * c, c)
        a = a_ref[0, pl.ds(r0, c), :]
        a = a[:, 0:GATE_RANK] if fwd else a[:, GATE_RANK:2 * GATE_RANK]
        x = _bdot(a, up_ref[0]) + b_ref[0]
        la = (jnp.minimum(x, 0.0) - jnp.log(1.0 + jnp.exp(-jnp.abs(x)))) * (1.0 / GLA_TAU)
        cum = sum(jnp.dot(tri, part, preferred_element_type=F32) for part in _split3(la))
        total = cum[c - 1:c, :] if fwd else cum[0:1, :]
        q = q_ref[0, pl.ds(r0, c), :] * (HEAD_DIM ** -0.5)
        k = k_ref[0, pl.ds(r0, c), :]
        v = v_ref[0, pl.ds(r0, c), :]
        q_dec = q * jnp.exp(cum)
        k_inv = k * jnp.exp(-cum)
        k_end = k * jnp.exp(total - cum)
        decay = jnp.exp(total)
        outs, new_st = [], []
        for h in range(N_HEADS):
            hs = slice(h * HEAD_DIM, (h + 1) * HEAD_DIM)
            att = jnp.where(keep, _bdot_nt(q_dec[:, hs], k_inv[:, hs]), 0.0)
            outs.append(_bdot(att, v[:, hs]) + _bdot_nt(q_dec[:, hs], st[h]))
            v_t = _bdot_nt(eye, v[:, hs])
            new_st.append(st[h] * decay[:, hs] + _bdot(v_t, k_end[:, hs]))
        return jnp.concatenate(outs, axis=1), tuple(new_st)

    def init_state(direction):
        if zero_init:
            return tuple(jnp.zeros((HEAD_DIM, HEAD_DIM), F32) for _ in range(N_HEADS))
        return tuple(transpose_exact(s0_ref[0, 0, direction, h]) for h in range(N_HEADS))

    def fwd_body(i, st):
        o, st = chunk(i, st, 0)
        o_ref[0, pl.ds(pl.multiple_of(i * c, c), c), :] = o
        return st

    st_f = lax.fori_loop(0, n_chunks, fwd_body, init_state(0))

    def bwd_body(i, st):
        n = n_chunks - 1 - i
        o_b, st = chunk(n, st, 1)
        rows = pl.ds(pl.multiple_of(n * c, c), c)
        o_ref[0, rows, :] = _head_rms(o_ref[0, rows, :] + o_b, seg_ref[...], gn_ref[0])
        return st

    st_b = lax.fori_loop(0, n_chunks, bwd_body, init_state(1))
    for h in range(N_HEADS):
        sfin_ref[0, 0, h] = transpose_exact(st_f[h])
        sfin_ref[0, 1, h] = transpose_exact(st_b[h])


def _gla(p, up_f, b_f, up_b, b_b, gn, seg, layer, s0):
    b, s, _ = p.shape
    zero_init = s0 is None
    col = lambda g: pl.BlockSpec((1, s, W_GRP), lambda i, g=g: (i, 0, g))
    lay = lambda shape: pl.BlockSpec((1,) + shape, lambda i: (layer,) + (0,) * len(shape))
    in_specs = [col(G_QD), col(G_KD), col(G_VD),
                pl.BlockSpec((1, s, LANES), lambda i: (i, 0, GATE_BLK)),
                lay((GATE_RANK, W_GRP)), lay((1, W_GRP)), lay((GATE_RANK, W_GRP)), lay((1, W_GRP)),
                lay((1, W_GRP)), pl.BlockSpec((W_GRP, W_GRP), lambda i: (0, 0))]
    args = [p, p, p, p, up_f, b_f, up_b, b_b, gn, seg]
    if not zero_init:
        in_specs.append(pl.BlockSpec((1, 1, 2, N_HEADS, HEAD_DIM, HEAD_DIM), lambda i: (i, layer, 0, 0, 0, 0)))
        args.append(s0)
    return pl.pallas_call(
        functools.partial(_gla_kernel, n_chunks=s // CHUNK, zero_init=zero_init),
        grid=(b,),
        in_specs=in_specs,
        out_specs=[pl.BlockSpec((1, s, W_GRP), lambda i: (i, 0, 0)),
                   pl.BlockSpec((1, 2, N_HEADS, HEAD_DIM, HEAD_DIM), lambda i: (i, 0, 0, 0, 0))],
        out_shape=[jax.ShapeDtypeStruct((b, s, W_GRP), F32),
                   jax.ShapeDtypeStruct((b, 2, N_HEADS, HEAD_DIM, HEAD_DIM), F32)],
        compiler_params=_cparams("arbitrary"),
        name="gla_zero" if zero_init else "gla",
    )(*args)


def _outproj_kernel(x_ref, mod_ref, oa_ref, oc_ref, od_ref, ga_ref, bb_ref, cb_ref, hb_ref, gb_ref,
                    gc_ref, gd_ref, cprev_ref, hprev_ref, cnext_ref, hnext_ref, cw_ref, w_ref, o_ref):
    j = pl.program_id(1)
    last = pl.num_programs(1) - 1
    tm = x_ref.shape[1]
    u = cb_ref[0] * hb_ref[0]
    u_prev_row = jnp.where(j == 0, 0.0, cprev_ref[0, 7:8, :] * hprev_ref[0, 7:8, :])
    u_next_row = jnp.where(j == last, 0.0, cnext_ref[0, 0:1, :] * hnext_ref[0, 0:1, :])
    rid = lax.broadcasted_iota(jnp.int32, u.shape, 0)
    u_prev = jnp.where(rid == 0, u_prev_row, pltpu.roll(u, 1, axis=0))
    u_next = jnp.where(rid == tm - 1, u_next_row, pltpu.roll(u, tm - 1, axis=0))
    cw = cw_ref[0]
    conv = u_prev * cw[0:1, :] + u * cw[1:2, :] + u_next * cw[2:3, :]
    y_a = oa_ref[0] * _silu(ga_ref[0])
    y_b = bb_ref[0] * conv * _silu(gb_ref[0])
    y_c = oc_ref[0] * _silu(gc_ref[0])
    y_d = od_ref[0] * _silu(gd_ref[0])
    y = jnp.concatenate([y_a, y_b, y_c, y_d], axis=1).astype(BF16)
    gate = mod_ref[0, :, 2 * D_MODEL:3 * D_MODEL]
    o_ref[0] = x_ref[0] + gate * jnp.dot(y, w_ref[0], preferred_element_type=F32)


def _outproj(x, mod, layer, o_a, o_c, o_d, p, conv_w, w_out, *, tm):
    b, s, _ = x.shape
    per_batch = mod.shape[0] > 1
    tile = lambda width: pl.BlockSpec((1, tm, width), lambda i, j: (i, j, 0))
    col = lambda g: pl.BlockSpec((1, tm, W_GRP), lambda i, j, g=g: (i, j, g))
    t8 = tm // 8
    n8 = s // 8
    prev = lambda g: pl.BlockSpec((1, 8, W_GRP), lambda i, j, g=g: (i, jnp.maximum(j * t8 - 1, 0), g))
    nxt = lambda g: pl.BlockSpec((1, 8, W_GRP), lambda i, j, g=g: (i, jnp.minimum((j + 1) * t8, n8 - 1), g))
    return pl.pallas_call(
        _outproj_kernel,
        grid=(b, s // tm),
        in_specs=[tile(D_MODEL),
                  pl.BlockSpec((1, 1, 3 * D_MODEL), (lambda i, j: (i, 0, 0)) if per_batch else (lambda i, j: (0, 0, 0))),
                  tile(W_GRP), tile(W_GRP), tile(W_GRP),
                  col(G_GA), col(G_BB), col(G_CB), col(G_HB), col(G_GB), col(G_GC), col(G_GD),
                  prev(G_CB), prev(G_HB), nxt(G_CB), nxt(G_HB),
                  pl.BlockSpec((1, 3, W_GRP), lambda i, j: (layer, 0, 0)),
                  pl.BlockSpec((1, D_MODEL, D_MODEL), lambda i, j: (layer, 0, 0))],
        out_specs=tile(D_MODEL),
        out_shape=jax.ShapeDtypeStruct((b, s, D_MODEL), F32),
        compiler_params=_cparams("arbitrary", "arbitrary"),
        name="outproj",
    )(x, mod, o_a, o_c, o_d, p, p, p, p, p, p, p, p, p, p, p, conv_w, w_out)


def kernel(x_prompt, x_sample, cache_k, cache_v, state_gla, c, c_ctx, norm_g, w_mod, b_mod, w_in, q_norm_g,
           k_norm_g, rpb, conv_w, gla_up_f, gla_bias_f, gla_up_b, gla_bias_b, gla_norm_g, w_out):
    dec_batch = c.shape[0]
    pad_rows = 8 - (1 + dec_batch) % 8
    cvec = jnp.concatenate([c_ctx[None], c, jnp.zeros((pad_rows, D_MODEL), F32)], axis=0)
    mod = _modulation(cvec, w_mod, b_mod)
    mod_ctx = mod[:, 0:1].reshape(DEPTH, 1, 1, 3 * D_MODEL)
    mod_smp = mod[:, 1:1 + dec_batch].reshape(DEPTH, dec_batch, 1, 3 * D_MODEL)

    w_in_b = jnp.pad(w_in, ((0, 0), (0, 0), (0, N_IN_PAD - N_IN))).astype(BF16)
    w_out_b = w_out.astype(BF16)
    seg = _seg_ones()
    norm_g3 = norm_g.reshape(DEPTH, 1, D_MODEL)
    qn = jnp.tile(q_norm_g, (1, N_HEADS)).reshape(DEPTH, 1, W_GRP)
    kn = jnp.tile(k_norm_g, (1, N_HEADS)).reshape(DEPTH, 1, W_GRP)
    gn = jnp.tile(gla_norm_g, (1, N_HEADS)).reshape(DEPTH, 1, W_GRP)
    b_f = gla_bias_f.reshape(DEPTH, 1, W_GRP)
    b_b = gla_bias_b.reshape(DEPTH, 1, W_GRP)
    bias = _bias_tables(rpb)

    x = x_prompt
    ks, vs, ss = [], [], []
    for l in range(DEPTH):
        p, k_new, v_new = _inproj(x, mod_ctx[l], l, norm_g3, w_in_b, qn, kn, seg, tm=x.shape[1], with_kv=True)
        o_a = _attn_ctx(p)
        o_c = _fft_ctx(p)
        o_d, s_fin = _gla(p, gla_up_f, b_f, gla_up_b, b_b, gn, seg, l, None)
        x = _outproj(x, mod_ctx[l], l, o_a, o_c, o_d, p, conv_w, w_out_b, tm=x.shape[1])
        ks.append(k_new)
        vs.append(v_new)
        ss.append(s_fin)
    y_prompt = x
    new_cache_k = jnp.stack(ks, axis=1)
    new_cache_v = jnp.stack(vs, axis=1)
    new_state = jnp.stack(ss, axis=1)

    xs = x_sample
    for l in range(DEPTH):
        (p,) = _inproj(xs, mod_smp[l], l, norm_g3, w_in_b, qn, kn, seg, tm=512, with_kv=False)
        o_a = _attn_nbr(p, cache_k, cache_v, bias, l)
        o_c = _fft_grid(p)
        o_d, _ = _gla(p, gla_up_f, b_f, gla_up_b, b_b, gn, seg, l, state_gla)
        xs = _outproj(xs, mod_smp[l], l, o_a, o_c, o_d, p, conv_w, w_out_b, tm=512)
    return (y_prompt, xs, new_cache_k, new_cache_v, new_state)
```

```python
import functools
import math

import numpy as np
import jax
import jax.numpy as jnp
from jax import lax
from jax.experimental import pallas as pl
from jax.experimental.pallas import tpu as pltpu

F32 = jnp.float32
BF16 = jnp.bfloat16

D_MODEL = 1024
DEPTH = 4
GRID_W = 64
W_GRP = 256
HEAD_DIM = 64
N_HEADS = 4
WIN_H = 8
WIN_W = 16
N_FOURIER = 4
GATE_RANK = 16
GLA_TAU = 16.0
CHUNK = 64
RMS_EPS = 1e-6
N_IN = 14 * W_GRP + 2 * GATE_RANK
LANES = 128
N_IN_PAD = -(-N_IN // LANES) * LANES
G_QA, G_KA, G_VA, G_GA, G_BB, G_CB, G_HB, G_GB, G_UC, G_GC, G_QD, G_KD, G_VD, G_GD = range(14)
GATE_BLK = 14 * W_GRP // LANES
NEG_BIG = -1e30
VMEM_LIMIT = 56 * 1024 * 1024


def _cparams(*sem):
    return pltpu.CompilerParams(dimension_semantics=sem, vmem_limit_bytes=VMEM_LIMIT)


def _bdot(a, b):
    return jnp.dot(a.astype(BF16), b.astype(BF16), preferred_element_type=F32)


def _bdot_nt(a, b):
    return lax.dot_general(a.astype(BF16), b.astype(BF16), (((1,), (1,)), ((), ())),
                           preferred_element_type=F32)


def _split2(x):
    hi = x.astype(BF16)
    lo = (x - hi.astype(F32)).astype(BF16)
    return hi, lo


def _split3(x):
    h1 = x.astype(BF16)
    r = x - h1.astype(F32)
    h2 = r.astype(BF16)
    h3 = (r - h2.astype(F32)).astype(BF16)
    return h1, h2, h3


def _dot3_cl(m_hi, m_lo, x):
    x_hi, x_lo = _split2(x)
    d = functools.partial(jnp.dot, preferred_element_type=F32)
    return d(m_hi, x_hi) + (d(m_hi, x_lo) + d(m_lo, x_hi))


def _dot3_cr(x, m_hi, m_lo):
    x_hi, x_lo = _split2(x)
    d = functools.partial(jnp.dot, preferred_element_type=F32)
    return d(x_hi, m_hi) + (d(x_lo, m_hi) + d(x_hi, m_lo))


def _silu(x):
    return x / (1.0 + jnp.exp(-x))


def _head_rms(t, seg, g):
    n = t.shape[0]
    both = jnp.dot(jnp.concatenate(_split2(t * t), axis=0), seg, preferred_element_type=F32)
    ss = both[:n] + both[n:]
    return t * lax.rsqrt(ss * (1.0 / HEAD_DIM) + RMS_EPS) * g


def _hi_lo(a):
    a = jnp.asarray(a, F32)
    hi = a.astype(BF16)
    return hi, (a - hi.astype(F32)).astype(BF16)


def _dft_cos_sin(n):
    k = np.arange(n)
    ang = 2.0 * np.pi * ((k[:, None] * k[None, :]) % n) / n
    return np.cos(ang), np.sin(ang)


def _block_diag(m, reps):
    n = m.shape[0]
    out = np.zeros((n * reps, n * reps), m.dtype)
    for i in range(reps):
        out[i * n:(i + 1) * n, i * n:(i + 1) * n] = m
    return out


def _seg_ones():
    return jnp.asarray(_block_diag(np.ones((HEAD_DIM, HEAD_DIM)), N_HEADS), BF16)


def _mod_kernel(c_ref, w_ref, b_ref, o_ref):
    o_ref[0] = _bdot(_silu(c_ref[...]), w_ref[0]) + b_ref[0]


def _modulation(cvec, w_mod, b_mod):
    rows = cvec.shape[0]
    return pl.pallas_call(
        _mod_kernel,
        grid=(DEPTH, 3),
        in_specs=[pl.BlockSpec((rows, D_MODEL), lambda l, j: (0, 0)),
                  pl.BlockSpec((1, D_MODEL, D_MODEL), lambda l, j: (l, 0, j)),
                  pl.BlockSpec((1, 1, D_MODEL), lambda l, j: (l, 0, j))],
        out_specs=pl.BlockSpec((1, rows, D_MODEL), lambda l, j: (l, 0, j)),
        out_shape=jax.ShapeDtypeStruct((DEPTH, rows, 3 * D_MODEL), F32),
        compiler_params=_cparams("arbitrary", "arbitrary"),
        name="adaln_mod",
    )(cvec, w_mod, b_mod.reshape(DEPTH, 1, 3 * D_MODEL))


def _inproj_kernel(x_ref, mod_ref, g_ref, w_ref, qn_ref, kn_ref, seg_ref, p_ref, k_ref, v_ref, h_scr, *, head_major):
    x = x_ref[0]
    ms = jnp.mean(x * x, axis=-1, keepdims=True)
    y = x * lax.rsqrt(ms + RMS_EPS) * g_ref[0]
    shift = mod_ref[0, :, 0:D_MODEL]
    scale = mod_ref[0, :, D_MODEL:2 * D_MODEL]
    h_scr[...] = (y * (1.0 + scale) + shift).astype(BF16)
    seg = seg_ref[...]
    for j in range(14):
        cs = slice(j * W_GRP, (j + 1) * W_GRP)
        t = jnp.dot(h_scr[...], w_ref[0, :, cs], preferred_element_type=F32)
        if j == G_QA:
            t = _head_rms(t, seg, qn_ref[0]) * (HEAD_DIM ** -0.5)
        elif j == G_KA:
            t = _head_rms(t, seg, kn_ref[0])
        p_ref[0, :, cs] = t
        if j in (G_KA, G_VA):
            ref = k_ref if j == G_KA else v_ref
            if head_major:
                for h in range(N_HEADS):
                    ref[0, h] = t[:, h * HEAD_DIM:(h + 1) * HEAD_DIM]
            else:
                ref[0] = t.astype(ref.dtype)
    cs = slice(14 * W_GRP, N_IN_PAD)
    p_ref[0, :, cs] = jnp.dot(h_scr[...], w_ref[0, :, cs], preferred_element_type=F32)


def _inproj(x, mod, layer, norm_g, w_in, qn, kn, seg, *, tm, head_major):
    b, s, _ = x.shape
    per_batch = mod.shape[0] > 1
    out_shape = [jax.ShapeDtypeStruct((b, s, N_IN_PAD), F32)]
    out_specs = [pl.BlockSpec((1, tm, N_IN_PAD), lambda i, j: (i, j, 0))]
    for _ in range(2):
        if head_major:
            out_shape.append(jax.ShapeDtypeStruct((b, N_HEADS, s, HEAD_DIM), F32))
            out_specs.append(pl.BlockSpec((1, N_HEADS, tm, HEAD_DIM), lambda i, j: (i, 0, j, 0)))
        else:
            out_shape.append(jax.ShapeDtypeStruct((b, s, W_GRP), BF16))
            out_specs.append(pl.BlockSpec((1, tm, W_GRP), lambda i, j: (i, j, 0)))
    return pl.pallas_call(
        functools.partial(_inproj_kernel, head_major=head_major),
        grid=(b, s // tm),
        in_specs=[pl.BlockSpec((1, tm, D_MODEL), lambda i, j: (i, j, 0)),
                  pl.BlockSpec((1, 1, 3 * D_MODEL), (lambda i, j: (i, 0, 0)) if per_batch else (lambda i, j: (0, 0, 0))),
                  pl.BlockSpec((1, 1, D_MODEL), lambda i, j: (layer, 0, 0)),
                  pl.BlockSpec((1, D_MODEL, N_IN_PAD), lambda i, j: (layer, 0, 0)),
                  pl.BlockSpec((1, 1, W_GRP), lambda i, j: (layer, 0, 0)),
                  pl.BlockSpec((1, 1, W_GRP), lambda i, j: (layer, 0, 0)),
                  pl.BlockSpec((W_GRP, W_GRP), lambda i, j: (0, 0))],
        out_specs=out_specs,
        out_shape=out_shape,
        scratch_shapes=[pltpu.VMEM((tm, D_MODEL), BF16)],
        compiler_params=_cparams("arbitrary", "arbitrary"),
        name="inproj_ctx" if head_major else "inproj_grid",
    )(x, mod, norm_g, w_in, qn, kn, seg)


def _attn_ctx_kernel(q_ref, k_ref, v_ref, o_ref):
    for h in range(N_HEADS):
        hs = slice(h * HEAD_DIM, (h + 1) * HEAD_DIM)
        s = _bdot_nt(q_ref[0, :, hs], k_ref[0, :, hs])
        e = jnp.exp(s - jnp.max(s, axis=-1, keepdims=True))
        l = jnp.sum(e, axis=-1, keepdims=True)
        o_ref[0, :, hs] = _bdot(e, v_ref[0, :, hs]) / l


def _attn_ctx(p):
    b, s, _ = p.shape
    col = lambda g: pl.BlockSpec((1, s, W_GRP), lambda i, g=g: (i, 0, g))
    return pl.pallas_call(
        _attn_ctx_kernel,
        grid=(b,),
        in_specs=[col(G_QA), col(G_KA), col(G_VA)],
        out_specs=pl.BlockSpec((1, s, W_GRP), lambda i: (i, 0, 0)),
        out_shape=jax.ShapeDtypeStruct((b, s, W_GRP), F32),
        compiler_params=_cparams("arbitrary"),
        name="attn_ctx",
    )(p, p, p)


def _bias_kernel(rpb_ref, o_ref):
    l = pl.program_id(0)
    h = pl.program_id(1)
    dl = pl.program_id(2)
    q = lax.broadcasted_iota(jnp.int32, (GRID_W, GRID_W), 0)
    kc = lax.broadcasted_iota(jnp.int32, (GRID_W, GRID_W), 1)
    dc = jnp.clip(kc - q + (WIN_W - 1), 0, 2 * WIN_W - 2)
    cs = jnp.clip(q - WIN_W // 2, 0, GRID_W - WIN_W)
    in_win = jnp.where(kc >= cs, jnp.where(kc < cs + WIN_W, 1, 0), 0) == 1
    n_dr, n_dc = 2 * WIN_H - 1, 2 * WIN_W - 1
    for i in range(WIN_H):
        base = ((l * N_HEADS + h) * n_dr + dl + i) * n_dc
        t = jnp.zeros((GRID_W, GRID_W), F32)
        for d in range(n_dc):
            t = jnp.where(dc == d, rpb_ref[base + d], t)
        o_ref[0, 0, 0, :, i * GRID_W:(i + 1) * GRID_W] = jnp.where(in_win, t, NEG_BIG)


def _bias_tables(rpb):
    return pl.pallas_call(
        _bias_kernel,
        grid=(DEPTH, N_HEADS, WIN_H),
        in_specs=[pl.BlockSpec(memory_space=pltpu.SMEM)],
        out_specs=pl.BlockSpec((1, 1, 1, GRID_W, WIN_H * GRID_W), lambda l, h, d: (l, h, d, 0, 0)),
        out_shape=jax.ShapeDtypeStruct((DEPTH, N_HEADS, WIN_H, GRID_W, WIN_H * GRID_W), F32),
        compiler_params=_cparams("arbitrary", "arbitrary", "arbitrary"),
        name="rpb_tables",
    )(rpb.reshape(-1))


def _attn_nbr_kernel(q_ref, k_ref, v_ref, ck_ref, cv_ref, bias_ref, o_ref, *, rows_per_step):
    rb = pl.program_id(1)
    n_rows = k_ref.shape[1] // GRID_W
    kh = min(WIN_H, n_rows)
    n_lat = kh * GRID_W

    def row_body(j, carry):
        r = rb * rows_per_step + j
        rs = jnp.clip(r - kh // 2, 0, n_rows - kh)
        dl = rs - r + (WIN_H - 1)
        k0 = pl.multiple_of(rs * GRID_W, GRID_W)
        q0 = pl.multiple_of(j * GRID_W, GRID_W)
        heads = [slice(h * HEAD_DIM, (h + 1) * HEAD_DIM) for h in range(N_HEADS)]
        q = q_ref[0, pl.ds(q0, GRID_W), :].astype(BF16)
        kw = k_ref[0, pl.ds(k0, n_lat), :]
        vw = v_ref[0, pl.ds(k0, n_lat), :]
        s_lat = [_bdot_nt(q[:, hs], kw[:, hs]) + bias_ref[0, h, dl] for h, hs in enumerate(heads)]
        s_ctx = [_bdot_nt(q[:, hs], ck_ref[0, 0, h]) for h, hs in enumerate(heads)]
        m = [jnp.maximum(jnp.max(a, axis=-1, keepdims=True), jnp.max(c, axis=-1, keepdims=True))
             for a, c in zip(s_lat, s_ctx)]
        e_lat = [jnp.exp(a - mh) for a, mh in zip(s_lat, m)]
        e_ctx = [jnp.exp(c - mh) for c, mh in zip(s_ctx, m)]
        l = [jnp.sum(a, axis=-1, keepdims=True) + jnp.sum(c, axis=-1, keepdims=True) for a, c in zip(e_lat, e_ctx)]
        o = [_bdot(e_lat[h], vw[:, hs]) + _bdot(e_ctx[h], cv_ref[0, 0, h]) for h, hs in enumerate(heads)]
        o_ref[0, pl.ds(q0, GRID_W), :] = jnp.concatenate([oh / lh for oh, lh in zip(o, l)], axis=1)
        return carry

    lax.fori_loop(0, rows_per_step, row_body, 0, unroll=2)


def _attn_nbr(p, k_tok, v_tok, cache_k, cache_v, bias, layer, *, rows_per_step=8):
    b, s, _ = p.shape
    tq = rows_per_step * GRID_W
    past = cache_k.shape[3]
    ctx_spec = pl.BlockSpec((1, 1, N_HEADS, past, HEAD_DIM), lambda i, j: (i, layer, 0, 0, 0))
    return pl.pallas_call(
        functools.partial(_attn_nbr_kernel, rows_per_step=rows_per_step),
        grid=(b, s // tq),
        in_specs=[pl.BlockSpec((1, tq, W_GRP), lambda i, j: (i, j, G_QA)),
                  pl.BlockSpec((1, s, W_GRP), lambda i, j: (i, 0, 0)),
                  pl.BlockSpec((1, s, W_GRP), lambda i, j: (i, 0, 0)),
                  ctx_spec, ctx_spec,
                  pl.BlockSpec((1, N_HEADS, WIN_H, GRID_W, WIN_H * GRID_W), lambda i, j: (layer, 0, 0, 0, 0))],
        out_specs=pl.BlockSpec((1, tq, W_GRP), lambda i, j: (i, j, 0)),
        out_shape=jax.ShapeDtypeStruct((b, s, W_GRP), F32),
        compiler_params=_cparams("arbitrary", "arbitrary"),
        name="attn_nbr",
    )(p, k_tok, v_tok, cache_k, cache_v, bias)


def _fft_ctx_consts(s):
    c, sn = _dft_cos_sin(s)
    cc, sc = _dft_cos_sin(W_GRP // N_FOURIER)
    m1 = np.concatenate([c, -sn], axis=0)
    m2 = np.concatenate([_block_diag(cc, N_FOURIER), _block_diag(sc, N_FOURIER)], axis=0)
    return _hi_lo(m1) + _hi_lo(m2)


def _fft_ctx_kernel(u_ref, m1h_ref, m1l_ref, m2h_ref, m2l_ref, o_ref, *, scale):
    s = u_ref.shape[1]
    t = _dot3_cl(m1h_ref[...], m1l_ref[...], u_ref[0])
    x = jnp.concatenate([t[:s], t[s:]], axis=1)
    o_ref[0] = _dot3_cr(x, m2h_ref[...], m2l_ref[...]) * scale


def _fft_ctx(p):
    b, s, _ = p.shape
    consts = _fft_ctx_consts(s)
    full = lambda a: pl.BlockSpec(a.shape, lambda i: (0,) * a.ndim)
    return pl.pallas_call(
        functools.partial(_fft_ctx_kernel, scale=1.0 / math.sqrt(s * (W_GRP // N_FOURIER))),
        grid=(b,),
        in_specs=[pl.BlockSpec((1, s, W_GRP), lambda i: (i, 0, G_UC))] + [full(a) for a in consts],
        out_specs=pl.BlockSpec((1, s, W_GRP), lambda i: (i, 0, 0)),
        out_shape=jax.ShapeDtypeStruct((b, s, W_GRP), F32),
        compiler_params=_cparams("arbitrary"),
        name="fft_ctx",
    )(p, *consts)


def _fft_grid_consts():
    n = GRID_W
    c, sn = _dft_cos_sin(n)
    cc, sc = _dft_cos_sin(W_GRP // N_FOURIER)
    m1 = np.concatenate([c, -sn], axis=0)
    m2 = np.block([[c, sn], [-sn, c]])
    reps = LANES // (W_GRP // N_FOURIER)
    m3 = np.concatenate([_block_diag(cc, reps), _block_diag(sc, reps)], axis=0)
    k = np.arange(n)
    ang = 2.0 * np.pi * (k[:, None] * k[None, :]) / (n * n)
    twr = np.broadcast_to(np.cos(ang)[:, :, None], (n, n, LANES))
    twi = np.broadcast_to(-np.sin(ang)[:, :, None], (n, n, LANES))
    return _hi_lo(m1) + _hi_lo(m2) + _hi_lo(m3) + (jnp.asarray(twr, F32), jnp.asarray(twi, F32))


FFT_UNROLL = 8
FFT_PITCH = 72


def _fft_grid_kernel(u_ref, m1h_ref, m1l_ref, m2h_ref, m2l_ref, m3h_ref, m3l_ref, twr_ref, twi_ref,
                     o_ref, tr_scr, ti_scr, *, scale):
    n = GRID_W

    def stage1(b, carry):
        a_rows = u_ref[0, pl.ds(b, n, stride=n), :]
        t = _dot3_cl(m1h_ref[...], m1l_ref[...], a_rows)
        tr, ti = t[:n], t[n:]
        wr, wi = twr_ref[b], twi_ref[b]
        tr_scr[pl.ds(b, n, stride=FFT_PITCH), :] = tr * wr - ti * wi
        ti_scr[pl.ds(b, n, stride=FFT_PITCH), :] = tr * wi + ti * wr
        return carry

    lax.fori_loop(0, n, stage1, 0, unroll=FFT_UNROLL)

    def stage2(d, carry):
        r0 = pl.multiple_of(d * FFT_PITCH, 8)
        g = jnp.concatenate([tr_scr[pl.ds(r0, n), :], ti_scr[pl.ds(r0, n), :]], axis=0)
        x = _dot3_cl(m2h_ref[...], m2l_ref[...], g)
        xc = jnp.concatenate([x[:n], x[n:]], axis=1)
        o_ref[0, pl.ds(d, n, stride=n), :] = _dot3_cr(xc, m3h_ref[...], m3l_ref[...]) * scale
        return carry

    lax.fori_loop(0, n, stage2, 0, unroll=FFT_UNROLL)


def _fft_grid(p):
    b, s, _ = p.shape
    assert s == GRID_W * GRID_W
    consts = _fft_grid_consts()
    full = lambda a: pl.BlockSpec(a.shape, lambda i, j: (0,) * a.ndim)
    blk0 = G_UC * W_GRP // LANES
    return pl.pallas_call(
        functools.partial(_fft_grid_kernel, scale=1.0 / math.sqrt(s * (W_GRP // N_FOURIER))),
        grid=(b, W_GRP // LANES),
        in_specs=[pl.BlockSpec((1, s, LANES), lambda i, j: (i, 0, blk0 + j))] + [full(a) for a in consts],
        out_specs=pl.BlockSpec((1, s, LANES), lambda i, j: (i, 0, j)),
        out_shape=jax.ShapeDtypeStruct((b, s, W_GRP), F32),
        scratch_shapes=[pltpu.VMEM((GRID_W * FFT_PITCH, LANES), F32)] * 2,
        compiler_params=_cparams("arbitrary", "arbitrary"),
        name="fft_grid",
    )(p, *consts)


GLA_GROUP = 4


def _gla_kernel(*refs, n_chunks, zero_init):
    if zero_init:
        (q_ref, k_ref, v_ref, a_ref, upf_ref, bf_ref, upb_ref, bb_ref, gn_ref, seg_ref,
         o_ref, sfin_ref, qd_scr, st_scr, dec_scr) = refs
        s0_ref = None
    else:
        (q_ref, k_ref, v_ref, a_ref, upf_ref, bf_ref, upb_ref, bb_ref, gn_ref, seg_ref, s0_ref,
         o_ref, sfin_ref, qd_scr, st_scr, dec_scr) = refs
    c = CHUNK
    row = lax.broadcasted_iota(jnp.int32, (c, c), 0)
    col = lax.broadcasted_iota(jnp.int32, (c, c), 1)
    heads = [slice(h * HEAD_DIM, (h + 1) * HEAD_DIM) for h in range(N_HEADS)]
    wrow = lax.broadcasted_iota(jnp.int32, (W_GRP, W_GRP), 0)
    wcol = lax.broadcasted_iota(jnp.int32, (W_GRP, W_GRP), 1)
    head_diag = (wrow // HEAD_DIM) == (wcol // HEAD_DIM)

    def blockdiag(x):
        return jnp.where(head_diag, jnp.concatenate([x] * N_HEADS, axis=0), jnp.zeros((), x.dtype))

    trow = lax.broadcasted_iota(jnp.int32, (c, W_GRP), 0)
    tcol = lax.broadcasted_iota(jnp.int32, (c, W_GRP), 1) % c
    keep = [trow >= tcol, trow <= tcol]
    eye_tiled = jnp.where(trow == tcol, 1.0, 0.0).astype(BF16)

    def head_transpose_exact(x):
        return sum(_bdot_nt(eye_tiled, blockdiag(part)) for part in _split3(x))

    sum_ops = [jnp.where(row >= col, 1.0, 0.0).astype(BF16),
               jnp.where(row <= col, 1.0, 0.0).astype(BF16)]
    dir_params = [(upf_ref, bf_ref), (upb_ref, bb_ref)]
    dirs = (0, 1)

    def chunk_rows(n):
        return pl.ds(pl.multiple_of(n * c, c), c)

    def phase1(i, carry):
        ns = [i * GLA_GROUP + g for g in range(GLA_GROUP)]
        rows = [chunk_rows(n) for n in ns]
        a = [a_ref[0, r, :] for r in rows]
        x = [[_bdot(ag[:, d * GATE_RANK:(d + 1) * GATE_RANK], dir_params[d][0][0]) + dir_params[d][1][0]
              for d in dirs] for ag in a]
        la = [[(jnp.minimum(xd, 0.0) - jnp.log(1.0 + jnp.exp(-jnp.abs(xd)))) * (1.0 / GLA_TAU) for xd in xg]
              for xg in x]
        cum = [[sum(jnp.dot(sum_ops[d], part, preferred_element_type=F32) for part in _split3(lg[d]))
                for d in dirs] for lg in la]
        tot_row = [[cg[0][c - 1:c, :], cg[1][0:1, :]] for cg in cum]
        q = [q_ref[0, r, :] * (HEAD_DIM ** -0.5) for r in rows]
        k = [k_ref[0, r, :] for r in rows]
        v_bd = [blockdiag(v_ref[0, r, :].astype(BF16)) for r in rows]
        q_dec = [[(q[g] * jnp.exp(cum[g][d])).astype(BF16) for d in dirs] for g in range(GLA_GROUP)]
        k_inv = [[blockdiag((k[g] * jnp.exp(-cum[g][d])).astype(BF16)) for d in dirs] for g in range(GLA_GROUP)]
        k_end = [[blockdiag((k[g] * jnp.exp(tot_row[g][d] - cum[g][d])).astype(BF16)) for d in dirs]
                 for g in range(GLA_GROUP)]
        scores = [[_bdot_nt(q_dec[g][d], k_inv[g][d]) for d in dirs] for g in range(GLA_GROUP)]
        v_t = [_bdot_nt(eye_tiled, vb).astype(BF16) for vb in v_bd]
        upd = [[_bdot(v_t[g], k_end[g][d]) for d in dirs] for g in range(GLA_GROUP)]
        att = [[jnp.where(keep[d], scores[g][d], 0.0).astype(BF16) for d in dirs] for g in range(GLA_GROUP)]
        for g in range(GLA_GROUP):
            o_ref[0, rows[g], :] = _bdot(att[g][0], v_bd[g]) + _bdot(att[g][1], v_bd[g])
            for d in dirs:
                qd_scr[d, rows[g], :] = q_dec[g][d]
                dec_scr[d, pl.ds(ns[g], 1), :] = jnp.exp(tot_row[g][d])
                st_scr[d, ns[g]] = upd[g][d]
        return carry

    lax.fori_loop(0, n_chunks // GLA_GROUP, phase1, 0)

    def init_state(d):
        if zero_init:
            return jnp.zeros((HEAD_DIM, W_GRP), F32)
        return head_transpose_exact(jnp.concatenate([s0_ref[0, 0, d, h] for h in range(N_HEADS)], axis=1))

    finals = []
    for d in range(2):
        def scan_body(i, st, d=d):
            n = i if d == 0 else n_chunks - 1 - i
            upd = st_scr[d, n]
            st_scr[d, n] = st
            return st * dec_scr[d, pl.ds(n, 1), :] + upd
        finals.append(lax.fori_loop(0, n_chunks, scan_body, init_state(d)))

    def phase3(i, carry):
        ns = [i * GLA_GROUP + g for g in range(GLA_GROUP)]
        rows = [chunk_rows(n) for n in ns]
        st_bd = [[blockdiag(st_scr[d, n].astype(BF16)) for d in dirs] for n in ns]
        inter = [[_bdot_nt(qd_scr[d, rows[g], :], st_bd[g][d]) for d in dirs] for g in range(GLA_GROUP)]
        o = jnp.concatenate([o_ref[0, rows[g], :] + inter[g][0] + inter[g][1] for g in range(GLA_GROUP)], axis=0)
        o = _head_rms(o, seg_ref[...], gn_ref[0])
        for g in range(GLA_GROUP):
            o_ref[0, rows[g], :] = o[g * c:(g + 1) * c]
        return carry

    lax.fori_loop(0, n_chunks // GLA_GROUP, phase3, 0)
    for d in range(2):
        s_fin = head_transpose_exact(finals[d])
        for h, hs in enumerate(heads):
            sfin_ref[0, d, h] = s_fin[:, hs]


def _gla(p, up_f, b_f, up_b, b_b, gn, seg, layer, s0):
    b, s, _ = p.shape
    zero_init = s0 is None
    n_chunks = s // CHUNK
    col = lambda g: pl.BlockSpec((1, s, W_GRP), lambda i, g=g: (i, 0, g))
    lay = lambda shape: pl.BlockSpec((1,) + shape, lambda i: (layer,) + (0,) * len(shape))
    in_specs = [col(G_QD), col(G_KD), col(G_VD),
                pl.BlockSpec((1, s, LANES), lambda i: (i, 0, GATE_BLK)),
                lay((GATE_RANK, W_GRP)), lay((1, W_GRP)), lay((GATE_RANK, W_GRP)), lay((1, W_GRP)),
                lay((1, W_GRP)), pl.BlockSpec((W_GRP, W_GRP), lambda i: (0, 0))]
    args = [p, p, p, p, up_f, b_f, up_b, b_b, gn, seg]
    if not zero_init:
        in_specs.append(pl.BlockSpec((1, 1, 2, N_HEADS, HEAD_DIM, HEAD_DIM), lambda i: (i, layer, 0, 0, 0, 0)))
        args.append(s0)
    return pl.pallas_call(
        functools.partial(_gla_kernel, n_chunks=n_chunks, zero_init=zero_init),
        grid=(b,),
        in_specs=in_specs,
        out_specs=[pl.BlockSpec((1, s, W_GRP), lambda i: (i, 0, 0)),
                   pl.BlockSpec((1, 2, N_HEADS, HEAD_DIM, HEAD_DIM), lambda i: (i, 0, 0, 0, 0))],
        out_shape=[jax.ShapeDtypeStruct((b, s, W_GRP), F32),
                   jax.ShapeDtypeStruct((b, 2, N_HEADS, HEAD_DIM, HEAD_DIM), F32)],
        scratch_shapes=[pltpu.VMEM((2, s, W_GRP), BF16),
                        pltpu.VMEM((2, n_chunks, HEAD_DIM, W_GRP), F32),
                        pltpu.VMEM((2, max(n_chunks, 8), W_GRP), F32)],
        compiler_params=_cparams("arbitrary"),
        name="gla_zero" if zero_init else "gla",
    )(*args)


def _outproj_kernel(x_ref, mod_ref, oa_ref, oc_ref, od_ref, ga_ref, bb_ref, cb_ref, hb_ref, gb_ref,
                    gc_ref, gd_ref, cprev_ref, hprev_ref, cnext_ref, hnext_ref, cw_ref, w_ref, o_ref):
    j = pl.program_id(1)
    last = pl.num_programs(1) - 1
    tm = x_ref.shape[1]
    u = cb_ref[0] * hb_ref[0]
    u_prev_row = jnp.where(j == 0, 0.0, cprev_ref[0, 7:8, :] * hprev_ref[0, 7:8, :])
    u_next_row = jnp.where(j == last, 0.0, cnext_ref[0, 0:1, :] * hnext_ref[0, 0:1, :])
    rid = lax.broadcasted_iota(jnp.int32, u.shape, 0)
    u_prev = jnp.where(rid == 0, u_prev_row, pltpu.roll(u, 1, axis=0))
    u_next = jnp.where(rid == tm - 1, u_next_row, pltpu.roll(u, tm - 1, axis=0))
    cw = cw_ref[0]
    conv = u_prev * cw[0:1, :] + u * cw[1:2, :] + u_next * cw[2:3, :]
    y_a = oa_ref[0] * _silu(ga_ref[0])
    y_b = bb_ref[0] * conv * _silu(gb_ref[0])
    y_c = oc_ref[0] * _silu(gc_ref[0])
    y_d = od_ref[0] * _silu(gd_ref[0])
    y = jnp.concatenate([y_a, y_b, y_c, y_d], axis=1).astype(BF16)
    gate = mod_ref[0, :, 2 * D_MODEL:3 * D_MODEL]
    o_ref[0] = x_ref[0] + gate * jnp.dot(y, w_ref[0], preferred_element_type=F32)


def _outproj(x, mod, layer, o_a, o_c, o_d, p, conv_w, w_out, *, tm):
    b, s, _ = x.shape
    per_batch = mod.shape[0] > 1
    tile = lambda width: pl.BlockSpec((1, tm, width), lambda i, j: (i, j, 0))
    col = lambda g: pl.BlockSpec((1, tm, W_GRP), lambda i, j, g=g: (i, j, g))
    t8 = tm // 8
    n8 = s // 8
    prev = lambda g: pl.BlockSpec((1, 8, W_GRP), lambda i, j, g=g: (i, jnp.maximum(j * t8 - 1, 0), g))
    nxt = lambda g: pl.BlockSpec((1, 8, W_GRP), lambda i, j, g=g: (i, jnp.minimum((j + 1) * t8, n8 - 1), g))
    return pl.pallas_call(
        _outproj_kernel,
        grid=(b, s // tm),
        in_specs=[tile(D_MODEL),
                  pl.BlockSpec((1, 1, 3 * D_MODEL), (lambda i, j: (i, 0, 0)) if per_batch else (lambda i, j: (0, 0, 0))),
                  tile(W_GRP), tile(W_GRP), tile(W_GRP),
                  col(G_GA), col(G_BB), col(G_CB), col(G_HB), col(G_GB), col(G_GC), col(G_GD),
                  prev(G_CB), prev(G_HB), nxt(G_CB), nxt(G_HB),
                  pl.BlockSpec((1, 3, W_GRP), lambda i, j: (layer, 0, 0)),
                  pl.BlockSpec((1, D_MODEL, D_MODEL), lambda i, j: (layer, 0, 0))],
        out_specs=tile(D_MODEL),
        out_shape=jax.ShapeDtypeStruct((b, s, D_MODEL), F32),
        compiler_params=_cparams("arbitrary", "arbitrary"),
        name="outproj",
    )(x, mod, o_a, o_c, o_d, p, p, p, p, p, p, p, p, p, p, p, conv_w, w_out)


def kernel(x_prompt, x_sample, cache_k, cache_v, state_gla, c, c_ctx, norm_g, w_mod, b_mod, w_in, q_norm_g,
           k_norm_g, rpb, conv_w, gla_up_f, gla_bias_f, gla_up_b, gla_bias_b, gla_norm_g, w_out):
    dec_batch = c.shape[0]
    pad_rows = 8 - (1 + dec_batch) % 8
    cvec = jnp.concatenate([c_ctx[None], c, jnp.zeros((pad_rows, D_MODEL), F32)], axis=0)
    mod = _modulation(cvec, w_mod, b_mod)
    mod_ctx = mod[:, 0:1].reshape(DEPTH, 1, 1, 3 * D_MODEL)
    mod_smp = mod[:, 1:1 + dec_batch].reshape(DEPTH, dec_batch, 1, 3 * D_MODEL)

    w_in_b = jnp.pad(w_in, ((0, 0), (0, 0), (0, N_IN_PAD - N_IN))).astype(BF16)
    w_out_b = w_out.astype(BF16)
    seg = _seg_ones()
    norm_g3 = norm_g.reshape(DEPTH, 1, D_MODEL)
    qn = jnp.tile(q_norm_g, (1, N_HEADS)).reshape(DEPTH, 1, W_GRP)
    kn = jnp.tile(k_norm_g, (1, N_HEADS)).reshape(DEPTH, 1, W_GRP)
    gn = jnp.tile(gla_norm_g, (1, N_HEADS)).reshape(DEPTH, 1, W_GRP)
    b_f = gla_bias_f.reshape(DEPTH, 1, W_GRP)
    b_b = gla_bias_b.reshape(DEPTH, 1, W_GRP)
    bias = _bias_tables(rpb)
    cache_k_b = cache_k.astype(BF16)
    cache_v_b = cache_v.astype(BF16)

    x = x_prompt
    ks, vs, ss = [], [], []
    for l in range(DEPTH):
        p, k_new, v_new = _inproj(x, mod_ctx[l], l, norm_g3, w_in_b, qn, kn, seg, tm=x.shape[1], head_major=True)
        o_a = _attn_ctx(p)
        o_c = _fft_ctx(p)
        o_d, s_fin = _gla(p, gla_up_f, b_f, gla_up_b, b_b, gn, seg, l, None)
        x = _outproj(x, mod_ctx[l], l, o_a, o_c, o_d, p, conv_w, w_out_b, tm=x.shape[1])
        ks.append(k_new)
        vs.append(v_new)
        ss.append(s_fin)
    y_prompt = x
    new_cache_k = jnp.stack(ks, axis=1)
    new_cache_v = jnp.stack(vs, axis=1)
    new_state = jnp.stack(ss, axis=1)

    xs = x_sample
    for l in range(DEPTH):
        p, k_tok, v_tok = _inproj(xs, mod_smp[l], l, norm_g3, w_in_b, qn, kn, seg, tm=512, head_major=False)
        o_a = _attn_nbr(p, k_tok, v_tok, cache_k_b, cache_v_b, bias, l)
        o_c = _fft_grid(p)
        o_d, _ = _gla(p, gla_up_f, b_f, gla_up_b, b_b, gn, seg, l, state_gla)
        xs = _outproj(xs, mod_smp[l], l, o_a, o_c, o_d, p, conv_w, w_out_b, tm=512)
    return (y_prompt, xs, new_cache_k, new_cache_v, new_state)
```

```python
import functools
import math

import numpy as np
import jax
import jax.numpy as jnp
from jax import lax
from jax.experimental import pallas as pl
from jax.experimental.pallas import tpu as pltpu

F32 = jnp.float32
BF16 = jnp.bfloat16

D_MODEL = 1024
DEPTH = 4
GRID_W = 64
W_GRP = 256
HEAD_DIM = 64
N_HEADS = 4
WIN_H = 8
WIN_W = 16
N_FOURIER = 4
GATE_RANK = 16
GLA_TAU = 16.0
CHUNK = 64
RMS_EPS = 1e-6
N_IN = 14 * W_GRP + 2 * GATE_RANK
LANES = 128
N_IN_PAD = -(-N_IN // LANES) * LANES
G_QA, G_KA, G_VA, G_GA, G_BB, G_CB, G_HB, G_GB, G_UC, G_GC, G_QD, G_KD, G_VD, G_GD = range(14)
GATE_BLK = 14 * W_GRP // LANES
NEG_BIG = -1e30
VMEM_LIMIT = 56 * 1024 * 1024


def _cparams(*sem):
    return pltpu.CompilerParams(dimension_semantics=sem, vmem_limit_bytes=VMEM_LIMIT)


def _bdot(a, b):
    return jnp.dot(a.astype(BF16), b.astype(BF16), preferred_element_type=F32)


def _bdot_nt(a, b):
    return lax.dot_general(a.astype(BF16), b.astype(BF16), (((1,), (1,)), ((), ())),
                           preferred_element_type=F32)


def _split2(x):
    hi = x.astype(BF16)
    lo = (x - hi.astype(F32)).astype(BF16)
    return hi, lo


def _split3(x):
    h1 = x.astype(BF16)
    r = x - h1.astype(F32)
    h2 = r.astype(BF16)
    h3 = (r - h2.astype(F32)).astype(BF16)
    return h1, h2, h3


def _silu(x):
    return x / (1.0 + jnp.exp(-x))


def _head_rms(t, seg, g):
    n = t.shape[0]
    both = jnp.dot(jnp.concatenate(_split2(t * t), axis=0), seg, preferred_element_type=F32)
    ss = both[:n] + both[n:]
    return t * lax.rsqrt(ss * (1.0 / HEAD_DIM) + RMS_EPS) * g


def _dft_cos_sin(n):
    k = np.arange(n)
    ang = 2.0 * np.pi * ((k[:, None] * k[None, :]) % n) / n
    return np.cos(ang), np.sin(ang)


def _block_diag(m, reps):
    n = m.shape[0]
    out = np.zeros((n * reps, n * reps), m.dtype)
    for i in range(reps):
        out[i * n:(i + 1) * n, i * n:(i + 1) * n] = m
    return out


def _seg_ones():
    return jnp.asarray(_block_diag(np.ones((HEAD_DIM, HEAD_DIM)), N_HEADS), BF16)


def _mod_kernel(c_ref, w_ref, b_ref, o_ref):
    o_ref[0] = _bdot(_silu(c_ref[...]), w_ref[0]) + b_ref[0]


def _modulation(cvec, w_mod, b_mod):
    rows = cvec.shape[0]
    return pl.pallas_call(
        _mod_kernel,
        grid=(DEPTH, 3),
        in_specs=[pl.BlockSpec((rows, D_MODEL), lambda l, j: (0, 0)),
                  pl.BlockSpec((1, D_MODEL, D_MODEL), lambda l, j: (l, 0, j)),
                  pl.BlockSpec((1, 1, D_MODEL), lambda l, j: (l, 0, j))],
        out_specs=pl.BlockSpec((1, rows, D_MODEL), lambda l, j: (l, 0, j)),
        out_shape=jax.ShapeDtypeStruct((DEPTH, rows, 3 * D_MODEL), F32),
        compiler_params=_cparams("arbitrary", "arbitrary"),
        name="adaln_mod",
    )(cvec, w_mod, b_mod.reshape(DEPTH, 1, 3 * D_MODEL))


def _inproj_kernel(x_ref, mod_ref, g_ref, w_ref, wg_ref, qn_ref, kn_ref, seg_ref, p_ref, k_ref, v_ref, h_scr, *,
                   v_ones):
    x = x_ref[0]
    ms = jnp.mean(x * x, axis=-1, keepdims=True)
    y = x * lax.rsqrt(ms + RMS_EPS) * g_ref[0]
    shift = mod_ref[0, :, 0:D_MODEL]
    scale = mod_ref[0, :, D_MODEL:2 * D_MODEL]
    h_scr[...] = (y * (1.0 + scale) + shift).astype(BF16)
    seg = seg_ref[...]
    for j in range(14):
        cs = slice(j * W_GRP, (j + 1) * W_GRP)
        t = jnp.dot(h_scr[...], w_ref[0, :, cs], preferred_element_type=F32)
        if j == G_QA:
            t = _head_rms(t, seg, qn_ref[0]) * (HEAD_DIM ** -0.5)
        elif j == G_KA:
            t = _head_rms(t, seg, kn_ref[0])
        p_ref[0, :, cs] = t
        if j in (G_KA, G_VA):
            ref = k_ref if j == G_KA else v_ref
            for h in range(N_HEADS):
                th = t[:, h * HEAD_DIM:(h + 1) * HEAD_DIM]
                if j == G_VA and v_ones:
                    th = jnp.concatenate([th, jnp.ones_like(th)], axis=1)
                ref[0, h] = th.astype(ref.dtype)
    p_ref[0, :, 14 * W_GRP:N_IN_PAD] = jnp.dot(h_scr[...], wg_ref[0], preferred_element_type=F32)


def _inproj(x, mod, layer, norm_g, w_main, w_gate, qn, kn, seg, *, tm, for_window_attn):
    b, s, _ = x.shape
    per_batch = mod.shape[0] > 1
    kv_dtype = BF16 if for_window_attn else F32
    v_width = 2 * HEAD_DIM if for_window_attn else HEAD_DIM
    head_major = lambda width: (jax.ShapeDtypeStruct((b, N_HEADS, s, width), kv_dtype),
                                pl.BlockSpec((1, N_HEADS, tm, width), lambda i, j: (i, 0, j, 0)))
    outs = [(jax.ShapeDtypeStruct((b, s, N_IN_PAD), F32), pl.BlockSpec((1, tm, N_IN_PAD), lambda i, j: (i, j, 0))),
            head_major(HEAD_DIM), head_major(v_width)]
    return pl.pallas_call(
        functools.partial(_inproj_kernel, v_ones=for_window_attn),
        grid=(b, s // tm),
        in_specs=[pl.BlockSpec((1, tm, D_MODEL), lambda i, j: (i, j, 0)),
                  pl.BlockSpec((1, 1, 3 * D_MODEL), (lambda i, j: (i, 0, 0)) if per_batch else (lambda i, j: (0, 0, 0))),
                  pl.BlockSpec((1, 1, D_MODEL), lambda i, j: (layer, 0, 0)),
                  pl.BlockSpec((1, D_MODEL, 14 * W_GRP), lambda i, j: (layer, 0, 0)),
                  pl.BlockSpec((1, D_MODEL, LANES), lambda i, j: (layer, 0, 0)),
                  pl.BlockSpec((1, 1, W_GRP), lambda i, j: (layer, 0, 0)),
                  pl.BlockSpec((1, 1, W_GRP), lambda i, j: (layer, 0, 0)),
                  pl.BlockSpec((W_GRP, W_GRP), lambda i, j: (0, 0))],
        out_specs=[o[1] for o in outs],
        out_shape=[o[0] for o in outs],
        scratch_shapes=[pltpu.VMEM((tm, D_MODEL), BF16)],
        compiler_params=_cparams("arbitrary", "arbitrary"),
        name="inproj_grid" if for_window_attn else "inproj_ctx",
    )(x, mod, norm_g, w_main, w_gate, qn, kn, seg)


def _attn_ctx_kernel(q_ref, k_ref, v_ref, o_ref):
    for h in range(N_HEADS):
        hs = slice(h * HEAD_DIM, (h + 1) * HEAD_DIM)
        s = _bdot_nt(q_ref[0, :, hs], k_ref[0, :, hs])
        e = jnp.exp(s - jnp.max(s, axis=-1, keepdims=True))
        l = jnp.sum(e, axis=-1, keepdims=True)
        o_ref[0, :, hs] = _bdot(e, v_ref[0, :, hs]) / l


def _attn_ctx(p):
    b, s, _ = p.shape
    col = lambda g: pl.BlockSpec((1, s, W_GRP), lambda i, g=g: (i, 0, g))
    return pl.pallas_call(
        _attn_ctx_kernel,
        grid=(b,),
        in_specs=[col(G_QA), col(G_KA), col(G_VA)],
        out_specs=pl.BlockSpec((1, s, W_GRP), lambda i: (i, 0, 0)),
        out_shape=jax.ShapeDtypeStruct((b, s, W_GRP), F32),
        compiler_params=_cparams("arbitrary"),
        name="attn_ctx",
    )(p, p, p)


def _bias_kernel(rpb_ref, o_ref):
    lh = pl.program_id(0)
    q = lax.broadcasted_iota(jnp.int32, (GRID_W, GRID_W), 0)
    kc = lax.broadcasted_iota(jnp.int32, (GRID_W, GRID_W), 1)
    dc = jnp.clip(kc - q + (WIN_W - 1), 0, 2 * WIN_W - 2)
    cs = jnp.clip(q - WIN_W // 2, 0, GRID_W - WIN_W)
    in_win = jnp.where(kc >= cs, jnp.where(kc < cs + WIN_W, 1, 0), 0) == 1
    n_dr, n_dc = 2 * WIN_H - 1, 2 * WIN_W - 1
    tiles = []
    for dr in range(n_dr):
        base = (lh * n_dr + dr) * n_dc
        t = jnp.zeros((GRID_W, GRID_W), F32)
        for d in range(n_dc):
            t = jnp.where(dc == d, rpb_ref[base + d], t)
        tiles.append(jnp.where(in_win, t, NEG_BIG))
    for dl in range(WIN_H):
        for i in range(WIN_H):
            o_ref[0, dl, :, i * GRID_W:(i + 1) * GRID_W] = tiles[dl + i]


def _bias_tables(rpb):
    out = pl.pallas_call(
        _bias_kernel,
        grid=(DEPTH * N_HEADS,),
        in_specs=[pl.BlockSpec(memory_space=pltpu.SMEM)],
        out_specs=pl.BlockSpec((1, WIN_H, GRID_W, WIN_H * GRID_W), lambda i: (i, 0, 0, 0)),
        out_shape=jax.ShapeDtypeStruct((DEPTH * N_HEADS, WIN_H, GRID_W, WIN_H * GRID_W), F32),
        compiler_params=_cparams("arbitrary"),
        name="rpb_tables",
    )(rpb.reshape(-1))
    return out.reshape(DEPTH, N_HEADS, WIN_H, GRID_W, WIN_H * GRID_W)


ATTN_ROW_GROUP = 2


def _attn_nbr_kernel(q_ref, k_ref, v_ref, ck_ref, cv_ref, bias_ref, o_ref, *, rows_per_step):
    rb = pl.program_id(1)
    n_rows = k_ref.shape[2] // GRID_W
    kh = min(WIN_H, n_rows)
    n_lat = kh * GRID_W
    heads = [slice(h * HEAD_DIM, (h + 1) * HEAD_DIM) for h in range(N_HEADS)]
    units = [(g, h) for g in range(ATTN_ROW_GROUP) for h in range(N_HEADS)]
    low_half = lax.broadcasted_iota(jnp.int32, (GRID_W, 2 * HEAD_DIM), 1) < HEAD_DIM

    def rows_body(jj, carry):
        q0, k0, dl, q = [], [], [], []
        for g in range(ATTN_ROW_GROUP):
            j = jj * ATTN_ROW_GROUP + g
            r = rb * rows_per_step + j
            rs = jnp.clip(r - kh // 2, 0, n_rows - kh)
            dl.append(rs - r + (WIN_H - 1))
            k0.append(pl.multiple_of(rs * GRID_W, GRID_W))
            q0.append(pl.multiple_of(j * GRID_W, GRID_W))
            q.append(q_ref[0, pl.ds(q0[g], GRID_W), :].astype(BF16))
        s_lat = [_bdot_nt(q[g][:, heads[h]], k_ref[0, h, pl.ds(k0[g], n_lat), :]) + bias_ref[0, h, dl[g]]
                 for g, h in units]
        s_ctx = [_bdot(q[g][:, heads[h]], ck_ref[0, 0, h]) for g, h in units]
        m = [jnp.maximum(jnp.max(a, axis=-1, keepdims=True), jnp.max(c, axis=-1, keepdims=True))
             for a, c in zip(s_lat, s_ctx)]
        e_lat = [jnp.exp(a - mu).astype(BF16) for a, mu in zip(s_lat, m)]
        e_ctx = [jnp.exp(c - mu).astype(BF16) for c, mu in zip(s_ctx, m)]
        res = [jnp.dot(e_lat[u], v_ref[0, h, pl.ds(k0[g], n_lat), :], preferred_element_type=F32)
               + jnp.dot(e_ctx[u], cv_ref[0, 0, h], preferred_element_type=F32)
               for u, (g, h) in enumerate(units)]
        for g in range(ATTN_ROW_GROUP):
            pairs = []
            for h in range(0, N_HEADS, 2):
                even, odd = res[g * N_HEADS + h], res[g * N_HEADS + h + 1]
                pairs.append(jnp.where(low_half, even / pltpu.roll(even, HEAD_DIM, axis=1),
                                       pltpu.roll(odd, HEAD_DIM, axis=1) / odd))
            o_ref[0, pl.ds(q0[g], GRID_W), :] = jnp.concatenate(pairs, axis=1)
        return carry

    lax.fori_loop(0, rows_per_step // ATTN_ROW_GROUP, rows_body, 0)


def _attn_nbr(p, k_hm, v_hm, cache_k, cache_v, bias, layer, *, rows_per_step=16):
    b, s, _ = p.shape
    tq = rows_per_step * GRID_W
    past = cache_v.shape[3]
    ctx_spec = lambda *dims: pl.BlockSpec((1, 1, N_HEADS) + dims, lambda i, j: (i, layer, 0, 0, 0))
    return pl.pallas_call(
        functools.partial(_attn_nbr_kernel, rows_per_step=rows_per_step),
        grid=(b, s // tq),
        in_specs=[pl.BlockSpec((1, tq, W_GRP), lambda i, j: (i, j, G_QA)),
                  pl.BlockSpec((1, N_HEADS, s, HEAD_DIM), lambda i, j: (i, 0, 0, 0)),
                  pl.BlockSpec((1, N_HEADS, s, 2 * HEAD_DIM), lambda i, j: (i, 0, 0, 0)),
                  ctx_spec(HEAD_DIM, past), ctx_spec(past, 2 * HEAD_DIM),
                  pl.BlockSpec((1, N_HEADS, WIN_H, GRID_W, WIN_H * GRID_W), lambda i, j: (layer, 0, 0, 0, 0))],
        out_specs=pl.BlockSpec((1, tq, W_GRP), lambda i, j: (i, j, 0)),
        out_shape=jax.ShapeDtypeStruct((b, s, W_GRP), F32),
        compiler_params=_cparams("arbitrary", "arbitrary"),
        name="attn_nbr",
    )(p, k_hm, v_hm, cache_k, cache_v, bias)


def _fft_ctx_consts(s):
    c, sn = _dft_cos_sin(s)
    cc, sc = _dft_cos_sin(W_GRP // N_FOURIER)
    m1 = np.concatenate([c, -sn], axis=0)
    m2 = np.concatenate([_block_diag(cc, N_FOURIER), _block_diag(sc, N_FOURIER)], axis=0)
    return jnp.asarray(m1, F32), jnp.asarray(m2, F32)


def _fft_ctx_kernel(u_ref, m1_ref, m2_ref, o_ref, *, scale):
    s = u_ref.shape[1]
    t = _bdot(m1_ref[...], u_ref[0])
    x = jnp.concatenate([t[:s], t[s:]], axis=1)
    o_ref[0] = _bdot(x, m2_ref[...]) * scale


def _fft_ctx(p):
    b, s, _ = p.shape
    consts = _fft_ctx_consts(s)
    full = lambda a: pl.BlockSpec(a.shape, lambda i: (0,) * a.ndim)
    return pl.pallas_call(
        functools.partial(_fft_ctx_kernel, scale=1.0 / math.sqrt(s * (W_GRP // N_FOURIER))),
        grid=(b,),
        in_specs=[pl.BlockSpec((1, s, W_GRP), lambda i: (i, 0, G_UC))] + [full(a) for a in consts],
        out_specs=pl.BlockSpec((1, s, W_GRP), lambda i: (i, 0, 0)),
        out_shape=jax.ShapeDtypeStruct((b, s, W_GRP), F32),
        compiler_params=_cparams("arbitrary"),
        name="fft_ctx",
    )(p, *consts)


def _fft_grid_consts():
    n = GRID_W
    c, sn = _dft_cos_sin(n)
    cc, sc = _dft_cos_sin(W_GRP // N_FOURIER)
    m1 = np.concatenate([c, -sn], axis=0)
    m2 = np.block([[c, sn], [-sn, c]])
    reps = LANES // (W_GRP // N_FOURIER)
    m3 = np.concatenate([_block_diag(cc, reps), _block_diag(sc, reps)], axis=0)
    k = np.arange(n)
    ang = 2.0 * np.pi * (k[:, None] * k[None, :]) / (n * n)
    twr = np.broadcast_to(np.cos(ang)[:, :, None], (n, n, LANES))
    twi = np.broadcast_to(-np.sin(ang)[:, :, None], (n, n, LANES))
    return tuple(jnp.asarray(m, F32) for m in (m1, m2, m3, twr, twi))


FFT_UNROLL = 8
FFT_PITCH = 72


def _fft_grid_kernel(u_ref, m1_ref, m2_ref, m3_ref, twr_ref, twi_ref, o_ref, tr_scr, ti_scr, *, scale):
    n = GRID_W
    m1, m2, m3 = (r[...].astype(BF16) for r in (m1_ref, m2_ref, m3_ref))

    def stage1(b, carry):
        a_rows = u_ref[0, pl.ds(b, n, stride=n), :]
        t = _bdot(m1, a_rows)
        tr, ti = t[:n], t[n:]
        wr, wi = twr_ref[b], twi_ref[b]
        tr_scr[pl.ds(b, n, stride=FFT_PITCH), :] = tr * wr - ti * wi
        ti_scr[pl.ds(b, n, stride=FFT_PITCH), :] = tr * wi + ti * wr
        return carry

    lax.fori_loop(0, n, stage1, 0, unroll=FFT_UNROLL)

    def stage2(d, carry):
        r0 = pl.multiple_of(d * FFT_PITCH, 8)
        g = jnp.concatenate([tr_scr[pl.ds(r0, n), :], ti_scr[pl.ds(r0, n), :]], axis=0)
        x = _bdot(m2, g)
        xc = jnp.concatenate([x[:n], x[n:]], axis=1)
        o_ref[0, pl.ds(d, n, stride=n), :] = _bdot(xc, m3) * scale
        return carry

    lax.fori_loop(0, n, stage2, 0, unroll=FFT_UNROLL)


def _fft_grid(p):
    b, s, _ = p.shape
    assert s == GRID_W * GRID_W
    consts = _fft_grid_consts()
    full = lambda a: pl.BlockSpec(a.shape, lambda i, j: (0,) * a.ndim)
    blk0 = G_UC * W_GRP // LANES
    return pl.pallas_call(
        functools.partial(_fft_grid_kernel, scale=1.0 / math.sqrt(s * (W_GRP // N_FOURIER))),
        grid=(b, W_GRP // LANES),
        in_specs=[pl.BlockSpec((1, s, LANES), lambda i, j: (i, 0, blk0 + j))] + [full(a) for a in consts],
        out_specs=pl.BlockSpec((1, s, LANES), lambda i, j: (i, 0, j)),
        out_shape=jax.ShapeDtypeStruct((b, s, W_GRP), F32),
        scratch_shapes=[pltpu.VMEM((GRID_W * FFT_PITCH, LANES), F32)] * 2,
        compiler_params=_cparams("arbitrary", "arbitrary"),
        name="fft_grid",
    )(p, *consts)


GLA_GROUP = 4


def _gla_kernel(*refs, n_chunks, zero_init):
    if zero_init:
        (q_ref, k_ref, v_ref, a_ref, upf_ref, bf_ref, upb_ref, bb_ref, gn_ref, seg_ref,
         o_ref, sfin_ref, qd_scr, st_scr, dec_scr) = refs
        s0_ref = None
    else:
        (q_ref, k_ref, v_ref, a_ref, upf_ref, bf_ref, upb_ref, bb_ref, gn_ref, seg_ref, s0_ref,
         o_ref, sfin_ref, qd_scr, st_scr, dec_scr) = refs
    c = CHUNK
    row = lax.broadcasted_iota(jnp.int32, (c, c), 0)
    col = lax.broadcasted_iota(jnp.int32, (c, c), 1)
    heads = [slice(h * HEAD_DIM, (h + 1) * HEAD_DIM) for h in range(N_HEADS)]
    wrow = lax.broadcasted_iota(jnp.int32, (W_GRP, W_GRP), 0)
    wcol = lax.broadcasted_iota(jnp.int32, (W_GRP, W_GRP), 1)
    head_diag = (wrow // HEAD_DIM) == (wcol // HEAD_DIM)

    def blockdiag(x):
        return jnp.where(head_diag, jnp.concatenate([x] * N_HEADS, axis=0), jnp.zeros((), x.dtype))

    trow = lax.broadcasted_iota(jnp.int32, (c, W_GRP), 0)
    tcol = lax.broadcasted_iota(jnp.int32, (c, W_GRP), 1) % c
    keep = [trow >= tcol, trow <= tcol]
    eye_tiled = jnp.where(trow == tcol, 1.0, 0.0).astype(BF16)

    def head_transpose_exact(x):
        return sum(_bdot_nt(eye_tiled, blockdiag(part)) for part in _split3(x))

    sum_ops = [jnp.where(row >= col, 1.0, 0.0).astype(BF16),
               jnp.where(row <= col, 1.0, 0.0).astype(BF16)]
    dir_params = [(upf_ref, bf_ref), (upb_ref, bb_ref)]
    dirs = (0, 1)

    def chunk_rows(n):
        return pl.ds(pl.multiple_of(n * c, c), c)

    def phase1(i, carry):
        ns = [i * GLA_GROUP + g for g in range(GLA_GROUP)]
        rows = [chunk_rows(n) for n in ns]
        a = [a_ref[0, r, :] for r in rows]
        x = [[_bdot(ag[:, d * GATE_RANK:(d + 1) * GATE_RANK], dir_params[d][0][0]) + dir_params[d][1][0]
              for d in dirs] for ag in a]
        la = [[(jnp.minimum(xd, 0.0) - jnp.log(1.0 + jnp.exp(-jnp.abs(xd)))) * (1.0 / GLA_TAU) for xd in xg]
              for xg in x]
        cum = [[sum(jnp.dot(sum_ops[d], part, preferred_element_type=F32) for part in _split3(lg[d]))
                for d in dirs] for lg in la]
        tot_row = [[cg[0][c - 1:c, :], cg[1][0:1, :]] for cg in cum]
        q = [q_ref[0, r, :] * (HEAD_DIM ** -0.5) for r in rows]
        k = [k_ref[0, r, :] for r in rows]
        v_bd = [blockdiag(v_ref[0, r, :].astype(BF16)) for r in rows]
        q_dec = [[(q[g] * jnp.exp(cum[g][d])).astype(BF16) for d in dirs] for g in range(GLA_GROUP)]
        k_inv = [[blockdiag((k[g] * jnp.exp(-cum[g][d])).astype(BF16)) for d in dirs] for g in range(GLA_GROUP)]
        k_end = [[blockdiag((k[g] * jnp.exp(tot_row[g][d] - cum[g][d])).astype(BF16)) for d in dirs]
                 for g in range(GLA_GROUP)]
        scores = [[_bdot_nt(q_dec[g][d], k_inv[g][d]) for d in dirs] for g in range(GLA_GROUP)]
        v_t = [_bdot_nt(eye_tiled, vb).astype(BF16) for vb in v_bd]
        upd = [[_bdot(v_t[g], k_end[g][d]) for d in dirs] for g in range(GLA_GROUP)]
        att = [[jnp.where(keep[d], scores[g][d], 0.0).astype(BF16) for d in dirs] for g in range(GLA_GROUP)]
        for g in range(GLA_GROUP):
            both = _bdot(jnp.concatenate(att[g], axis=0), v_bd[g])
            o_ref[0, rows[g], :] = both[:c] + both[c:]
            for d in dirs:
                qd_scr[d, rows[g], :] = q_dec[g][d]
                dec_scr[d, pl.ds(ns[g], 1), :] = jnp.exp(tot_row[g][d])
                st_scr[d, ns[g]] = upd[g][d]
        return carry

    lax.fori_loop(0, n_chunks // GLA_GROUP, phase1, 0)

    def init_state(d):
        if zero_init:
            return jnp.zeros((HEAD_DIM, W_GRP), F32)
        return head_transpose_exact(jnp.concatenate([s0_ref[0, 0, d, h] for h in range(N_HEADS)], axis=1))

    finals = []
    for d in range(2):
        def scan_body(i, st, d=d):
            n = i if d == 0 else n_chunks - 1 - i
            upd = st_scr[d, n]
            st_scr[d, n] = st
            return st * dec_scr[d, pl.ds(n, 1), :] + upd
        finals.append(lax.fori_loop(0, n_chunks, scan_body, init_state(d)))

    def phase3(i, carry):
        ns = [i * GLA_GROUP + g for g in range(GLA_GROUP)]
        rows = [chunk_rows(n) for n in ns]
        st_bd = [[blockdiag(st_scr[d, n].astype(BF16)) for d in dirs] for n in ns]
        inter = [[_bdot_nt(qd_scr[d, rows[g], :], st_bd[g][d]) for d in dirs] for g in range(GLA_GROUP)]
        o = jnp.concatenate([o_ref[0, rows[g], :] + inter[g][0] + inter[g][1] for g in range(GLA_GROUP)], axis=0)
        o = _head_rms(o, seg_ref[...], gn_ref[0])
        for g in range(GLA_GROUP):
            o_ref[0, rows[g], :] = o[g * c:(g + 1) * c]
        return carry

    lax.fori_loop(0, n_chunks // GLA_GROUP, phase3, 0)
    for d in range(2):
        s_fin = head_transpose_exact(finals[d])
        for h, hs in enumerate(heads):
            sfin_ref[0, d, h] = s_fin[:, hs]


def _gla(p, up_f, b_f, up_b, b_b, gn, seg, layer, s0):
    b, s, _ = p.shape
    zero_init = s0 is None
    n_chunks = s // CHUNK
    col = lambda g: pl.BlockSpec((1, s, W_GRP), lambda i, g=g: (i, 0, g))
    lay = lambda shape: pl.BlockSpec((1,) + shape, lambda i: (layer,) + (0,) * len(shape))
    in_specs = [col(G_QD), col(G_KD), col(G_VD),
                pl.BlockSpec((1, s, LANES), lambda i: (i, 0, GATE_BLK)),
                lay((GATE_RANK, W_GRP)), lay((1, W_GRP)), lay((GATE_RANK, W_GRP)), lay((1, W_GRP)),
                lay((1, W_GRP)), pl.BlockSpec((W_GRP, W_GRP), lambda i: (0, 0))]
    args = [p, p, p, p, up_f, b_f, up_b, b_b, gn, seg]
    if not zero_init:
        in_specs.append(pl.BlockSpec((1, 1, 2, N_HEADS, HEAD_DIM, HEAD_DIM), lambda i: (i, layer, 0, 0, 0, 0)))
        args.append(s0)
    return pl.pallas_call(
        functools.partial(_gla_kernel, n_chunks=n_chunks, zero_init=zero_init),
        grid=(b,),
        in_specs=in_specs,
        out_specs=[pl.BlockSpec((1, s, W_GRP), lambda i: (i, 0, 0)),
                   pl.BlockSpec((1, 2, N_HEADS, HEAD_DIM, HEAD_DIM), lambda i: (i, 0, 0, 0, 0))],
        out_shape=[jax.ShapeDtypeStruct((b, s, W_GRP), F32),
                   jax.ShapeDtypeStruct((b, 2, N_HEADS, HEAD_DIM, HEAD_DIM), F32)],
        scratch_shapes=[pltpu.VMEM((2, s, W_GRP), BF16),
                        pltpu.VMEM((2, n_chunks, HEAD_DIM, W_GRP), F32),
                        pltpu.VMEM((2, max(n_chunks, 8), W_GRP), F32)],
        compiler_params=_cparams("arbitrary"),
        name="gla_zero" if zero_init else "gla",
    )(*args)


def _outproj_kernel(x_ref, mod_ref, oa_ref, oc_ref, od_ref, ga_ref, bb_ref, cb_ref, hb_ref, gb_ref,
                    gc_ref, gd_ref, cprev_ref, hprev_ref, cnext_ref, hnext_ref, cw_ref, w_ref, o_ref):
    j = pl.program_id(1)
    last = pl.num_programs(1) - 1
    tm = x_ref.shape[1]
    u = cb_ref[0] * hb_ref[0]
    u_prev_row = jnp.where(j == 0, 0.0, cprev_ref[0, 7:8, :] * hprev_ref[0, 7:8, :])
    u_next_row = jnp.where(j == last, 0.0, cnext_ref[0, 0:1, :] * hnext_ref[0, 0:1, :])
    rid = lax.broadcasted_iota(jnp.int32, u.shape, 0)
    u_prev = jnp.where(rid == 0, u_prev_row, pltpu.roll(u, 1, axis=0))
    u_next = jnp.where(rid == tm - 1, u_next_row, pltpu.roll(u, tm - 1, axis=0))
    cw = cw_ref[0]
    conv = u_prev * cw[0:1, :] + u * cw[1:2, :] + u_next * cw[2:3, :]
    y_a = oa_ref[0] * _silu(ga_ref[0])
    y_b = bb_ref[0] * conv * _silu(gb_ref[0])
    y_c = oc_ref[0] * _silu(gc_ref[0])
    y_d = od_ref[0] * _silu(gd_ref[0])
    y = jnp.concatenate([y_a, y_b, y_c, y_d], axis=1).astype(BF16)
    gate = mod_ref[0, :, 2 * D_MODEL:3 * D_MODEL]
    o_ref[0] = x_ref[0] + gate * jnp.dot(y, w_ref[0], preferred_element_type=F32)


def _outproj(x, mod, layer, o_a, o_c, o_d, p, conv_w, w_out, *, tm):
    b, s, _ = x.shape
    per_batch = mod.shape[0] > 1
    tile = lambda width: pl.BlockSpec((1, tm, width), lambda i, j: (i, j, 0))
    col = lambda g: pl.BlockSpec((1, tm, W_GRP), lambda i, j, g=g: (i, j, g))
    t8 = tm // 8
    n8 = s // 8
    prev = lambda g: pl.BlockSpec((1, 8, W_GRP), lambda i, j, g=g: (i, jnp.maximum(j * t8 - 1, 0), g))
    nxt = lambda g: pl.BlockSpec((1, 8, W_GRP), lambda i, j, g=g: (i, jnp.minimum((j + 1) * t8, n8 - 1), g))
    return pl.pallas_call(
        _outproj_kernel,
        grid=(b, s // tm),
        in_specs=[tile(D_MODEL),
                  pl.BlockSpec((1, 1, 3 * D_MODEL), (lambda i, j: (i, 0, 0)) if per_batch else (lambda i, j: (0, 0, 0))),
                  tile(W_GRP), tile(W_GRP), tile(W_GRP),
                  col(G_GA), col(G_BB), col(G_CB), col(G_HB), col(G_GB), col(G_GC), col(G_GD),
                  prev(G_CB), prev(G_HB), nxt(G_CB), nxt(G_HB),
                  pl.BlockSpec((1, 3, W_GRP), lambda i, j: (layer, 0, 0)),
                  pl.BlockSpec((1, D_MODEL, D_MODEL), lambda i, j: (layer, 0, 0))],
        out_specs=tile(D_MODEL),
        out_shape=jax.ShapeDtypeStruct((b, s, D_MODEL), F32),
        compiler_params=_cparams("arbitrary", "arbitrary"),
        name="outproj",
    )(x, mod, o_a, o_c, o_d, p, p, p, p, p, p, p, p, p, p, p, conv_w, w_out)


def kernel(x_prompt, x_sample, cache_k, cache_v, state_gla, c, c_ctx, norm_g, w_mod, b_mod, w_in, q_norm_g,
           k_norm_g, rpb, conv_w, gla_up_f, gla_bias_f, gla_up_b, gla_bias_b, gla_norm_g, w_out):
    dec_batch = c.shape[0]
    pad_rows = 8 - (1 + dec_batch) % 8
    cvec = jnp.concatenate([c_ctx[None], c, jnp.zeros((pad_rows, D_MODEL), F32)], axis=0)
    mod = _modulation(cvec, w_mod, b_mod)
    mod_ctx = mod[:, 0:1].reshape(DEPTH, 1, 1, 3 * D_MODEL)
    mod_smp = mod[:, 1:1 + dec_batch].reshape(DEPTH, dec_batch, 1, 3 * D_MODEL)

    w_main = w_in[:, :, :14 * W_GRP].astype(BF16)
    w_gate = jnp.pad(w_in[:, :, 14 * W_GRP:], ((0, 0), (0, 0), (0, N_IN_PAD - N_IN))).astype(BF16)
    w_out_b = w_out.astype(BF16)
    seg = _seg_ones()
    norm_g3 = norm_g.reshape(DEPTH, 1, D_MODEL)
    qn = jnp.tile(q_norm_g, (1, N_HEADS)).reshape(DEPTH, 1, W_GRP)
    kn = jnp.tile(k_norm_g, (1, N_HEADS)).reshape(DEPTH, 1, W_GRP)
    gn = jnp.tile(gla_norm_g, (1, N_HEADS)).reshape(DEPTH, 1, W_GRP)
    b_f = gla_bias_f.reshape(DEPTH, 1, W_GRP)
    b_b = gla_bias_b.reshape(DEPTH, 1, W_GRP)
    bias = _bias_tables(rpb)
    cache_k_b = jnp.swapaxes(cache_k, -1, -2).astype(BF16)
    cache_v_b = jnp.concatenate([cache_v.astype(BF16), jnp.ones(cache_v.shape, BF16)], axis=-1)

    x = x_prompt
    ks, vs, ss = [], [], []
    for l in range(DEPTH):
        p, k_new, v_new = _inproj(x, mod_ctx[l], l, norm_g3, w_main, w_gate, qn, kn, seg, tm=x.shape[1],
                                  for_window_attn=False)
        o_a = _attn_ctx(p)
        o_c = _fft_ctx(p)
        o_d, s_fin = _gla(p, gla_up_f, b_f, gla_up_b, b_b, gn, seg, l, None)
        x = _outproj(x, mod_ctx[l], l, o_a, o_c, o_d, p, conv_w, w_out_b, tm=x.shape[1])
        ks.append(k_new)
        vs.append(v_new)
        ss.append(s_fin)
    y_prompt = x
    new_cache_k = jnp.stack(ks, axis=1)
    new_cache_v = jnp.stack(vs, axis=1)
    new_state = jnp.stack(ss, axis=1)

    xs = x_sample
    for l in range(DEPTH):
        p, k_hm, v_hm = _inproj(xs, mod_smp[l], l, norm_g3, w_main, w_gate, qn, kn, seg, tm=512,
                                for_window_attn=True)
        o_a = _attn_nbr(p, k_hm, v_hm, cache_k_b, cache_v_b, bias, l)
        o_c = _fft_grid(p)
        o_d, _ = _gla(p, gla_up_f, b_f, gla_up_b, b_b, gn, seg, l, state_gla)
        xs = _outproj(xs, mod_smp[l], l, o_a, o_c, o_d, p, conv_w, w_out_b, tm=512)
    return (y_prompt, xs, new_cache_k, new_cache_v, new_state)
```

```python
import functools
import math

import numpy as np
import jax
import jax.numpy as jnp
from jax import lax
from jax.experimental import pallas as pl
from jax.experimental.pallas import tpu as pltpu

F32 = jnp.float32
BF16 = jnp.bfloat16

D_MODEL = 1024
DEPTH = 4
GRID_W = 64
W_GRP = 256
HEAD_DIM = 64
N_HEADS = 4
WIN_H = 8
WIN_W = 16
N_FOURIER = 4
GATE_RANK = 16
GLA_TAU = 16.0
CHUNK = 64
RMS_EPS = 1e-6
N_IN = 14 * W_GRP + 2 * GATE_RANK
LANES = 128
N_IN_PAD = -(-N_IN // LANES) * LANES
G_QA, G_KA, G_VA, G_GA, G_BB, G_CB, G_HB, G_GB, G_UC, G_GC, G_QD, G_KD, G_VD, G_GD = range(14)
GATE_BLK = 14 * W_GRP // LANES
NEG_BIG = -1e30
VMEM_LIMIT = 56 * 1024 * 1024
TOKEN_TILE = 512


def _cparams(*sem):
    return pltpu.CompilerParams(dimension_semantics=sem, vmem_limit_bytes=VMEM_LIMIT)


def _bdot(a, b):
    return jnp.dot(a.astype(BF16), b.astype(BF16), preferred_element_type=F32)


def _bdot_nt(a, b):
    return lax.dot_general(a.astype(BF16), b.astype(BF16), (((1,), (1,)), ((), ())),
                           preferred_element_type=F32)


def _split2(x):
    hi = x.astype(BF16)
    lo = (x - hi.astype(F32)).astype(BF16)
    return hi, lo


def _split3(x):
    h1 = x.astype(BF16)
    r = x - h1.astype(F32)
    h2 = r.astype(BF16)
    h3 = (r - h2.astype(F32)).astype(BF16)
    return h1, h2, h3


def _silu(x):
    return x / (1.0 + jnp.exp(-x))


def _head_rms(t, seg, g):
    n = t.shape[0]
    both = jnp.dot(jnp.concatenate(_split2(t * t), axis=0), seg, preferred_element_type=F32)
    ss = both[:n] + both[n:]
    return t * lax.rsqrt(ss * (1.0 / HEAD_DIM) + RMS_EPS) * g


def _dft_cos_sin(n):
    k = np.arange(n)
    ang = 2.0 * np.pi * ((k[:, None] * k[None, :]) % n) / n
    return np.cos(ang), np.sin(ang)


def _block_diag(m, reps):
    n = m.shape[0]
    out = np.zeros((n * reps, n * reps), m.dtype)
    for i in range(reps):
        out[i * n:(i + 1) * n, i * n:(i + 1) * n] = m
    return out


def _seg_ones():
    return jnp.asarray(_block_diag(np.ones((HEAD_DIM, HEAD_DIM)), N_HEADS), BF16)


def _mod_kernel(c_ref, w_ref, b_ref, o_ref):
    o_ref[0] = _bdot(_silu(c_ref[...]), w_ref[0]) + b_ref[0]


def _modulation(cvec, w_mod, b_mod):
    rows = cvec.shape[0]
    return pl.pallas_call(
        _mod_kernel,
        grid=(DEPTH, 3),
        in_specs=[pl.BlockSpec((rows, D_MODEL), lambda l, j: (0, 0)),
                  pl.BlockSpec((1, D_MODEL, D_MODEL), lambda l, j: (l, 0, j)),
                  pl.BlockSpec((1, 1, D_MODEL), lambda l, j: (l, 0, j))],
        out_specs=pl.BlockSpec((1, rows, D_MODEL), lambda l, j: (l, 0, j)),
        out_shape=jax.ShapeDtypeStruct((DEPTH, rows, 3 * D_MODEL), F32),
        compiler_params=_cparams("arbitrary", "arbitrary"),
        name="adaln_mod",
    )(cvec, w_mod, b_mod.reshape(DEPTH, 1, 3 * D_MODEL))


INPROJ_ORDER = (2, 0, 3, 1, 4, 5, 6, 7, 8, 9, 10, 11, 12, 13)
INPROJ_PIECES = 8


def _inproj_kernel(x0_ref, xn_ref, mod0_ref, modn_ref, g_ref, w_ref, wg_ref, qn_ref, kn_ref, seg_ref, *rest,
                   for_window_attn, n_seq):
    p_ref, k_ref, v_ref, h_even, h_odd = rest[-5:]
    t = pl.program_id(0)
    rows = p_ref.shape[0] // n_seq

    def normed(x_ref, mod_ref, rows=slice(None)):
        x = x_ref[rows, :]
        ms = jnp.mean(x * x, axis=-1, keepdims=True)
        y = x * lax.rsqrt(ms + RMS_EPS) * g_ref[0]
        shift = mod_ref[0, :, 0:D_MODEL]
        scale = mod_ref[0, :, D_MODEL:2 * D_MODEL]
        return (y * (1.0 + scale) + shift).astype(BF16)

    def store_heads(ref, val, ones):
        for q in range(n_seq):
            for h in range(N_HEADS):
                th = val[q * rows:(q + 1) * rows, h * HEAD_DIM:(h + 1) * HEAD_DIM]
                if ones:
                    th = jnp.concatenate([th, jnp.ones_like(th)], axis=1)
                if for_window_attn:
                    ref[0, h] = th.astype(ref.dtype)
                else:
                    ref[q, 0, h] = th

    def step(h_cur, h_nxt):
        seg = seg_ref[...]
        piece = p_ref.shape[0] // INPROJ_PIECES
        first_late = len(INPROJ_ORDER) - INPROJ_PIECES
        for idx, j in enumerate(INPROJ_ORDER):
            if idx >= first_late:
                part = slice((idx - first_late) * piece, (idx - first_late + 1) * piece)
                h_nxt[part, :] = normed(xn_ref, modn_ref, part)
            cs = slice(j * W_GRP, (j + 1) * W_GRP)
            y = jnp.dot(h_cur[...], w_ref[0, :, cs], preferred_element_type=F32)
            if j == G_QA:
                y = _head_rms(y, seg, qn_ref[0]) * (HEAD_DIM ** -0.5)
            elif j == G_KA:
                y = _head_rms(y, seg, kn_ref[0])
            p_ref[:, cs] = y
            if j == G_KA:
                store_heads(k_ref, y, False)
            elif j == G_VA:
                store_heads(v_ref, y, for_window_attn)
        p_ref[:, 14 * W_GRP:N_IN_PAD] = jnp.dot(h_cur[...], wg_ref[0], preferred_element_type=F32)

    @pl.when(t == 0)
    def _():
        h_even[...] = normed(x0_ref, mod0_ref)

    @pl.when(t % 2 == 0)
    def _():
        step(h_even, h_odd)

    @pl.when(t % 2 == 1)
    def _():
        step(h_odd, h_even)


def _inproj(x, mod, layer, norm_g, w_main, w_gate, qn, kn, seg, *, tm, caches=None):
    b, s, _ = x.shape
    for_window_attn = caches is None
    assert s % tm == 0 if for_window_attn else tm % s == 0
    n_tiles = b * s // tm
    tiles_per_seq = max(s // tm, 1)
    n_seq = max(tm // s, 1)
    per_batch = mod.shape[0] > 1
    nxt = lambda t: jnp.minimum(t + 1, n_tiles - 1)
    mod_spec = lambda tile: pl.BlockSpec(
        (1, 1, 3 * D_MODEL), (lambda t: (tile(t) // tiles_per_seq, 0, 0)) if per_batch else (lambda t: (0, 0, 0)))
    lay = lambda *dims: pl.BlockSpec((1,) + dims, lambda t: (layer,) + (0,) * len(dims))
    in_specs = [pl.BlockSpec((tm, D_MODEL), lambda t: (0, 0)),
                pl.BlockSpec((tm, D_MODEL), lambda t: (nxt(t), 0)),
                mod_spec(lambda t: 0), mod_spec(nxt),
                lay(1, D_MODEL), lay(D_MODEL, 14 * W_GRP), lay(D_MODEL, LANES), lay(1, W_GRP), lay(1, W_GRP),
                pl.BlockSpec((W_GRP, W_GRP), lambda t: (0, 0))]
    x2 = x.reshape(b * s, D_MODEL)
    args = [x2, x2, mod, mod, norm_g, w_main, w_gate, qn, kn, seg]
    out_shape = [jax.ShapeDtypeStruct((b * s, N_IN_PAD), F32)]
    out_specs = [pl.BlockSpec((tm, N_IN_PAD), lambda t: (t, 0))]
    aliases = {}
    if for_window_attn:
        for width in (HEAD_DIM, 2 * HEAD_DIM):
            out_shape.append(jax.ShapeDtypeStruct((b, N_HEADS, s, width), BF16))
            out_specs.append(pl.BlockSpec((1, N_HEADS, tm, width),
                                          lambda t: (t // tiles_per_seq, 0, t % tiles_per_seq, 0)))
    else:
        for i, cache in enumerate(caches):
            aliases[len(args)] = 1 + i
            args.append(cache)
            in_specs.append(pl.BlockSpec(memory_space=pl.ANY))
            out_shape.append(jax.ShapeDtypeStruct(cache.shape, cache.dtype))
            out_specs.append(pl.BlockSpec((n_seq, 1, N_HEADS, s, HEAD_DIM), lambda t: (t, layer, 0, 0, 0)))
    p, k, v = pl.pallas_call(
        functools.partial(_inproj_kernel, for_window_attn=for_window_attn, n_seq=n_seq),
        grid=(n_tiles,),
        in_specs=in_specs,
        out_specs=out_specs,
        out_shape=out_shape,
        input_output_aliases=aliases,
        scratch_shapes=[pltpu.VMEM((tm, D_MODEL), BF16)] * 2,
        compiler_params=_cparams("arbitrary"),
        name="inproj_grid" if for_window_attn else "inproj_ctx",
    )(*args)
    return p.reshape(b, s, N_IN_PAD), k, v


def _attn_ctx_kernel(q_ref, k_ref, v_ref, o_ref):
    for h in range(N_HEADS):
        hs = slice(h * HEAD_DIM, (h + 1) * HEAD_DIM)
        s = _bdot_nt(q_ref[0, :, hs], k_ref[0, :, hs])
        e = jnp.exp(s - jnp.max(s, axis=-1, keepdims=True))
        l = jnp.sum(e, axis=-1, keepdims=True)
        o_ref[0, :, hs] = _bdot(e, v_ref[0, :, hs]) / l


def _attn_ctx(p):
    b, s, _ = p.shape
    col = lambda g: pl.BlockSpec((1, s, W_GRP), lambda i, g=g: (i, 0, g))
    return pl.pallas_call(
        _attn_ctx_kernel,
        grid=(b,),
        in_specs=[col(G_QA), col(G_KA), col(G_VA)],
        out_specs=pl.BlockSpec((1, s, W_GRP), lambda i: (i, 0, 0)),
        out_shape=jax.ShapeDtypeStruct((b, s, W_GRP), F32),
        compiler_params=_cparams("arbitrary"),
        name="attn_ctx",
    )(p, p, p)


def _bias_kernel(rpb_ref, o_ref):
    lh = pl.program_id(0)
    q = lax.broadcasted_iota(jnp.int32, (GRID_W, GRID_W), 0)
    kc = lax.broadcasted_iota(jnp.int32, (GRID_W, GRID_W), 1)
    dc = jnp.clip(kc - q + (WIN_W - 1), 0, 2 * WIN_W - 2)
    cs = jnp.clip(q - WIN_W // 2, 0, GRID_W - WIN_W)
    in_win = jnp.where(kc >= cs, jnp.where(kc < cs + WIN_W, 1, 0), 0) == 1
    n_dr, n_dc = 2 * WIN_H - 1, 2 * WIN_W - 1
    tiles = []
    for dr in range(n_dr):
        base = (lh * n_dr + dr) * n_dc
        t = jnp.zeros((GRID_W, GRID_W), F32)
        for d in range(n_dc):
            t = jnp.where(dc == d, rpb_ref[base + d], t)
        tiles.append(jnp.where(in_win, t, NEG_BIG))
    for dl in range(WIN_H):
        for i in range(WIN_H):
            o_ref[0, dl, :, i * GRID_W:(i + 1) * GRID_W] = tiles[dl + i]


def _bias_tables(rpb):
    out = pl.pallas_call(
        _bias_kernel,
        grid=(DEPTH * N_HEADS,),
        in_specs=[pl.BlockSpec(memory_space=pltpu.SMEM)],
        out_specs=pl.BlockSpec((1, WIN_H, GRID_W, WIN_H * GRID_W), lambda i: (i, 0, 0, 0)),
        out_shape=jax.ShapeDtypeStruct((DEPTH * N_HEADS, WIN_H, GRID_W, WIN_H * GRID_W), F32),
        compiler_params=_cparams("arbitrary"),
        name="rpb_tables",
    )(rpb.reshape(-1))
    return out.reshape(DEPTH, N_HEADS, WIN_H, GRID_W, WIN_H * GRID_W)


ATTN_ROW_GROUP = 2


def _attn_nbr_kernel(q_ref, k_ref, v_ref, ck_ref, cv_ref, bias_ref, o_ref, *, rows_per_step):
    rb = pl.program_id(1)
    n_rows = k_ref.shape[2] // GRID_W
    kh = min(WIN_H, n_rows)
    n_lat = kh * GRID_W
    heads = [slice(h * HEAD_DIM, (h + 1) * HEAD_DIM) for h in range(N_HEADS)]
    units = [(g, h) for g in range(ATTN_ROW_GROUP) for h in range(N_HEADS)]
    low_half = lax.broadcasted_iota(jnp.int32, (GRID_W, 2 * HEAD_DIM), 1) < HEAD_DIM

    def rows_body(jj, carry):
        q0, k0, dl, q = [], [], [], []
        for g in range(ATTN_ROW_GROUP):
            j = jj * ATTN_ROW_GROUP + g
            r = rb * rows_per_step + j
            rs = jnp.clip(r - kh // 2, 0, n_rows - kh)
            dl.append(rs - r + (WIN_H - 1))
            k0.append(pl.multiple_of(rs * GRID_W, GRID_W))
            q0.append(pl.multiple_of(j * GRID_W, GRID_W))
            q.append(q_ref[0, pl.ds(q0[g], GRID_W), :].astype(BF16))
        s_lat = [_bdot_nt(q[g][:, heads[h]], k_ref[0, h, pl.ds(k0[g], n_lat), :]) + bias_ref[0, h, dl[g]]
                 for g, h in units]
        s_ctx = [_bdot_nt(q[g][:, heads[h]], ck_ref[0, 0, h]) for g, h in units]
        m = [jnp.maximum(jnp.max(a, axis=-1, keepdims=True), jnp.max(c, axis=-1, keepdims=True))
             for a, c in zip(s_lat, s_ctx)]
        e_lat = [jnp.exp(a - mu).astype(BF16) for a, mu in zip(s_lat, m)]
        e_ctx = [jnp.exp(c - mu).astype(BF16) for c, mu in zip(s_ctx, m)]
        res = [jnp.dot(e_lat[u], v_ref[0, h, pl.ds(k0[g], n_lat), :], preferred_element_type=F32)
               + jnp.dot(e_ctx[u], cv_ref[0, 0, h], preferred_element_type=F32)
               for u, (g, h) in enumerate(units)]
        for g in range(ATTN_ROW_GROUP):
            pairs = []
            for h in range(0, N_HEADS, 2):
                even, odd = res[g * N_HEADS + h], res[g * N_HEADS + h + 1]
                pairs.append(jnp.where(low_half, even / pltpu.roll(even, HEAD_DIM, axis=1),
                                       pltpu.roll(odd, HEAD_DIM, axis=1) / odd))
            o_ref[0, pl.ds(q0[g], GRID_W), :] = jnp.concatenate(pairs, axis=1)
        return carry

    lax.fori_loop(0, rows_per_step // ATTN_ROW_GROUP, rows_body, 0)


def _attn_nbr(p, k_hm, v_hm, cache_k, cache_v, bias, layer, *, rows_per_step=16):
    b, s, _ = p.shape
    tq = rows_per_step * GRID_W
    past = cache_v.shape[3]
    ctx_spec = lambda *dims: pl.BlockSpec((1, 1, N_HEADS) + dims, lambda i, j: (i, layer, 0, 0, 0))
    return pl.pallas_call(
        functools.partial(_attn_nbr_kernel, rows_per_step=rows_per_step),
        grid=(b, s // tq),
        in_specs=[pl.BlockSpec((1, tq, W_GRP), lambda i, j: (i, j, G_QA)),
                  pl.BlockSpec((1, N_HEADS, s, HEAD_DIM), lambda i, j: (i, 0, 0, 0)),
                  pl.BlockSpec((1, N_HEADS, s, 2 * HEAD_DIM), lambda i, j: (i, 0, 0, 0)),
                  ctx_spec(past, HEAD_DIM), ctx_spec(past, 2 * HEAD_DIM),
                  pl.BlockSpec((1, N_HEADS, WIN_H, GRID_W, WIN_H * GRID_W), lambda i, j: (layer, 0, 0, 0, 0))],
        out_specs=pl.BlockSpec((1, tq, W_GRP), lambda i, j: (i, j, 0)),
        out_shape=jax.ShapeDtypeStruct((b, s, W_GRP), F32),
        compiler_params=_cparams("arbitrary", "arbitrary"),
        name="attn_nbr",
    )(p, k_hm, v_hm, cache_k, cache_v, bias)


def _fft_ctx_consts(s):
    c, sn = _dft_cos_sin(s)
    cc, sc = _dft_cos_sin(W_GRP // N_FOURIER)
    m1 = np.concatenate([c, -sn], axis=0)
    m2 = np.concatenate([_block_diag(cc, N_FOURIER), _block_diag(sc, N_FOURIER)], axis=0)
    return jnp.asarray(m1, F32), jnp.asarray(m2, F32)


def _fft_ctx_kernel(u_ref, m1_ref, m2_ref, o_ref, *, scale):
    s = u_ref.shape[1]
    t = _bdot(m1_ref[...], u_ref[0])
    x = jnp.concatenate([t[:s], t[s:]], axis=1)
    o_ref[0] = _bdot(x, m2_ref[...]) * scale


def _fft_ctx(p):
    b, s, _ = p.shape
    consts = _fft_ctx_consts(s)
    full = lambda a: pl.BlockSpec(a.shape, lambda i: (0,) * a.ndim)
    return pl.pallas_call(
        functools.partial(_fft_ctx_kernel, scale=1.0 / math.sqrt(s * (W_GRP // N_FOURIER))),
        grid=(b,),
        in_specs=[pl.BlockSpec((1, s, W_GRP), lambda i: (i, 0, G_UC))] + [full(a) for a in consts],
        out_specs=pl.BlockSpec((1, s, W_GRP), lambda i: (i, 0, 0)),
        out_shape=jax.ShapeDtypeStruct((b, s, W_GRP), F32),
        compiler_params=_cparams("arbitrary"),
        name="fft_ctx",
    )(p, *consts)


def _fft_grid_consts():
    n = GRID_W
    c, sn = _dft_cos_sin(n)
    cc, sc = _dft_cos_sin(W_GRP // N_FOURIER)
    m1 = np.concatenate([c, -sn], axis=0)
    m2 = np.block([[c, sn], [-sn, c]])
    reps = LANES // (W_GRP // N_FOURIER)
    m3 = np.concatenate([_block_diag(cc, reps), _block_diag(sc, reps)], axis=0)
    k = np.arange(n)
    ang = 2.0 * np.pi * (k[:, None] * k[None, :]) / (n * n)
    twr = np.broadcast_to(np.cos(ang)[:, :, None], (n, n, LANES))
    twi = np.broadcast_to(-np.sin(ang)[:, :, None], (n, n, LANES))
    return tuple(jnp.asarray(m, F32) for m in (m1, m2, m3, twr, twi))


FFT_UNROLL = 8
FFT_PITCH = 72


def _fft_grid_kernel(u_ref, m1_ref, m2_ref, m3_ref, twr_ref, twi_ref, o_ref, tr_scr, ti_scr, *, scale):
    n = GRID_W
    m1, m2, m3 = (r[...].astype(BF16) for r in (m1_ref, m2_ref, m3_ref))

    def stage1(b, carry):
        a_rows = u_ref[0, pl.ds(b, n, stride=n), :]
        t = _bdot(m1, a_rows)
        tr, ti = t[:n], t[n:]
        wr, wi = twr_ref[b], twi_ref[b]
        tr_scr[pl.ds(b, n, stride=FFT_PITCH), :] = tr * wr - ti * wi
        ti_scr[pl.ds(b, n, stride=FFT_PITCH), :] = tr * wi + ti * wr
        return carry

    lax.fori_loop(0, n, stage1, 0, unroll=FFT_UNROLL)

    def stage2(d, carry):
        r0 = pl.multiple_of(d * FFT_PITCH, 8)
        g = jnp.concatenate([tr_scr[pl.ds(r0, n), :], ti_scr[pl.ds(r0, n), :]], axis=0)
        x = _bdot(m2, g)
        xc = jnp.concatenate([x[:n], x[n:]], axis=1)
        o_ref[0, pl.ds(d, n, stride=n), :] = _bdot(xc, m3) * scale
        return carry

    lax.fori_loop(0, n, stage2, 0, unroll=FFT_UNROLL)


def _fft_grid(p):
    b, s, _ = p.shape
    assert s == GRID_W * GRID_W
    consts = _fft_grid_consts()
    full = lambda a: pl.BlockSpec(a.shape, lambda i, j: (0,) * a.ndim)
    blk0 = G_UC * W_GRP // LANES
    return pl.pallas_call(
        functools.partial(_fft_grid_kernel, scale=1.0 / math.sqrt(s * (W_GRP // N_FOURIER))),
        grid=(b, W_GRP // LANES),
        in_specs=[pl.BlockSpec((1, s, LANES), lambda i, j: (i, 0, blk0 + j))] + [full(a) for a in consts],
        out_specs=pl.BlockSpec((1, s, LANES), lambda i, j: (i, 0, j)),
        out_shape=jax.ShapeDtypeStruct((b, s, W_GRP), F32),
        scratch_shapes=[pltpu.VMEM((GRID_W * FFT_PITCH, LANES), F32)] * 2,
        compiler_params=_cparams("arbitrary", "arbitrary"),
        name="fft_grid",
    )(p, *consts)


GLA_GROUP = 4


def _gla_kernel(*refs, n_chunks, zero_init):
    if zero_init:
        (q_ref, k_ref, v_ref, a_ref, upf_ref, bf_ref, upb_ref, bb_ref, gn_ref, seg_ref, _,
         o_ref, sfin_ref, qd_scr, st_scr, dec_scr) = refs
        s0_ref = None
    else:
        (q_ref, k_ref, v_ref, a_ref, upf_ref, bf_ref, upb_ref, bb_ref, gn_ref, seg_ref, s0_ref,
         o_ref, qd_scr, st_scr, dec_scr) = refs
        sfin_ref = None
    c = CHUNK
    row = lax.broadcasted_iota(jnp.int32, (c, c), 0)
    col = lax.broadcasted_iota(jnp.int32, (c, c), 1)
    heads = [slice(h * HEAD_DIM, (h + 1) * HEAD_DIM) for h in range(N_HEADS)]
    wrow = lax.broadcasted_iota(jnp.int32, (W_GRP, W_GRP), 0)
    wcol = lax.broadcasted_iota(jnp.int32, (W_GRP, W_GRP), 1)
    head_diag = (wrow // HEAD_DIM) == (wcol // HEAD_DIM)

    def blockdiag(x):
        return jnp.where(head_diag, jnp.concatenate([x] * N_HEADS, axis=0), jnp.zeros((), x.dtype))

    trow = lax.broadcasted_iota(jnp.int32, (c, W_GRP), 0)
    tcol = lax.broadcasted_iota(jnp.int32, (c, W_GRP), 1) % c
    keep = [trow >= tcol, trow <= tcol]
    eye_tiled = jnp.where(trow == tcol, 1.0, 0.0).astype(BF16)

    def head_transpose_exact(x):
        return sum(_bdot_nt(eye_tiled, blockdiag(part)) for part in _split3(x))

    sum_ops = [jnp.where(row >= col, 1.0, 0.0).astype(BF16),
               jnp.where(row <= col, 1.0, 0.0).astype(BF16)]
    dir_params = [(upf_ref, bf_ref), (upb_ref, bb_ref)]
    dirs = (0, 1)

    def chunk_rows(n):
        return pl.ds(pl.multiple_of(n * c, c), c)

    def phase1(i, carry):
        ns = [i * GLA_GROUP + g for g in range(GLA_GROUP)]
        rows = [chunk_rows(n) for n in ns]
        a = [a_ref[0, r, :] for r in rows]
        x = [[_bdot(ag[:, d * GATE_RANK:(d + 1) * GATE_RANK], dir_params[d][0][0]) + dir_params[d][1][0]
              for d in dirs] for ag in a]
        la = [[(jnp.minimum(xd, 0.0) - jnp.log(1.0 + jnp.exp(-jnp.abs(xd)))) * (1.0 / GLA_TAU) for xd in xg]
              for xg in x]
        cum = [[sum(jnp.dot(sum_ops[d], part, preferred_element_type=F32) for part in _split3(lg[d]))
                for d in dirs] for lg in la]
        tot_row = [[cg[0][c - 1:c, :], cg[1][0:1, :]] for cg in cum]
        q = [q_ref[0, r, :] * (HEAD_DIM ** -0.5) for r in rows]
        k = [k_ref[0, r, :] for r in rows]
        v_bd = [blockdiag(v_ref[0, r, :].astype(BF16)) for r in rows]
        q_dec = [[(q[g] * jnp.exp(cum[g][d])).astype(BF16) for d in dirs] for g in range(GLA_GROUP)]
        k_inv = [[blockdiag((k[g] * jnp.exp(-cum[g][d])).astype(BF16)) for d in dirs] for g in range(GLA_GROUP)]
        k_end = [[blockdiag((k[g] * jnp.exp(tot_row[g][d] - cum[g][d])).astype(BF16)) for d in dirs]
                 for g in range(GLA_GROUP)]
        scores = [[_bdot_nt(q_dec[g][d], k_inv[g][d]) for d in dirs] for g in range(GLA_GROUP)]
        v_t = [_bdot_nt(eye_tiled, vb).astype(BF16) for vb in v_bd]
        upd = [[_bdot(v_t[g], k_end[g][d]) for d in dirs] for g in range(GLA_GROUP)]
        att = [[jnp.where(keep[d], scores[g][d], 0.0).astype(BF16) for d in dirs] for g in range(GLA_GROUP)]
        for g in range(GLA_GROUP):
            both = _bdot(jnp.concatenate(att[g], axis=0), v_bd[g])
            o_ref[0, rows[g], :] = both[:c] + both[c:]
            for d in dirs:
                qd_scr[d, rows[g], :] = q_dec[g][d]
                dec_scr[d, pl.ds(ns[g], 1), :] = jnp.exp(tot_row[g][d])
                st_scr[d, ns[g]] = upd[g][d]
        return carry

    lax.fori_loop(0, n_chunks // GLA_GROUP, phase1, 0)

    def init_state(d):
        if zero_init:
            return jnp.zeros((HEAD_DIM, W_GRP), F32)
        return head_transpose_exact(jnp.concatenate([s0_ref[0, 0, d, h] for h in range(N_HEADS)], axis=1))

    finals = []
    for d in range(2):
        def scan_body(i, st, d=d):
            n = i if d == 0 else n_chunks - 1 - i
            upd = st_scr[d, n]
            st_scr[d, n] = st
            return st * dec_scr[d, pl.ds(n, 1), :] + upd
        finals.append(lax.fori_loop(0, n_chunks, scan_body, init_state(d)))

    def phase3(i, carry):
        ns = [i * GLA_GROUP + g for g in range(GLA_GROUP)]
        rows = [chunk_rows(n) for n in ns]
        st_bd = [[blockdiag(st_scr[d, n].astype(BF16)) for d in dirs] for n in ns]
        inter = [[_bdot_nt(qd_scr[d, rows[g], :], st_bd[g][d]) for d in dirs] for g in range(GLA_GROUP)]
        o = jnp.concatenate([o_ref[0, rows[g], :] + inter[g][0] + inter[g][1] for g in range(GLA_GROUP)], axis=0)
        o = _head_rms(o, seg_ref[...], gn_ref[0])
        for g in range(GLA_GROUP):
            o_ref[0, rows[g], :] = o[g * c:(g + 1) * c]
        return carry

    lax.fori_loop(0, n_chunks // GLA_GROUP, phase3, 0)
    if sfin_ref is not None:
        for d in range(2):
            s_fin = head_transpose_exact(finals[d])
            for h, hs in enumerate(heads):
                sfin_ref[0, 0, d, h] = s_fin[:, hs]


def _gla(p, up_f, b_f, up_b, b_b, gn, seg, layer, state, *, zero_init):
    b, s, _ = p.shape
    n_chunks = s // CHUNK
    col = lambda g: pl.BlockSpec((1, s, W_GRP), lambda i, g=g: (i, 0, g))
    lay = lambda shape: pl.BlockSpec((1,) + shape, lambda i: (layer,) + (0,) * len(shape))
    in_specs = [col(G_QD), col(G_KD), col(G_VD),
                pl.BlockSpec((1, s, LANES), lambda i: (i, 0, GATE_BLK)),
                lay((GATE_RANK, W_GRP)), lay((1, W_GRP)), lay((GATE_RANK, W_GRP)), lay((1, W_GRP)),
                lay((1, W_GRP)), pl.BlockSpec((W_GRP, W_GRP), lambda i: (0, 0))]
    args = [p, p, p, p, up_f, b_f, up_b, b_b, gn, seg, state]
    state_spec = pl.BlockSpec((1, 1, 2, N_HEADS, HEAD_DIM, HEAD_DIM), lambda i: (i, layer, 0, 0, 0, 0))
    out_specs = [pl.BlockSpec((1, s, W_GRP), lambda i: (i, 0, 0))]
    out_shape = [jax.ShapeDtypeStruct((b, s, W_GRP), F32)]
    if zero_init:
        in_specs.append(pl.BlockSpec(memory_space=pl.ANY))
        out_specs.append(state_spec)
        out_shape.append(jax.ShapeDtypeStruct(state.shape, state.dtype))
    else:
        in_specs.append(state_spec)
    return pl.pallas_call(
        functools.partial(_gla_kernel, n_chunks=n_chunks, zero_init=zero_init),
        grid=(b,),
        in_specs=in_specs,
        out_specs=out_specs,
        out_shape=out_shape,
        input_output_aliases={len(args) - 1: 1} if zero_init else {},
        scratch_shapes=[pltpu.VMEM((2, s, W_GRP), BF16),
                        pltpu.VMEM((2, n_chunks, HEAD_DIM, W_GRP), F32),
                        pltpu.VMEM((2, max(n_chunks, 8), W_GRP), F32)],
        compiler_params=_cparams("arbitrary"),
        name="gla_zero" if zero_init else "gla",
    )(*args)


def _outproj_kernel(x_ref, mod_ref, oa_ref, oc_ref, od_ref, ga_ref, bb_ref, cb_ref, hb_ref, gb_ref,
                    gc_ref, gd_ref, cprev_ref, hprev_ref, cnext_ref, hnext_ref, cw_ref, w_ref, o_ref, *, seq_len):
    tm = x_ref.shape[0]
    u = cb_ref[...] * hb_ref[...]
    rid = lax.broadcasted_iota(jnp.int32, u.shape, 0)
    pos = (pl.program_id(0) * tm + rid) % seq_len
    u_prev = jnp.where(rid == 0, cprev_ref[7:8, :] * hprev_ref[7:8, :], pltpu.roll(u, 1, axis=0))
    u_next = jnp.where(rid == tm - 1, cnext_ref[0:1, :] * hnext_ref[0:1, :], pltpu.roll(u, tm - 1, axis=0))
    u_prev = jnp.where(pos == 0, 0.0, u_prev)
    u_next = jnp.where(pos == seq_len - 1, 0.0, u_next)
    cw = cw_ref[0]
    conv = u_prev * cw[0:1, :] + u * cw[1:2, :] + u_next * cw[2:3, :]
    y_a = oa_ref[...] * _silu(ga_ref[...])
    y_b = bb_ref[...] * conv * _silu(gb_ref[...])
    y_c = oc_ref[...] * _silu(gc_ref[...])
    y_d = od_ref[...] * _silu(gd_ref[...])
    y = jnp.concatenate([y_a, y_b, y_c, y_d], axis=1).astype(BF16)
    gate = mod_ref[0, :, 2 * D_MODEL:3 * D_MODEL]
    o_ref[...] = x_ref[...] + gate * jnp.dot(y, w_ref[0], preferred_element_type=F32)


def _outproj(x, mod, layer, o_a, o_c, o_d, p, conv_w, w_out, *, tm):
    b, s, _ = x.shape
    n = b * s
    per_batch = mod.shape[0] > 1
    assert s % tm == 0 or (tm % s == 0 and not per_batch)
    tiles_per_seq = max(s // tm, 1)
    flat = lambda a: a.reshape(n, a.shape[-1])
    tile = lambda width: pl.BlockSpec((tm, width), lambda t: (t, 0))
    col = lambda g: pl.BlockSpec((tm, W_GRP), lambda t, g=g: (t, g))
    t8 = tm // 8
    prev = lambda g: pl.BlockSpec((8, W_GRP), lambda t, g=g: (jnp.maximum(t * t8 - 1, 0), g))
    nxt = lambda g: pl.BlockSpec((8, W_GRP), lambda t, g=g: (jnp.minimum((t + 1) * t8, n // 8 - 1), g))
    out = pl.pallas_call(
        functools.partial(_outproj_kernel, seq_len=s),
        grid=(n // tm,),
        in_specs=[tile(D_MODEL),
                  pl.BlockSpec((1, 1, 3 * D_MODEL),
                               (lambda t: (t // tiles_per_seq, 0, 0)) if per_batch else (lambda t: (0, 0, 0))),
                  tile(W_GRP), tile(W_GRP), tile(W_GRP),
                  col(G_GA), col(G_BB), col(G_CB), col(G_HB), col(G_GB), col(G_GC), col(G_GD),
                  prev(G_CB), prev(G_HB), nxt(G_CB), nxt(G_HB),
                  pl.BlockSpec((1, 3, W_GRP), lambda t: (layer, 0, 0)),
                  pl.BlockSpec((1, D_MODEL, D_MODEL), lambda t: (layer, 0, 0))],
        out_specs=tile(D_MODEL),
        out_shape=jax.ShapeDtypeStruct((n, D_MODEL), F32),
        compiler_params=_cparams("arbitrary"),
        name="outproj",
    )(flat(x), mod, flat(o_a), flat(o_c), flat(o_d), *([flat(p)] * 11), conv_w, w_out)
    return out.reshape(b, s, D_MODEL)


def kernel(x_prompt, x_sample, cache_k, cache_v, state_gla, c, c_ctx, norm_g, w_mod, b_mod, w_in, q_norm_g,
           k_norm_g, rpb, conv_w, gla_up_f, gla_bias_f, gla_up_b, gla_bias_b, gla_norm_g, w_out):
    dec_batch = c.shape[0]
    pad_rows = 8 - (1 + dec_batch) % 8
    cvec = jnp.concatenate([c_ctx[None], c, jnp.zeros((pad_rows, D_MODEL), F32)], axis=0)
    mod = _modulation(cvec, w_mod, b_mod)
    mod_ctx = mod[:, 0:1].reshape(DEPTH, 1, 1, 3 * D_MODEL)
    mod_smp = mod[:, 1:1 + dec_batch].reshape(DEPTH, dec_batch, 1, 3 * D_MODEL)

    w_main = w_in[:, :, :14 * W_GRP].astype(BF16)
    w_gate = jnp.pad(w_in[:, :, 14 * W_GRP:], ((0, 0), (0, 0), (0, N_IN_PAD - N_IN))).astype(BF16)
    w_out_b = w_out.astype(BF16)
    seg = _seg_ones()
    norm_g3 = norm_g.reshape(DEPTH, 1, D_MODEL)
    qn = jnp.tile(q_norm_g, (1, N_HEADS)).reshape(DEPTH, 1, W_GRP)
    kn = jnp.tile(k_norm_g, (1, N_HEADS)).reshape(DEPTH, 1, W_GRP)
    gn = jnp.tile(gla_norm_g, (1, N_HEADS)).reshape(DEPTH, 1, W_GRP)
    b_f = gla_bias_f.reshape(DEPTH, 1, W_GRP)
    b_b = gla_bias_b.reshape(DEPTH, 1, W_GRP)
    bias = _bias_tables(rpb)
    cache_k_b = cache_k.astype(BF16)
    cache_v_b = jnp.concatenate([cache_v.astype(BF16), jnp.ones(cache_v.shape, BF16)], axis=-1)

    x = x_prompt
    batch, seq, _ = x.shape
    new_cache_k = jnp.zeros((batch, DEPTH, N_HEADS, seq, HEAD_DIM), F32)
    new_cache_v = jnp.zeros((batch, DEPTH, N_HEADS, seq, HEAD_DIM), F32)
    new_state = jnp.zeros((batch, DEPTH, 2, N_HEADS, HEAD_DIM, HEAD_DIM), F32)
    for l in range(DEPTH):
        p, new_cache_k, new_cache_v = _inproj(x, mod_ctx[l], l, norm_g3, w_main, w_gate, qn, kn, seg,
                                              tm=TOKEN_TILE, caches=(new_cache_k, new_cache_v))
        o_a = _attn_ctx(p)
        o_c = _fft_ctx(p)
        o_d, new_state = _gla(p, gla_up_f, b_f, gla_up_b, b_b, gn, seg, l, new_state, zero_init=True)
        x = _outproj(x, mod_ctx[l], l, o_a, o_c, o_d, p, conv_w, w_out_b, tm=TOKEN_TILE)
    y_prompt = x

    xs = x_sample
    for l in range(DEPTH):
        p, k_hm, v_hm = _inproj(xs, mod_smp[l], l, norm_g3, w_main, w_gate, qn, kn, seg, tm=TOKEN_TILE)
        o_a = _attn_nbr(p, k_hm, v_hm, cache_k_b, cache_v_b, bias, l)
        o_c = _fft_grid(p)
        (o_d,) = _gla(p, gla_up_f, b_f, gla_up_b, b_b, gn, seg, l, state_gla, zero_init=False)
        xs = _outproj(xs, mod_smp[l], l, o_a, o_c, o_d, p, conv_w, w_out_b, tm=TOKEN_TILE)
    return (y_prompt, xs, new_cache_k, new_cache_v, new_state)
```

```python
import functools
import math

import numpy as np
import jax
import jax.numpy as jnp
from jax import lax
from jax.experimental import pallas as pl
from jax.experimental.pallas import tpu as pltpu

F32 = jnp.float32
BF16 = jnp.bfloat16

D_MODEL = 1024
DEPTH = 4
GRID_W = 64
W_GRP = 256
HEAD_DIM = 64
N_HEADS = 4
WIN_H = 8
WIN_W = 16
N_FOURIER = 4
GATE_RANK = 16
GLA_TAU = 16.0
CHUNK = 64
RMS_EPS = 1e-6
N_IN = 14 * W_GRP + 2 * GATE_RANK
LANES = 128
N_IN_PAD = -(-N_IN // LANES) * LANES
G_QA, G_KA, G_VA, G_GA, G_BB, G_CB, G_HB, G_GB, G_UC, G_GC, G_QD, G_KD, G_VD, G_GD = range(14)
F32_GROUPS = (G_GA, G_BB, G_CB, G_HB, G_GB, G_UC, G_GC, G_QD, G_KD, G_GD)
BF16_GROUPS = (G_QA, G_KA, G_VA, G_VD)
P32 = {g: i for i, g in enumerate(F32_GROUPS)}
P16 = {g: i for i, g in enumerate(BF16_GROUPS)}
N_P32 = len(F32_GROUPS) * W_GRP
N_P16 = len(BF16_GROUPS) * W_GRP + LANES
GATE_BLK = len(BF16_GROUPS) * W_GRP // LANES
NEG_BIG = -1e30
VMEM_LIMIT = 56 * 1024 * 1024
TOKEN_TILE = 512


def _cparams(*sem):
    return pltpu.CompilerParams(dimension_semantics=sem, vmem_limit_bytes=VMEM_LIMIT)


def _bdot(a, b):
    return jnp.dot(a.astype(BF16), b.astype(BF16), preferred_element_type=F32)


def _bdot_nt(a, b):
    return lax.dot_general(a.astype(BF16), b.astype(BF16), (((1,), (1,)), ((), ())),
                           preferred_element_type=F32)


def _split2(x):
    hi = x.astype(BF16)
    lo = (x - hi.astype(F32)).astype(BF16)
    return hi, lo


def _split3(x):
    h1 = x.astype(BF16)
    r = x - h1.astype(F32)
    h2 = r.astype(BF16)
    h3 = (r - h2.astype(F32)).astype(BF16)
    return h1, h2, h3


def _silu(x):
    return x / (1.0 + jnp.exp(-x))


def _head_rms(t, seg, g):
    n = t.shape[0]
    both = jnp.dot(jnp.concatenate(_split2(t * t), axis=0), seg, preferred_element_type=F32)
    ss = both[:n] + both[n:]
    return t * lax.rsqrt(ss * (1.0 / HEAD_DIM) + RMS_EPS) * g


def _dft_cos_sin(n):
    k = np.arange(n)
    ang = 2.0 * np.pi * ((k[:, None] * k[None, :]) % n) / n
    return np.cos(ang), np.sin(ang)


def _block_diag(m, reps):
    n = m.shape[0]
    out = np.zeros((n * reps, n * reps), m.dtype)
    for i in range(reps):
        out[i * n:(i + 1) * n, i * n:(i + 1) * n] = m
    return out


def _seg_ones():
    return jnp.asarray(_block_diag(np.ones((HEAD_DIM, HEAD_DIM)), N_HEADS), BF16)


def _mod_kernel(c_ref, w_ref, b_ref, o_ref):
    o_ref[0] = _bdot(_silu(c_ref[...]), w_ref[0]) + b_ref[0]


def _modulation(cvec, w_mod, b_mod):
    rows = cvec.shape[0]
    return pl.pallas_call(
        _mod_kernel,
        grid=(DEPTH, 3),
        in_specs=[pl.BlockSpec((rows, D_MODEL), lambda l, j: (0, 0)),
                  pl.BlockSpec((1, D_MODEL, D_MODEL), lambda l, j: (l, 0, j)),
                  pl.BlockSpec((1, 1, D_MODEL), lambda l, j: (l, 0, j))],
        out_specs=pl.BlockSpec((1, rows, D_MODEL), lambda l, j: (l, 0, j)),
        out_shape=jax.ShapeDtypeStruct((DEPTH, rows, 3 * D_MODEL), F32),
        compiler_params=_cparams("arbitrary", "arbitrary"),
        name="adaln_mod",
    )(cvec, w_mod, b_mod.reshape(DEPTH, 1, 3 * D_MODEL))


def _cast_w_in_kernel(w_ref, main_ref, gate_ref):
    w = w_ref[0]
    rows = w.shape[0]
    main_ref[0] = w[:, :14 * W_GRP].astype(BF16)
    pad = jnp.zeros((rows, N_IN_PAD - N_IN), F32)
    gate_ref[0] = jnp.concatenate([w[:, 14 * W_GRP:], pad], axis=1).astype(BF16)


def _cast_w_in(w_in, *, rows=256):
    return pl.pallas_call(
        _cast_w_in_kernel,
        grid=(DEPTH, D_MODEL // rows),
        in_specs=[pl.BlockSpec((1, rows, N_IN), lambda l, i: (l, i, 0))],
        out_specs=[pl.BlockSpec((1, rows, 14 * W_GRP), lambda l, i: (l, i, 0)),
                   pl.BlockSpec((1, rows, LANES), lambda l, i: (l, i, 0))],
        out_shape=[jax.ShapeDtypeStruct((DEPTH, D_MODEL, 14 * W_GRP), BF16),
                   jax.ShapeDtypeStruct((DEPTH, D_MODEL, LANES), BF16)],
        compiler_params=_cparams("arbitrary", "arbitrary"),
        name="cast_w_in",
    )(w_in)


INPROJ_ORDER = (2, 0, 3, 1, 4, 5, 6, 7, 8, 9, 10, 11, 12, 13)
INPROJ_PIECES = 8


def _inproj_kernel(x0_ref, xn_ref, mod0_ref, modn_ref, g_ref, w_ref, wg_ref, qn_ref, kn_ref, seg_ref, *rest,
                   for_window_attn, n_seq):
    p32_ref, p16_ref, k_ref, v_ref, h_even, h_odd = rest[-6:]
    t = pl.program_id(0)
    rows = p32_ref.shape[0] // n_seq

    def normed(x_ref, mod_ref, rows=slice(None)):
        x = x_ref[rows, :]
        ms = jnp.mean(x * x, axis=-1, keepdims=True)
        y = x * lax.rsqrt(ms + RMS_EPS) * g_ref[0]
        shift = mod_ref[0, :, 0:D_MODEL]
        scale = mod_ref[0, :, D_MODEL:2 * D_MODEL]
        return (y * (1.0 + scale) + shift).astype(BF16)

    def store_heads(ref, val, ones):
        for q in range(n_seq):
            for h in range(N_HEADS):
                th = val[q * rows:(q + 1) * rows, h * HEAD_DIM:(h + 1) * HEAD_DIM]
                if ones:
                    th = jnp.concatenate([th, jnp.ones_like(th)], axis=1)
                if for_window_attn:
                    ref[0, h] = th.astype(ref.dtype)
                else:
                    ref[q, 0, h] = th

    def step(h_cur, h_nxt):
        seg = seg_ref[...]
        piece = p32_ref.shape[0] // INPROJ_PIECES
        first_late = len(INPROJ_ORDER) - INPROJ_PIECES
        for idx, j in enumerate(INPROJ_ORDER):
            if idx >= first_late:
                part = slice((idx - first_late) * piece, (idx - first_late + 1) * piece)
                h_nxt[part, :] = normed(xn_ref, modn_ref, part)
            cs = slice(j * W_GRP, (j + 1) * W_GRP)
            y = jnp.dot(h_cur[...], w_ref[0, :, cs], preferred_element_type=F32)
            if j == G_QA:
                y = _head_rms(y, seg, qn_ref[0]) * (HEAD_DIM ** -0.5)
            elif j == G_KA:
                y = _head_rms(y, seg, kn_ref[0])
            if j in P32:
                p32_ref[:, P32[j] * W_GRP:(P32[j] + 1) * W_GRP] = y
            else:
                p16_ref[:, P16[j] * W_GRP:(P16[j] + 1) * W_GRP] = y.astype(BF16)
            if j == G_KA:
                store_heads(k_ref, y, False)
            elif j == G_VA:
                store_heads(v_ref, y, for_window_attn)
        p16_ref[:, GATE_BLK * LANES:N_P16] = jnp.dot(h_cur[...], wg_ref[0], preferred_element_type=F32).astype(BF16)

    @pl.when(t == 0)
    def _():
        h_even[...] = normed(x0_ref, mod0_ref)

    @pl.when(t % 2 == 0)
    def _():
        step(h_even, h_odd)

    @pl.when(t % 2 == 1)
    def _():
        step(h_odd, h_even)


def _inproj(x, mod, layer, norm_g, w_main, w_gate, qn, kn, seg, *, tm, caches=None):
    b, s, _ = x.shape
    for_window_attn = caches is None
    assert s % tm == 0 if for_window_attn else tm % s == 0
    n_tiles = b * s // tm
    tiles_per_seq = max(s // tm, 1)
    n_seq = max(tm // s, 1)
    per_batch = mod.shape[0] > 1
    nxt = lambda t: jnp.minimum(t + 1, n_tiles - 1)
    mod_spec = lambda tile: pl.BlockSpec(
        (1, 1, 3 * D_MODEL), (lambda t: (tile(t) // tiles_per_seq, 0, 0)) if per_batch else (lambda t: (0, 0, 0)))
    lay = lambda *dims: pl.BlockSpec((1,) + dims, lambda t: (layer,) + (0,) * len(dims))
    in_specs = [pl.BlockSpec((tm, D_MODEL), lambda t: (0, 0)),
                pl.BlockSpec((tm, D_MODEL), lambda t: (nxt(t), 0)),
                mod_spec(lambda t: 0), mod_spec(nxt),
                lay(1, D_MODEL), lay(D_MODEL, 14 * W_GRP), lay(D_MODEL, LANES), lay(1, W_GRP), lay(1, W_GRP),
                pl.BlockSpec((W_GRP, W_GRP), lambda t: (0, 0))]
    x2 = x.reshape(b * s, D_MODEL)
    args = [x2, x2, mod, mod, norm_g, w_main, w_gate, qn, kn, seg]
    out_shape = [jax.ShapeDtypeStruct((b * s, N_P32), F32), jax.ShapeDtypeStruct((b * s, N_P16), BF16)]
    out_specs = [pl.BlockSpec((tm, N_P32), lambda t: (t, 0)), pl.BlockSpec((tm, N_P16), lambda t: (t, 0))]
    aliases = {}
    if for_window_attn:
        for width in (HEAD_DIM, 2 * HEAD_DIM):
            out_shape.append(jax.ShapeDtypeStruct((b, N_HEADS, s, width), BF16))
            out_specs.append(pl.BlockSpec((1, N_HEADS, tm, width),
                                          lambda t: (t // tiles_per_seq, 0, t % tiles_per_seq, 0)))
    else:
        for i, cache in enumerate(caches):
            aliases[len(args)] = 2 + i
            args.append(cache)
            in_specs.append(pl.BlockSpec(memory_space=pl.ANY))
            out_shape.append(jax.ShapeDtypeStruct(cache.shape, cache.dtype))
            out_specs.append(pl.BlockSpec((n_seq, 1, N_HEADS, s, HEAD_DIM), lambda t: (t, layer, 0, 0, 0)))
    p32, p16, k, v = pl.pallas_call(
        functools.partial(_inproj_kernel, for_window_attn=for_window_attn, n_seq=n_seq),
        grid=(n_tiles,),
        in_specs=in_specs,
        out_specs=out_specs,
        out_shape=out_shape,
        input_output_aliases=aliases,
        scratch_shapes=[pltpu.VMEM((tm, D_MODEL), BF16)] * 2,
        compiler_params=_cparams("arbitrary"),
        name="inproj_grid" if for_window_attn else "inproj_ctx",
    )(*args)
    return p32.reshape(b, s, N_P32), p16.reshape(b, s, N_P16), k, v


def _attn_ctx_kernel(q_ref, k_ref, v_ref, o_ref):
    for h in range(N_HEADS):
        hs = slice(h * HEAD_DIM, (h + 1) * HEAD_DIM)
        s = _bdot_nt(q_ref[0, :, hs], k_ref[0, :, hs])
        e = jnp.exp(s - jnp.max(s, axis=-1, keepdims=True))
        l = jnp.sum(e, axis=-1, keepdims=True)
        o_ref[0, :, hs] = _bdot(e, v_ref[0, :, hs]) / l


def _attn_ctx(p):
    b, s, _ = p.shape
    col = lambda g: pl.BlockSpec((1, s, W_GRP), lambda i, g=g: (i, 0, g))
    return pl.pallas_call(
        _attn_ctx_kernel,
        grid=(b,),
        in_specs=[col(P16[G_QA]), col(P16[G_KA]), col(P16[G_VA])],
        out_specs=pl.BlockSpec((1, s, W_GRP), lambda i: (i, 0, 0)),
        out_shape=jax.ShapeDtypeStruct((b, s, W_GRP), F32),
        compiler_params=_cparams("arbitrary"),
        name="attn_ctx",
    )(p, p, p)


def _bias_kernel(rpb_ref, o_ref):
    lh = pl.program_id(0)
    q = lax.broadcasted_iota(jnp.int32, (GRID_W, GRID_W), 0)
    kc = lax.broadcasted_iota(jnp.int32, (GRID_W, GRID_W), 1)
    dc = jnp.clip(kc - q + (WIN_W - 1), 0, 2 * WIN_W - 2)
    cs = jnp.clip(q - WIN_W // 2, 0, GRID_W - WIN_W)
    in_win = jnp.where(kc >= cs, jnp.where(kc < cs + WIN_W, 1, 0), 0) == 1
    n_dr, n_dc = 2 * WIN_H - 1, 2 * WIN_W - 1
    tiles = []
    for dr in range(n_dr):
        base = (lh * n_dr + dr) * n_dc
        t = jnp.zeros((GRID_W, GRID_W), F32)
        for d in range(n_dc):
            t = jnp.where(dc == d, rpb_ref[base + d], t)
        tiles.append(jnp.where(in_win, t, NEG_BIG))
    for dl in range(WIN_H):
        for i in range(WIN_H):
            o_ref[0, dl, :, i * GRID_W:(i + 1) * GRID_W] = tiles[dl + i]


def _bias_tables(rpb):
    out = pl.pallas_call(
        _bias_kernel,
        grid=(DEPTH * N_HEADS,),
        in_specs=[pl.BlockSpec(memory_space=pltpu.SMEM)],
        out_specs=pl.BlockSpec((1, WIN_H, GRID_W, WIN_H * GRID_W), lambda i: (i, 0, 0, 0)),
        out_shape=jax.ShapeDtypeStruct((DEPTH * N_HEADS, WIN_H, GRID_W, WIN_H * GRID_W), F32),
        compiler_params=_cparams("arbitrary"),
        name="rpb_tables",
    )(rpb.reshape(-1))
    return out.reshape(DEPTH, N_HEADS, WIN_H, GRID_W, WIN_H * GRID_W)


ATTN_ROW_GROUP = 4


def _attn_nbr_kernel(q_ref, k_ref, v_ref, ck_ref, cv_ref, bias_ref, o_ref, *, rows_per_step):
    rb = pl.program_id(1)
    n_rows = k_ref.shape[2] // GRID_W
    kh = min(WIN_H, n_rows)
    n_lat = kh * GRID_W
    heads = [slice(h * HEAD_DIM, (h + 1) * HEAD_DIM) for h in range(N_HEADS)]
    units = [(g, h) for g in range(ATTN_ROW_GROUP) for h in range(N_HEADS)]
    low_half = lax.broadcasted_iota(jnp.int32, (GRID_W, 2 * HEAD_DIM), 1) < HEAD_DIM

    def rows_body(jj, carry):
        q0, k0, dl, q = [], [], [], []
        for g in range(ATTN_ROW_GROUP):
            j = jj * ATTN_ROW_GROUP + g
            r = rb * rows_per_step + j
            rs = jnp.clip(r - kh // 2, 0, n_rows - kh)
            dl.append(rs - r + (WIN_H - 1))
            k0.append(pl.multiple_of(rs * GRID_W, GRID_W))
            q0.append(pl.multiple_of(j * GRID_W, GRID_W))
            q.append(q_ref[0, pl.ds(q0[g], GRID_W), :].astype(BF16))
        s_lat = [_bdot_nt(q[g][:, heads[h]], k_ref[0, h, pl.ds(k0[g], n_lat), :]) + bias_ref[0, h, dl[g]]
                 for g, h in units]
        q_all = jnp.concatenate(q, axis=0)
        s_ctx_all = [_bdot_nt(q_all[:, hs], ck_ref[0, 0, h]) for h, hs in enumerate(heads)]
        s_ctx = [s_ctx_all[h][g * GRID_W:(g + 1) * GRID_W] for g, h in units]
        m = [jnp.maximum(jnp.max(a, axis=-1, keepdims=True), jnp.max(c, axis=-1, keepdims=True))
             for a, c in zip(s_lat, s_ctx)]
        e_lat = [jnp.exp(a - mu).astype(BF16) for a, mu in zip(s_lat, m)]
        e_ctx = [jnp.exp(c - mu).astype(BF16) for c, mu in zip(s_ctx, m)]
        pv_ctx = [jnp.dot(jnp.concatenate([e_ctx[g * N_HEADS + h] for g in range(ATTN_ROW_GROUP)], axis=0),
                          cv_ref[0, 0, h], preferred_element_type=F32) for h in range(N_HEADS)]
        res = [jnp.dot(e_lat[u], v_ref[0, h, pl.ds(k0[g], n_lat), :], preferred_element_type=F32)
               + pv_ctx[h][g * GRID_W:(g + 1) * GRID_W]
               for u, (g, h) in enumerate(units)]
        for g in range(ATTN_ROW_GROUP):
            pairs = []
            for h in range(0, N_HEADS, 2):
                even, odd = res[g * N_HEADS + h], res[g * N_HEADS + h + 1]
                pairs.append(jnp.where(low_half, even / pltpu.roll(even, HEAD_DIM, axis=1),
                                       pltpu.roll(odd, HEAD_DIM, axis=1) / odd))
            o_ref[0, pl.ds(q0[g], GRID_W), :] = jnp.concatenate(pairs, axis=1)
        return carry

    lax.fori_loop(0, rows_per_step // ATTN_ROW_GROUP, rows_body, 0)


def _attn_nbr(p, k_hm, v_hm, cache_k, cache_v, bias, layer, *, rows_per_step=16):
    b, s, _ = p.shape
    tq = rows_per_step * GRID_W
    past = cache_v.shape[3]
    ctx_spec = lambda *dims: pl.BlockSpec((1, 1, N_HEADS) + dims, lambda i, j: (i, layer, 0, 0, 0))
    return pl.pallas_call(
        functools.partial(_attn_nbr_kernel, rows_per_step=rows_per_step),
        grid=(b, s // tq),
        in_specs=[pl.BlockSpec((1, tq, W_GRP), lambda i, j: (i, j, P16[G_QA])),
                  pl.BlockSpec((1, N_HEADS, s, HEAD_DIM), lambda i, j: (i, 0, 0, 0)),
                  pl.BlockSpec((1, N_HEADS, s, 2 * HEAD_DIM), lambda i, j: (i, 0, 0, 0)),
                  ctx_spec(past, HEAD_DIM), ctx_spec(past, 2 * HEAD_DIM),
                  pl.BlockSpec((1, N_HEADS, WIN_H, GRID_W, WIN_H * GRID_W), lambda i, j: (layer, 0, 0, 0, 0))],
        out_specs=pl.BlockSpec((1, tq, W_GRP), lambda i, j: (i, j, 0)),
        out_shape=jax.ShapeDtypeStruct((b, s, W_GRP), F32),
        compiler_params=_cparams("arbitrary", "arbitrary"),
        name="attn_nbr",
    )(p, k_hm, v_hm, cache_k, cache_v, bias)


def _fft_ctx_consts(s):
    c, sn = _dft_cos_sin(s)
    cc, sc = _dft_cos_sin(W_GRP // N_FOURIER)
    m1 = np.concatenate([c, -sn], axis=0)
    m2 = np.concatenate([_block_diag(cc, N_FOURIER), _block_diag(sc, N_FOURIER)], axis=0)
    return jnp.asarray(m1, F32), jnp.asarray(m2, F32)


def _fft_ctx_kernel(u_ref, m1_ref, m2_ref, o_ref, *, scale):
    s = u_ref.shape[1]
    t = _bdot(m1_ref[...], u_ref[0])
    x = jnp.concatenate([t[:s], t[s:]], axis=1)
    o_ref[0] = _bdot(x, m2_ref[...]) * scale


def _fft_ctx(p):
    b, s, _ = p.shape
    consts = _fft_ctx_consts(s)
    full = lambda a: pl.BlockSpec(a.shape, lambda i: (0,) * a.ndim)
    return pl.pallas_call(
        functools.partial(_fft_ctx_kernel, scale=1.0 / math.sqrt(s * (W_GRP // N_FOURIER))),
        grid=(b,),
        in_specs=[pl.BlockSpec((1, s, W_GRP), lambda i: (i, 0, P32[G_UC]))] + [full(a) for a in consts],
        out_specs=pl.BlockSpec((1, s, W_GRP), lambda i: (i, 0, 0)),
        out_shape=jax.ShapeDtypeStruct((b, s, W_GRP), F32),
        compiler_params=_cparams("arbitrary"),
        name="fft_ctx",
    )(p, *consts)


def _fft_grid_consts():
    n = GRID_W
    c, sn = _dft_cos_sin(n)
    cc, sc = _dft_cos_sin(W_GRP // N_FOURIER)
    m1 = np.concatenate([c, -sn], axis=0)
    m2 = np.block([[c, sn], [-sn, c]])
    reps = LANES // (W_GRP // N_FOURIER)
    m3 = np.concatenate([_block_diag(cc, reps), _block_diag(sc, reps)], axis=0)
    k = np.arange(n)
    ang = 2.0 * np.pi * (k[:, None] * k[None, :]) / (n * n)
    twr = np.broadcast_to(np.cos(ang)[:, :, None], (n, n, LANES))
    twi = np.broadcast_to(-np.sin(ang)[:, :, None], (n, n, LANES))
    return tuple(jnp.asarray(m, F32) for m in (m1, m2, m3, twr, twi))


FFT_UNROLL = 8
FFT_PITCH = 72


def _fft_grid_kernel(u_ref, m1_ref, m2_ref, m3_ref, twr_ref, twi_ref, o_ref, tr_scr, ti_scr, *, scale):
    n = GRID_W
    m1, m2, m3 = (r[...].astype(BF16) for r in (m1_ref, m2_ref, m3_ref))

    def stage1(b, carry):
        a_rows = u_ref[0, pl.ds(b, n, stride=n), :]
        t = _bdot(m1, a_rows)
        tr, ti = t[:n], t[n:]
        wr, wi = twr_ref[b], twi_ref[b]
        tr_scr[pl.ds(b, n, stride=FFT_PITCH), :] = tr * wr - ti * wi
        ti_scr[pl.ds(b, n, stride=FFT_PITCH), :] = tr * wi + ti * wr
        return carry

    lax.fori_loop(0, n, stage1, 0, unroll=FFT_UNROLL)

    def stage2(d, carry):
        r0 = pl.multiple_of(d * FFT_PITCH, 8)
        g = jnp.concatenate([tr_scr[pl.ds(r0, n), :], ti_scr[pl.ds(r0, n), :]], axis=0)
        x = _bdot(m2, g)
        xc = jnp.concatenate([x[:n], x[n:]], axis=1)
        o_ref[0, pl.ds(d, n, stride=n), :] = _bdot(xc, m3) * scale
        return carry

    lax.fori_loop(0, n, stage2, 0, unroll=FFT_UNROLL)


def _fft_grid(p):
    b, s, _ = p.shape
    assert s == GRID_W * GRID_W
    consts = _fft_grid_consts()
    full = lambda a: pl.BlockSpec(a.shape, lambda i, j: (0,) * a.ndim)
    blk0 = P32[G_UC] * W_GRP // LANES
    return pl.pallas_call(
        functools.partial(_fft_grid_kernel, scale=1.0 / math.sqrt(s * (W_GRP // N_FOURIER))),
        grid=(b, W_GRP // LANES),
        in_specs=[pl.BlockSpec((1, s, LANES), lambda i, j: (i, 0, blk0 + j))] + [full(a) for a in consts],
        out_specs=pl.BlockSpec((1, s, LANES), lambda i, j: (i, 0, j)),
        out_shape=jax.ShapeDtypeStruct((b, s, W_GRP), F32),
        scratch_shapes=[pltpu.VMEM((GRID_W * FFT_PITCH, LANES), F32)] * 2,
        compiler_params=_cparams("arbitrary", "arbitrary"),
        name="fft_grid",
    )(p, *consts)


GLA_GROUP = 4


def _gla_kernel(*refs, n_chunks, zero_init):
    if zero_init:
        (q_ref, k_ref, v_ref, a_ref, up_ref, gb_ref, gn_ref, seg_ref, _,
         o_ref, sfin_ref, qd_scr, st_scr, dec_scr) = refs
        s0_ref = None
    else:
        (q_ref, k_ref, v_ref, a_ref, up_ref, gb_ref, gn_ref, seg_ref, s0_ref,
         o_ref, qd_scr, st_scr, dec_scr) = refs
        sfin_ref = None
    c = CHUNK
    row = lax.broadcasted_iota(jnp.int32, (c, c), 0)
    col = lax.broadcasted_iota(jnp.int32, (c, c), 1)
    heads = [slice(h * HEAD_DIM, (h + 1) * HEAD_DIM) for h in range(N_HEADS)]
    wrow = lax.broadcasted_iota(jnp.int32, (W_GRP, W_GRP), 0)
    wcol = lax.broadcasted_iota(jnp.int32, (W_GRP, W_GRP), 1)
    head_diag = (wrow // HEAD_DIM) == (wcol // HEAD_DIM)

    def blockdiag(x):
        return jnp.where(head_diag, jnp.concatenate([x] * N_HEADS, axis=0), jnp.zeros((), x.dtype))

    trow = lax.broadcasted_iota(jnp.int32, (c, W_GRP), 0)
    tcol = lax.broadcasted_iota(jnp.int32, (c, W_GRP), 1) % c
    keep = [trow >= tcol, trow <= tcol]
    eye_tiled = jnp.where(trow == tcol, 1.0, 0.0).astype(BF16)

    def head_transpose_exact(x):
        return sum(_bdot_nt(eye_tiled, blockdiag(part)) for part in _split3(x))

    sum_ops = [jnp.where(row >= col, 1.0, 0.0).astype(BF16),
               jnp.where(row <= col, 1.0, 0.0).astype(BF16)]
    dirs = (0, 1)

    def chunk_rows(n):
        return pl.ds(pl.multiple_of(n * c, c), c)

    def phase1(i, carry):
        ns = [i * GLA_GROUP + g for g in range(GLA_GROUP)]
        rows = [chunk_rows(n) for n in ns]
        x_all = _bdot(jnp.concatenate([a_ref[0, r, :] for r in rows], axis=0), up_ref[0]) + gb_ref[0]
        x = [[x_all[g * c:(g + 1) * c, d * W_GRP:(d + 1) * W_GRP] for d in dirs] for g in range(GLA_GROUP)]
        la = [[(jnp.minimum(xd, 0.0) - jnp.log(1.0 + jnp.exp(-jnp.abs(xd)))) * (1.0 / GLA_TAU) for xd in xg]
              for xg in x]
        cum = [[sum(jnp.dot(sum_ops[d], part, preferred_element_type=F32) for part in _split3(lg[d]))
                for d in dirs] for lg in la]
        tot_row = [[cg[0][c - 1:c, :], cg[1][0:1, :]] for cg in cum]
        q = [q_ref[0, r, :] * (HEAD_DIM ** -0.5) for r in rows]
        k = [k_ref[0, r, :] for r in rows]
        v_bd = [blockdiag(v_ref[0, r, :].astype(BF16)) for r in rows]
        q_dec = [[(q[g] * jnp.exp(cum[g][d])).astype(BF16) for d in dirs] for g in range(GLA_GROUP)]
        k_inv = [[blockdiag((k[g] * jnp.exp(-cum[g][d])).astype(BF16)) for d in dirs] for g in range(GLA_GROUP)]
        k_end = [[blockdiag((k[g] * jnp.exp(tot_row[g][d] - cum[g][d])).astype(BF16)) for d in dirs]
                 for g in range(GLA_GROUP)]
        scores = [[_bdot_nt(q_dec[g][d], k_inv[g][d]) for d in dirs] for g in range(GLA_GROUP)]
        v_t = [_bdot_nt(eye_tiled, vb).astype(BF16) for vb in v_bd]
        upd = [[_bdot(v_t[g], k_end[g][d]) for d in dirs] for g in range(GLA_GROUP)]
        att = [[jnp.where(keep[d], scores[g][d], 0.0).astype(BF16) for d in dirs] for g in range(GLA_GROUP)]
        for g in range(GLA_GROUP):
            both = _bdot(jnp.concatenate(att[g], axis=0), v_bd[g])
            o_ref[0, rows[g], :] = both[:c] + both[c:]
            for d in dirs:
                qd_scr[d, rows[g], :] = q_dec[g][d]
                dec_scr[d, pl.ds(ns[g], 1), :] = jnp.exp(tot_row[g][d])
                st_scr[d, ns[g]] = upd[g][d]
        return carry

    lax.fori_loop(0, n_chunks // GLA_GROUP, phase1, 0)

    def init_state(d):
        if zero_init:
            return jnp.zeros((HEAD_DIM, W_GRP), F32)
        return head_transpose_exact(jnp.concatenate([s0_ref[0, 0, d, h] for h in range(N_HEADS)], axis=1))

    finals = []
    for d in range(2):
        def scan_body(i, st, d=d):
            n = i if d == 0 else n_chunks - 1 - i
            upd = st_scr[d, n]
            st_scr[d, n] = st
            return st * dec_scr[d, pl.ds(n, 1), :] + upd
        finals.append(lax.fori_loop(0, n_chunks, scan_body, init_state(d)))

    def phase3(i, carry):
        ns = [i * GLA_GROUP + g for g in range(GLA_GROUP)]
        rows = [chunk_rows(n) for n in ns]
        st_bd = [[blockdiag(st_scr[d, n].astype(BF16)) for d in dirs] for n in ns]
        inter = [[_bdot_nt(qd_scr[d, rows[g], :], st_bd[g][d]) for d in dirs] for g in range(GLA_GROUP)]
        o = jnp.concatenate([o_ref[0, rows[g], :] + inter[g][0] + inter[g][1] for g in range(GLA_GROUP)], axis=0)
        o = _head_rms(o, seg_ref[...], gn_ref[0])
        for g in range(GLA_GROUP):
            o_ref[0, rows[g], :] = o[g * c:(g + 1) * c]
        return carry

    lax.fori_loop(0, n_chunks // GLA_GROUP, phase3, 0)
    if sfin_ref is not None:
        for d in range(2):
            s_fin = head_transpose_exact(finals[d])
            for h, hs in enumerate(heads):
                sfin_ref[0, 0, d, h] = s_fin[:, hs]


def _gla(p32, p16, gate_up, gate_b, gn, seg, layer, state, *, zero_init):
    b, s, _ = p32.shape
    n_chunks = s // CHUNK
    col = lambda g: pl.BlockSpec((1, s, W_GRP), lambda i, g=g: (i, 0, g))
    lay = lambda shape: pl.BlockSpec((1,) + shape, lambda i: (layer,) + (0,) * len(shape))
    in_specs = [col(P32[G_QD]), col(P32[G_KD]), col(P16[G_VD]),
                pl.BlockSpec((1, s, LANES), lambda i: (i, 0, GATE_BLK)),
                lay((LANES, 2 * W_GRP)), lay((1, 2 * W_GRP)),
                lay((1, W_GRP)), pl.BlockSpec((W_GRP, W_GRP), lambda i: (0, 0))]
    args = [p32, p32, p16, p16, gate_up, gate_b, gn, seg, state]
    state_spec = pl.BlockSpec((1, 1, 2, N_HEADS, HEAD_DIM, HEAD_DIM), lambda i: (i, layer, 0, 0, 0, 0))
    out_specs = [pl.BlockSpec((1, s, W_GRP), lambda i: (i, 0, 0))]
    out_shape = [jax.ShapeDtypeStruct((b, s, W_GRP), F32)]
    if zero_init:
        in_specs.append(pl.BlockSpec(memory_space=pl.ANY))
        out_specs.append(state_spec)
        out_shape.append(jax.ShapeDtypeStruct(state.shape, state.dtype))
    else:
        in_specs.append(state_spec)
    return pl.pallas_call(
        functools.partial(_gla_kernel, n_chunks=n_chunks, zero_init=zero_init),
        grid=(b,),
        in_specs=in_specs,
        out_specs=out_specs,
        out_shape=out_shape,
        input_output_aliases={len(args) - 1: 1} if zero_init else {},
        scratch_shapes=[pltpu.VMEM((2, s, W_GRP), BF16),
                        pltpu.VMEM((2, n_chunks, HEAD_DIM, W_GRP), F32),
                        pltpu.VMEM((2, max(n_chunks, 8), W_GRP), F32)],
        compiler_params=_cparams("arbitrary"),
        name="gla_zero" if zero_init else "gla",
    )(*args)


def _outproj_kernel(x_ref, mod_ref, oa_ref, oc_ref, od_ref, ga_ref, bb_ref, cb_ref, hb_ref, gb_ref,
                    gc_ref, gd_ref, cprev_ref, hprev_ref, cnext_ref, hnext_ref, cw_ref, w_ref, o_ref, *, seq_len):
    tm = x_ref.shape[0]
    u = cb_ref[...] * hb_ref[...]
    rid = lax.broadcasted_iota(jnp.int32, u.shape, 0)
    pos = (pl.program_id(0) * tm + rid) % seq_len
    u_prev = jnp.where(rid == 0, cprev_ref[7:8, :] * hprev_ref[7:8, :], pltpu.roll(u, 1, axis=0))
    u_next = jnp.where(rid == tm - 1, cnext_ref[0:1, :] * hnext_ref[0:1, :], pltpu.roll(u, tm - 1, axis=0))
    u_prev = jnp.where(pos == 0, 0.0, u_prev)
    u_next = jnp.where(pos == seq_len - 1, 0.0, u_next)
    cw = cw_ref[0]
    conv = u_prev * cw[0:1, :] + u * cw[1:2, :] + u_next * cw[2:3, :]
    y_a = oa_ref[...] * _silu(ga_ref[...])
    y_b = bb_ref[...] * conv * _silu(gb_ref[...])
    y_c = oc_ref[...] * _silu(gc_ref[...])
    y_d = od_ref[...] * _silu(gd_ref[...])
    y = jnp.concatenate([y_a, y_b, y_c, y_d], axis=1).astype(BF16)
    gate = mod_ref[0, :, 2 * D_MODEL:3 * D_MODEL]
    o_ref[...] = x_ref[...] + gate * jnp.dot(y, w_ref[0], preferred_element_type=F32)


def _outproj(x, mod, layer, o_a, o_c, o_d, p, conv_w, w_out, *, tm):
    b, s, _ = x.shape
    n = b * s
    per_batch = mod.shape[0] > 1
    assert s % tm == 0 or (tm % s == 0 and not per_batch)
    tiles_per_seq = max(s // tm, 1)
    flat = lambda a: a.reshape(n, a.shape[-1])
    tile = lambda width: pl.BlockSpec((tm, width), lambda t: (t, 0))
    col = lambda g: pl.BlockSpec((tm, W_GRP), lambda t, g=g: (t, g))
    t8 = tm // 8
    prev = lambda g: pl.BlockSpec((8, W_GRP), lambda t, g=g: (jnp.maximum(t * t8 - 1, 0), g))
    nxt = lambda g: pl.BlockSpec((8, W_GRP), lambda t, g=g: (jnp.minimum((t + 1) * t8, n // 8 - 1), g))
    out = pl.pallas_call(
        functools.partial(_outproj_kernel, seq_len=s),
        grid=(n // tm,),
        in_specs=[tile(D_MODEL),
                  pl.BlockSpec((1, 1, 3 * D_MODEL),
                               (lambda t: (t // tiles_per_seq, 0, 0)) if per_batch else (lambda t: (0, 0, 0))),
                  tile(W_GRP), tile(W_GRP), tile(W_GRP),
                  *[col(P32[g]) for g in (G_GA, G_BB, G_CB, G_HB, G_GB, G_GC, G_GD)],
                  prev(P32[G_CB]), prev(P32[G_HB]), nxt(P32[G_CB]), nxt(P32[G_HB]),
                  pl.BlockSpec((1, 3, W_GRP), lambda t: (layer, 0, 0)),
                  pl.BlockSpec((1, D_MODEL, D_MODEL), lambda t: (layer, 0, 0))],
        out_specs=tile(D_MODEL),
        out_shape=jax.ShapeDtypeStruct((n, D_MODEL), F32),
        compiler_params=_cparams("arbitrary"),
        name="outproj",
    )(flat(x), mod, flat(o_a), flat(o_c), flat(o_d), *([flat(p)] * 11), conv_w, w_out)
    return out.reshape(b, s, D_MODEL)


def kernel(x_prompt, x_sample, cache_k, cache_v, state_gla, c, c_ctx, norm_g, w_mod, b_mod, w_in, q_norm_g,
           k_norm_g, rpb, conv_w, gla_up_f, gla_bias_f, gla_up_b, gla_bias_b, gla_norm_g, w_out):
    dec_batch = c.shape[0]
    pad_rows = 8 - (1 + dec_batch) % 8
    cvec = jnp.concatenate([c_ctx[None], c, jnp.zeros((pad_rows, D_MODEL), F32)], axis=0)
    mod = _modulation(cvec, w_mod, b_mod)
    mod_ctx = mod[:, 0:1].reshape(DEPTH, 1, 1, 3 * D_MODEL)
    mod_smp = mod[:, 1:1 + dec_batch].reshape(DEPTH, dec_batch, 1, 3 * D_MODEL)

    w_main, w_gate = _cast_w_in(w_in)
    w_out_b = w_out.astype(BF16)
    seg = _seg_ones()
    norm_g3 = norm_g.reshape(DEPTH, 1, D_MODEL)
    qn = jnp.tile(q_norm_g, (1, N_HEADS)).reshape(DEPTH, 1, W_GRP)
    kn = jnp.tile(k_norm_g, (1, N_HEADS)).reshape(DEPTH, 1, W_GRP)
    gn = jnp.tile(gla_norm_g, (1, N_HEADS)).reshape(DEPTH, 1, W_GRP)
    gate_up = jnp.zeros((DEPTH, LANES, 2 * W_GRP), F32)
    gate_up = gate_up.at[:, :GATE_RANK, :W_GRP].set(gla_up_f).at[:, GATE_RANK:2 * GATE_RANK, W_GRP:].set(gla_up_b)
    gate_b = jnp.concatenate([gla_bias_f, gla_bias_b], axis=-1).reshape(DEPTH, 1, 2 * W_GRP)
    bias = _bias_tables(rpb)
    cache_k_b = cache_k.astype(BF16)
    cache_v_b = jnp.concatenate([cache_v.astype(BF16), jnp.ones(cache_v.shape, BF16)], axis=-1)

    x = x_prompt
    batch, seq, _ = x.shape
    new_cache_k = jnp.zeros((batch, DEPTH, N_HEADS, seq, HEAD_DIM), F32)
    new_cache_v = jnp.zeros((batch, DEPTH, N_HEADS, seq, HEAD_DIM), F32)
    new_state = jnp.zeros((batch, DEPTH, 2, N_HEADS, HEAD_DIM, HEAD_DIM), F32)
    for l in range(DEPTH):
        p32, p16, new_cache_k, new_cache_v = _inproj(x, mod_ctx[l], l, norm_g3, w_main, w_gate, qn, kn, seg,
                                                     tm=TOKEN_TILE, caches=(new_cache_k, new_cache_v))
        o_a = _attn_ctx(p16)
        o_c = _fft_ctx(p32)
        o_d, new_state = _gla(p32, p16, gate_up, gate_b, gn, seg, l, new_state, zero_init=True)
        x = _outproj(x, mod_ctx[l], l, o_a, o_c, o_d, p32, conv_w, w_out_b, tm=TOKEN_TILE)
    y_prompt = x

    xs = x_sample
    for l in range(DEPTH):
        p32, p16, k_hm, v_hm = _inproj(xs, mod_smp[l], l, norm_g3, w_main, w_gate, qn, kn, seg, tm=TOKEN_TILE)
        o_a = _attn_nbr(p16, k_hm, v_hm, cache_k_b, cache_v_b, bias, l)
        o_c = _fft_grid(p32)
        (o_d,) = _gla(p32, p16, gate_up, gate_b, gn, seg, l, state_gla, zero_init=False)
        xs = _outproj(xs, mod_smp[l], l, o_a, o_c, o_d, p32, conv_w, w_out_b, tm=TOKEN_TILE)
    return (y_prompt, xs, new_cache_k, new_cache_v, new_state)
```

```python
import functools
import math

import numpy as np
import jax
import jax.numpy as jnp
from jax import lax
from jax.experimental import pallas as pl
from jax.experimental.pallas import tpu as pltpu

F32 = jnp.float32
BF16 = jnp.bfloat16

D_MODEL = 1024
DEPTH = 4
GRID_W = 64
W_GRP = 256
HEAD_DIM = 64
N_HEADS = 4
WIN_H = 8
WIN_W = 16
N_FOURIER = 4
GATE_RANK = 16
GLA_TAU = 16.0
CHUNK = 64
RMS_EPS = 1e-6
N_IN = 14 * W_GRP + 2 * GATE_RANK
LANES = 128
N_IN_PAD = -(-N_IN // LANES) * LANES
G_QA, G_KA, G_VA, G_GA, G_BB, G_CB, G_HB, G_GB, G_UC, G_GC, G_QD, G_KD, G_VD, G_GD = range(14)
F32_GROUPS = (G_UC, G_QD, G_KD)
BF16_GROUPS = (G_QA, G_KA, G_VA, G_VD, G_GA, G_BB, G_CB, G_HB, G_GB, G_GC, G_GD)
P32 = {g: i for i, g in enumerate(F32_GROUPS)}
P16 = {g: i for i, g in enumerate(BF16_GROUPS)}
N_P32 = len(F32_GROUPS) * W_GRP
N_P16 = len(BF16_GROUPS) * W_GRP + LANES
GATE_BLK = len(BF16_GROUPS) * W_GRP // LANES
NEG_BIG = -1e30
VMEM_LIMIT = 56 * 1024 * 1024
TOKEN_TILE = 512


def _cparams(*sem):
    return pltpu.CompilerParams(dimension_semantics=sem, vmem_limit_bytes=VMEM_LIMIT)


def _bdot(a, b):
    return jnp.dot(a.astype(BF16), b.astype(BF16), preferred_element_type=F32)


def _bdot_nt(a, b):
    return lax.dot_general(a.astype(BF16), b.astype(BF16), (((1,), (1,)), ((), ())),
                           preferred_element_type=F32)


def _split2(x):
    hi = x.astype(BF16)
    lo = (x - hi.astype(F32)).astype(BF16)
    return hi, lo


def _split3(x):
    h1 = x.astype(BF16)
    r = x - h1.astype(F32)
    h2 = r.astype(BF16)
    h3 = (r - h2.astype(F32)).astype(BF16)
    return h1, h2, h3


def _silu(x):
    return x / (1.0 + jnp.exp(-x))


def _head_rms(t, seg, g):
    n = t.shape[0]
    both = jnp.dot(jnp.concatenate(_split2(t * t), axis=0), seg, preferred_element_type=F32)
    ss = both[:n] + both[n:]
    return t * lax.rsqrt(ss * (1.0 / HEAD_DIM) + RMS_EPS) * g


def _dft_cos_sin(n):
    k = np.arange(n)
    ang = 2.0 * np.pi * ((k[:, None] * k[None, :]) % n) / n
    return np.cos(ang), np.sin(ang)


def _block_diag(m, reps):
    n = m.shape[0]
    out = np.zeros((n * reps, n * reps), m.dtype)
    for i in range(reps):
        out[i * n:(i + 1) * n, i * n:(i + 1) * n] = m
    return out


def _seg_ones():
    return jnp.asarray(_block_diag(np.ones((HEAD_DIM, HEAD_DIM)), N_HEADS), BF16)


def _mod_kernel(c_ref, w_ref, b_ref, o_ref):
    o_ref[0] = _bdot(_silu(c_ref[...]), w_ref[0]) + b_ref[0]


def _modulation(cvec, w_mod, b_mod):
    rows = cvec.shape[0]
    return pl.pallas_call(
        _mod_kernel,
        grid=(DEPTH, 3),
        in_specs=[pl.BlockSpec((rows, D_MODEL), lambda l, j: (0, 0)),
                  pl.BlockSpec((1, D_MODEL, D_MODEL), lambda l, j: (l, 0, j)),
                  pl.BlockSpec((1, 1, D_MODEL), lambda l, j: (l, 0, j))],
        out_specs=pl.BlockSpec((1, rows, D_MODEL), lambda l, j: (l, 0, j)),
        out_shape=jax.ShapeDtypeStruct((DEPTH, rows, 3 * D_MODEL), F32),
        compiler_params=_cparams("arbitrary", "arbitrary"),
        name="adaln_mod",
    )(cvec, w_mod, b_mod.reshape(DEPTH, 1, 3 * D_MODEL))


INPROJ_ORDER = (2, 0, 3, 1, 4, 5, 6, 7, 8, 9, 10, 11, 12, 13)
INPROJ_PIECES = 8


def _inproj_kernel(x_ref, mod_ref, g_ref, w_ref, wg_ref, qn_ref, kn_ref, seg_ref, *rest,
                   for_window_attn, n_seq):
    p32_ref, p16_ref, k_ref, v_ref, h_even, h_odd = rest[-6:]
    t = pl.program_id(0)
    rows = p32_ref.shape[0] // n_seq

    def normed(x_ref, mod_ref, rows=slice(None)):
        x = x_ref[rows, :]
        ms = jnp.mean(x * x, axis=-1, keepdims=True)
        y = x * lax.rsqrt(ms + RMS_EPS) * g_ref[0]
        shift = mod_ref[0, :, 0:D_MODEL]
        scale = mod_ref[0, :, D_MODEL:2 * D_MODEL]
        return (y * (1.0 + scale) + shift).astype(BF16)

    def store_heads(ref, val, ones):
        for q in range(n_seq):
            for h in range(N_HEADS):
                th = val[q * rows:(q + 1) * rows, h * HEAD_DIM:(h + 1) * HEAD_DIM]
                if ones:
                    th = jnp.concatenate([th, jnp.ones_like(th)], axis=1)
                if for_window_attn:
                    ref[0, h] = th.astype(ref.dtype)
                else:
                    ref[q, 0, h] = th

    def step(h_cur, h_nxt):
        seg = seg_ref[...]
        piece = p32_ref.shape[0] // INPROJ_PIECES
        first_late = len(INPROJ_ORDER) - INPROJ_PIECES
        for idx, j in enumerate(INPROJ_ORDER):
            if idx >= first_late:
                part = slice((idx - first_late) * piece, (idx - first_late + 1) * piece)
                h_nxt[part, :] = normed(x_ref, mod_ref, part)
            cs = slice(j * W_GRP, (j + 1) * W_GRP)
            y = jnp.dot(h_cur[...], w_ref[0, :, cs], preferred_element_type=F32)
            if j == G_QA:
                y = _head_rms(y, seg, qn_ref[0]) * (HEAD_DIM ** -0.5)
            elif j == G_KA:
                y = _head_rms(y, seg, kn_ref[0])
            if j in P32:
                p32_ref[:, P32[j] * W_GRP:(P32[j] + 1) * W_GRP] = y
            else:
                p16_ref[:, P16[j] * W_GRP:(P16[j] + 1) * W_GRP] = y.astype(BF16)
            if j == G_KA:
                store_heads(k_ref, y, False)
            elif j == G_VA:
                store_heads(v_ref, y, for_window_attn)
        p16_ref[:, GATE_BLK * LANES:N_P16] = jnp.dot(h_cur[...], wg_ref[0], preferred_element_type=F32).astype(BF16)

    @pl.when(t == 0)
    def _():
        h_even[...] = normed(x_ref, mod_ref)
        for ref in (p32_ref, p16_ref, k_ref, v_ref):
            ref[...] = jnp.zeros(ref.shape, ref.dtype)

    @pl.when(t % 2 == 1)
    def _():
        step(h_even, h_odd)

    @pl.when(jnp.logical_and(t > 0, t % 2 == 0))
    def _():
        step(h_odd, h_even)


def _inproj(x, mod, layer, norm_g, w_in, qn, kn, seg, *, tm, caches=None):
    b, s, _ = x.shape
    for_window_attn = caches is None
    assert s % tm == 0 if for_window_attn else tm % s == 0
    n_tiles = b * s // tm
    tiles_per_seq = max(s // tm, 1)
    n_seq = max(tm // s, 1)
    per_batch = mod.shape[0] > 1
    prep = lambda t: jnp.minimum(t, n_tiles - 1)
    done = lambda t: jnp.maximum(t - 1, 0)
    lay = lambda *dims: pl.BlockSpec((1,) + dims, lambda t: (layer,) + (0,) * len(dims))
    in_specs = [pl.BlockSpec((tm, D_MODEL), lambda t: (prep(t), 0)),
                pl.BlockSpec((1, 1, 3 * D_MODEL),
                             (lambda t: (prep(t) // tiles_per_seq, 0, 0)) if per_batch else (lambda t: (0, 0, 0))),
                lay(1, D_MODEL),
                pl.BlockSpec((1, D_MODEL, 14 * W_GRP), lambda t: (layer, 0, 0)),
                pl.BlockSpec((1, D_MODEL, LANES), lambda t: (layer, 0, 14 * W_GRP // LANES)),
                lay(1, W_GRP), lay(1, W_GRP),
                pl.BlockSpec((W_GRP, W_GRP), lambda t: (0, 0))]
    x2 = x.reshape(b * s, D_MODEL)
    args = [x2, mod, norm_g, w_in, w_in, qn, kn, seg]
    out_shape = [jax.ShapeDtypeStruct((b * s, N_P32), F32), jax.ShapeDtypeStruct((b * s, N_P16), BF16)]
    out_specs = [pl.BlockSpec((tm, N_P32), lambda t: (done(t), 0)), pl.BlockSpec((tm, N_P16), lambda t: (done(t), 0))]
    aliases = {}
    if for_window_attn:
        for width in (HEAD_DIM, 2 * HEAD_DIM):
            out_shape.append(jax.ShapeDtypeStruct((b, N_HEADS, s, width), BF16))
            out_specs.append(pl.BlockSpec((1, N_HEADS, tm, width),
                                          lambda t: (done(t) // tiles_per_seq, 0, done(t) % tiles_per_seq, 0)))
    else:
        for i, cache in enumerate(caches):
            aliases[len(args)] = 2 + i
            args.append(cache)
            in_specs.append(pl.BlockSpec(memory_space=pl.ANY))
            out_shape.append(jax.ShapeDtypeStruct(cache.shape, cache.dtype))
            out_specs.append(pl.BlockSpec((n_seq, 1, N_HEADS, s, HEAD_DIM), lambda t: (done(t), layer, 0, 0, 0)))
    p32, p16, k, v = pl.pallas_call(
        functools.partial(_inproj_kernel, for_window_attn=for_window_attn, n_seq=n_seq),
        grid=(n_tiles + 1,),
        in_specs=in_specs,
        out_specs=out_specs,
        out_shape=out_shape,
        input_output_aliases=aliases,
        scratch_shapes=[pltpu.VMEM((tm, D_MODEL), BF16)] * 2,
        compiler_params=_cparams("arbitrary"),
        name="inproj_grid" if for_window_attn else "inproj_ctx",
    )(*args)
    return p32.reshape(b, s, N_P32), p16.reshape(b, s, N_P16), k, v


def _attn_ctx_kernel(q_ref, k_ref, v_ref, o_ref):
    for h in range(N_HEADS):
        hs = slice(h * HEAD_DIM, (h + 1) * HEAD_DIM)
        s = _bdot_nt(q_ref[0, :, hs], k_ref[0, :, hs])
        e = jnp.exp(s - jnp.max(s, axis=-1, keepdims=True))
        l = jnp.sum(e, axis=-1, keepdims=True)
        o_ref[0, :, hs] = _bdot(e, v_ref[0, :, hs]) / l


def _attn_ctx(p):
    b, s, _ = p.shape
    col = lambda g: pl.BlockSpec((1, s, W_GRP), lambda i, g=g: (i, 0, g))
    return pl.pallas_call(
        _attn_ctx_kernel,
        grid=(b,),
        in_specs=[col(P16[G_QA]), col(P16[G_KA]), col(P16[G_VA])],
        out_specs=pl.BlockSpec((1, s, W_GRP), lambda i: (i, 0, 0)),
        out_shape=jax.ShapeDtypeStruct((b, s, W_GRP), F32),
        compiler_params=_cparams("arbitrary"),
        name="attn_ctx",
    )(p, p, p)


def _bias_kernel(rpb_ref, o_ref):
    lh = pl.program_id(0)
    q = lax.broadcasted_iota(jnp.int32, (GRID_W, GRID_W), 0)
    kc = lax.broadcasted_iota(jnp.int32, (GRID_W, GRID_W), 1)
    dc = jnp.clip(kc - q + (WIN_W - 1), 0, 2 * WIN_W - 2)
    cs = jnp.clip(q - WIN_W // 2, 0, GRID_W - WIN_W)
    in_win = jnp.where(kc >= cs, jnp.where(kc < cs + WIN_W, 1, 0), 0) == 1
    n_dr, n_dc = 2 * WIN_H - 1, 2 * WIN_W - 1
    tiles = []
    for dr in range(n_dr):
        base = (lh * n_dr + dr) * n_dc
        t = jnp.zeros((GRID_W, GRID_W), F32)
        for d in range(n_dc):
            t = jnp.where(dc == d, rpb_ref[base + d], t)
        tiles.append(jnp.where(in_win, t, NEG_BIG))
    for dl in range(WIN_H):
        for i in range(WIN_H):
            o_ref[0, dl, :, i * GRID_W:(i + 1) * GRID_W] = tiles[dl + i]


def _bias_tables(rpb):
    out = pl.pallas_call(
        _bias_kernel,
        grid=(DEPTH * N_HEADS,),
        in_specs=[pl.BlockSpec(memory_space=pltpu.SMEM)],
        out_specs=pl.BlockSpec((1, WIN_H, GRID_W, WIN_H * GRID_W), lambda i: (i, 0, 0, 0)),
        out_shape=jax.ShapeDtypeStruct((DEPTH * N_HEADS, WIN_H, GRID_W, WIN_H * GRID_W), F32),
        compiler_params=_cparams("arbitrary"),
        name="rpb_tables",
    )(rpb.reshape(-1))
    return out.reshape(DEPTH, N_HEADS, WIN_H, GRID_W, WIN_H * GRID_W)


ATTN_ROW_GROUP = 4


def _attn_nbr_kernel(q_ref, k_ref, v_ref, ck_ref, cv_ref, bias_ref, o_ref, *, rows_per_step):
    rb = pl.program_id(1)
    n_rows = k_ref.shape[2] // GRID_W
    kh = min(WIN_H, n_rows)
    n_lat = kh * GRID_W
    heads = [slice(h * HEAD_DIM, (h + 1) * HEAD_DIM) for h in range(N_HEADS)]
    units = [(g, h) for g in range(ATTN_ROW_GROUP) for h in range(N_HEADS)]
    low_half = lax.broadcasted_iota(jnp.int32, (GRID_W, 2 * HEAD_DIM), 1) < HEAD_DIM

    def rows_body(jj, carry):
        q0, k0, dl, q = [], [], [], []
        for g in range(ATTN_ROW_GROUP):
            j = jj * ATTN_ROW_GROUP + g
            r = rb * rows_per_step + j
            rs = jnp.clip(r - kh // 2, 0, n_rows - kh)
            dl.append(rs - r + (WIN_H - 1))
            k0.append(pl.multiple_of(rs * GRID_W, GRID_W))
            q0.append(pl.multiple_of(j * GRID_W, GRID_W))
            q.append(q_ref[0, pl.ds(q0[g], GRID_W), :].astype(BF16))
        s_lat = [_bdot_nt(q[g][:, heads[h]], k_ref[0, h, pl.ds(k0[g], n_lat), :]) + bias_ref[0, h, dl[g]]
                 for g, h in units]
        q_all = jnp.concatenate(q, axis=0)
        s_ctx_all = [_bdot_nt(q_all[:, hs], ck_ref[0, 0, h]) for h, hs in enumerate(heads)]
        s_ctx = [s_ctx_all[h][g * GRID_W:(g + 1) * GRID_W] for g, h in units]
        m = [jnp.maximum(jnp.max(a, axis=-1, keepdims=True), jnp.max(c, axis=-1, keepdims=True))
             for a, c in zip(s_lat, s_ctx)]
        e_lat = [jnp.exp(a - mu).astype(BF16) for a, mu in zip(s_lat, m)]
        e_ctx = [jnp.exp(c - mu).astype(BF16) for c, mu in zip(s_ctx, m)]
        pv_ctx = [jnp.dot(jnp.concatenate([e_ctx[g * N_HEADS + h] for g in range(ATTN_ROW_GROUP)], axis=0),
                          cv_ref[0, 0, h], preferred_element_type=F32) for h in range(N_HEADS)]
        res = [jnp.dot(e_lat[u], v_ref[0, h, pl.ds(k0[g], n_lat), :], preferred_element_type=F32)
               + pv_ctx[h][g * GRID_W:(g + 1) * GRID_W]
               for u, (g, h) in enumerate(units)]
        for g in range(ATTN_ROW_GROUP):
            pairs = []
            for h in range(0, N_HEADS, 2):
                even, odd = res[g * N_HEADS + h], res[g * N_HEADS + h + 1]
                pairs.append(jnp.where(low_half, even / pltpu.roll(even, HEAD_DIM, axis=1),
                                       pltpu.roll(odd, HEAD_DIM, axis=1) / odd))
            o_ref[0, pl.ds(q0[g], GRID_W), :] = jnp.concatenate(pairs, axis=1)
        return carry

    lax.fori_loop(0, rows_per_step // ATTN_ROW_GROUP, rows_body, 0)


def _attn_nbr(p, k_hm, v_hm, cache_k, cache_v, bias, layer, *, rows_per_step=16):
    b, s, _ = p.shape
    tq = rows_per_step * GRID_W
    past = cache_v.shape[3]
    ctx_spec = lambda *dims: pl.BlockSpec((1, 1, N_HEADS) + dims, lambda i, j: (i, layer, 0, 0, 0))
    return pl.pallas_call(
        functools.partial(_attn_nbr_kernel, rows_per_step=rows_per_step),
        grid=(b, s // tq),
        in_specs=[pl.BlockSpec((1, tq, W_GRP), lambda i, j: (i, j, P16[G_QA])),
                  pl.BlockSpec((1, N_HEADS, s, HEAD_DIM), lambda i, j: (i, 0, 0, 0)),
                  pl.BlockSpec((1, N_HEADS, s, 2 * HEAD_DIM), lambda i, j: (i, 0, 0, 0)),
                  ctx_spec(past, HEAD_DIM), ctx_spec(past, 2 * HEAD_DIM),
                  pl.BlockSpec((1, N_HEADS, WIN_H, GRID_W, WIN_H * GRID_W), lambda i, j: (layer, 0, 0, 0, 0))],
        out_specs=pl.BlockSpec((1, tq, W_GRP), lambda i, j: (i, j, 0)),
        out_shape=jax.ShapeDtypeStruct((b, s, W_GRP), F32),
        compiler_params=_cparams("arbitrary", "arbitrary"),
        name="attn_nbr",
    )(p, k_hm, v_hm, cache_k, cache_v, bias)


def _fft_ctx_consts(s):
    c, sn = _dft_cos_sin(s)
    cc, sc = _dft_cos_sin(W_GRP // N_FOURIER)
    m1 = np.concatenate([c, -sn], axis=0)
    m2 = np.concatenate([_block_diag(cc, N_FOURIER), _block_diag(sc, N_FOURIER)], axis=0)
    return jnp.asarray(m1, F32), jnp.asarray(m2, F32)


def _fft_ctx_kernel(u_ref, m1_ref, m2_ref, o_ref, *, scale):
    s = u_ref.shape[1]
    t = _bdot(m1_ref[...], u_ref[0])
    x = jnp.concatenate([t[:s], t[s:]], axis=1)
    o_ref[0] = _bdot(x, m2_ref[...]) * scale


def _fft_ctx(p):
    b, s, _ = p.shape
    consts = _fft_ctx_consts(s)
    full = lambda a: pl.BlockSpec(a.shape, lambda i: (0,) * a.ndim)
    return pl.pallas_call(
        functools.partial(_fft_ctx_kernel, scale=1.0 / math.sqrt(s * (W_GRP // N_FOURIER))),
        grid=(b,),
        in_specs=[pl.BlockSpec((1, s, W_GRP), lambda i: (i, 0, P32[G_UC]))] + [full(a) for a in consts],
        out_specs=pl.BlockSpec((1, s, W_GRP), lambda i: (i, 0, 0)),
        out_shape=jax.ShapeDtypeStruct((b, s, W_GRP), F32),
        compiler_params=_cparams("arbitrary"),
        name="fft_ctx",
    )(p, *consts)


def _fft_grid_consts():
    n = GRID_W
    c, sn = _dft_cos_sin(n)
    cc, sc = _dft_cos_sin(W_GRP // N_FOURIER)
    m1 = np.concatenate([c, -sn], axis=0)
    m2 = np.block([[c, sn], [-sn, c]])
    reps = LANES // (W_GRP // N_FOURIER)
    m3 = np.concatenate([_block_diag(cc, reps), _block_diag(sc, reps)], axis=0)
    k = np.arange(n)
    ang = 2.0 * np.pi * (k[:, None] * k[None, :]) / (n * n)
    twr = np.broadcast_to(np.cos(ang)[:, :, None], (n, n, LANES))
    twi = np.broadcast_to(-np.sin(ang)[:, :, None], (n, n, LANES))
    return tuple(jnp.asarray(m, F32) for m in (m1, m2, m3, twr, twi))


FFT_UNROLL = 8
FFT_PITCH = 72


def _fft_grid_kernel(u_ref, m1_ref, m2_ref, m3_ref, twr_ref, twi_ref, o_ref, tr_scr, ti_scr, *, scale):
    n = GRID_W
    m1, m2, m3 = (r[...].astype(BF16) for r in (m1_ref, m2_ref, m3_ref))

    def stage1(b, carry):
        a_rows = u_ref[0, pl.ds(b, n, stride=n), :]
        t = _bdot(m1, a_rows)
        tr, ti = t[:n], t[n:]
        wr, wi = twr_ref[b], twi_ref[b]
        tr_scr[pl.ds(b, n, stride=FFT_PITCH), :] = tr * wr - ti * wi
        ti_scr[pl.ds(b, n, stride=FFT_PITCH), :] = tr * wi + ti * wr
        return carry

    lax.fori_loop(0, n, stage1, 0, unroll=FFT_UNROLL)

    def stage2(dd, carry):
        xcs = []
        for u in range(FFT_UNROLL):
            r0 = pl.multiple_of((dd * FFT_UNROLL + u) * FFT_PITCH, 8)
            g = jnp.concatenate([tr_scr[pl.ds(r0, n), :], ti_scr[pl.ds(r0, n), :]], axis=0)
            x = _bdot(m2, g)
            xcs.append(jnp.concatenate([x[:n], x[n:]], axis=1))
        out = _bdot(jnp.concatenate(xcs, axis=0), m3) * scale
        for u in range(FFT_UNROLL):
            o_ref[0, pl.ds(dd * FFT_UNROLL + u, n, stride=n), :] = out[u * n:(u + 1) * n]
        return carry

    lax.fori_loop(0, n // FFT_UNROLL, stage2, 0)


def _fft_grid(p):
    b, s, _ = p.shape
    assert s == GRID_W * GRID_W
    consts = _fft_grid_consts()
    full = lambda a: pl.BlockSpec(a.shape, lambda i, j: (0,) * a.ndim)
    blk0 = P32[G_UC] * W_GRP // LANES
    return pl.pallas_call(
        functools.partial(_fft_grid_kernel, scale=1.0 / math.sqrt(s * (W_GRP // N_FOURIER))),
        grid=(b, W_GRP // LANES),
        in_specs=[pl.BlockSpec((1, s, LANES), lambda i, j: (i, 0, blk0 + j))] + [full(a) for a in consts],
        out_specs=pl.BlockSpec((1, s, LANES), lambda i, j: (i, 0, j)),
        out_shape=jax.ShapeDtypeStruct((b, s, W_GRP), F32),
        scratch_shapes=[pltpu.VMEM((GRID_W * FFT_PITCH, LANES), F32)] * 2,
        compiler_params=_cparams("arbitrary", "arbitrary"),
        name="fft_grid",
    )(p, *consts)


GLA_GROUP = 4


def _gla_kernel(*refs, n_chunks, zero_init):
    if zero_init:
        (q_ref, k_ref, v_ref, a_ref, up_ref, gb_ref, gn_ref, seg_ref, _,
         o_ref, sfin_ref, qd_scr, st_scr, dec_scr) = refs
        s0_ref = None
    else:
        (q_ref, k_ref, v_ref, a_ref, up_ref, gb_ref, gn_ref, seg_ref, s0_ref,
         o_ref, qd_scr, st_scr, dec_scr) = refs
        sfin_ref = None
    c = CHUNK
    row = lax.broadcasted_iota(jnp.int32, (c, c), 0)
    col = lax.broadcasted_iota(jnp.int32, (c, c), 1)
    heads = [slice(h * HEAD_DIM, (h + 1) * HEAD_DIM) for h in range(N_HEADS)]
    wrow = lax.broadcasted_iota(jnp.int32, (W_GRP, W_GRP), 0)
    wcol = lax.broadcasted_iota(jnp.int32, (W_GRP, W_GRP), 1)
    head_diag = (wrow // HEAD_DIM) == (wcol // HEAD_DIM)

    def blockdiag(x):
        return jnp.where(head_diag, jnp.concatenate([x] * N_HEADS, axis=0), jnp.zeros((), x.dtype))

    trow = lax.broadcasted_iota(jnp.int32, (c, W_GRP), 0)
    tcol = lax.broadcasted_iota(jnp.int32, (c, W_GRP), 1) % c
    keep = [trow >= tcol, trow <= tcol]
    eye_tiled = jnp.where(trow == tcol, 1.0, 0.0).astype(BF16)

    def head_transpose_exact(x):
        return sum(_bdot_nt(eye_tiled, blockdiag(part)) for part in _split3(x))

    sum_ops = [jnp.where(row >= col, 1.0, 0.0).astype(BF16),
               jnp.where(row <= col, 1.0, 0.0).astype(BF16)]
    dirs = (0, 1)

    def chunk_rows(n):
        return pl.ds(pl.multiple_of(n * c, c), c)

    def phase1(i, carry):
        ns = [i * GLA_GROUP + g for g in range(GLA_GROUP)]
        rows = [chunk_rows(n) for n in ns]
        x_all = _bdot(jnp.concatenate([a_ref[0, r, :] for r in rows], axis=0), up_ref[0]) + gb_ref[0]
        x = [[x_all[g * c:(g + 1) * c, d * W_GRP:(d + 1) * W_GRP] for d in dirs] for g in range(GLA_GROUP)]
        la = [[(jnp.minimum(xd, 0.0) - jnp.log(1.0 + jnp.exp(-jnp.abs(xd)))) * (1.0 / GLA_TAU) for xd in xg]
              for xg in x]
        cum = [[sum(jnp.dot(sum_ops[d], part, preferred_element_type=F32) for part in _split3(lg[d]))
                for d in dirs] for lg in la]
        tot_row = [[cg[0][c - 1:c, :], cg[1][0:1, :]] for cg in cum]
        q = [q_ref[0, r, :] * (HEAD_DIM ** -0.5) for r in rows]
        k = [k_ref[0, r, :] for r in rows]
        v_bd = [blockdiag(v_ref[0, r, :].astype(BF16)) for r in rows]
        q_dec = [[(q[g] * jnp.exp(cum[g][d])).astype(BF16) for d in dirs] for g in range(GLA_GROUP)]
        k_inv = [[blockdiag((k[g] * jnp.exp(-cum[g][d])).astype(BF16)) for d in dirs] for g in range(GLA_GROUP)]
        k_end = [[blockdiag((k[g] * jnp.exp(tot_row[g][d] - cum[g][d])).astype(BF16)) for d in dirs]
                 for g in range(GLA_GROUP)]
        scores = [[_bdot_nt(q_dec[g][d], k_inv[g][d]) for d in dirs] for g in range(GLA_GROUP)]
        v_t = [_bdot_nt(eye_tiled, vb).astype(BF16) for vb in v_bd]
        upd = [[_bdot(v_t[g], k_end[g][d]) for d in dirs] for g in range(GLA_GROUP)]
        att = [[jnp.where(keep[d], scores[g][d], 0.0).astype(BF16) for d in dirs] for g in range(GLA_GROUP)]
        for g in range(GLA_GROUP):
            both = _bdot(jnp.concatenate(att[g], axis=0), v_bd[g])
            o_ref[0, rows[g], :] = both[:c] + both[c:]
            for d in dirs:
                qd_scr[d, rows[g], :] = q_dec[g][d]
                dec_scr[d, pl.ds(ns[g], 1), :] = jnp.exp(tot_row[g][d])
                st_scr[d, ns[g]] = upd[g][d]
        return carry

    lax.fori_loop(0, n_chunks // GLA_GROUP, phase1, 0)

    def init_state(d):
        if zero_init:
            return jnp.zeros((HEAD_DIM, W_GRP), F32)
        return head_transpose_exact(jnp.concatenate([s0_ref[0, 0, d, h] for h in range(N_HEADS)], axis=1))

    finals = []
    for d in range(2):
        def scan_body(i, st, d=d):
            n = i if d == 0 else n_chunks - 1 - i
            upd = st_scr[d, n]
            st_scr[d, n] = st
            return st * dec_scr[d, pl.ds(n, 1), :] + upd
        finals.append(lax.fori_loop(0, n_chunks, scan_body, init_state(d)))

    def phase3(i, carry):
        ns = [i * GLA_GROUP + g for g in range(GLA_GROUP)]
        rows = [chunk_rows(n) for n in ns]
        st_bd = [[blockdiag(st_scr[d, n].astype(BF16)) for d in dirs] for n in ns]
        inter = [[_bdot_nt(qd_scr[d, rows[g], :], st_bd[g][d]) for d in dirs] for g in range(GLA_GROUP)]
        o = jnp.concatenate([o_ref[0, rows[g], :] + inter[g][0] + inter[g][1] for g in range(GLA_GROUP)], axis=0)
        o = _head_rms(o, seg_ref[...], gn_ref[0])
        for g in range(GLA_GROUP):
            o_ref[0, rows[g], :] = o[g * c:(g + 1) * c]
        return carry

    lax.fori_loop(0, n_chunks // GLA_GROUP, phase3, 0)
    if sfin_ref is not None:
        for d in range(2):
            s_fin = head_transpose_exact(finals[d])
            for h, hs in enumerate(heads):
                sfin_ref[0, 0, d, h] = s_fin[:, hs]


def _gla(p32, p16, gate_up, gate_b, gn, seg, layer, state, *, zero_init):
    b, s, _ = p32.shape
    n_chunks = s // CHUNK
    col = lambda g: pl.BlockSpec((1, s, W_GRP), lambda i, g=g: (i, 0, g))
    lay = lambda shape: pl.BlockSpec((1,) + shape, lambda i: (layer,) + (0,) * len(shape))
    in_specs = [col(P32[G_QD]), col(P32[G_KD]), col(P16[G_VD]),
                pl.BlockSpec((1, s, LANES), lambda i: (i, 0, GATE_BLK)),
                lay((LANES, 2 * W_GRP)), lay((1, 2 * W_GRP)),
                lay((1, W_GRP)), pl.BlockSpec((W_GRP, W_GRP), lambda i: (0, 0))]
    args = [p32, p32, p16, p16, gate_up, gate_b, gn, seg, state]
    state_spec = pl.BlockSpec((1, 1, 2, N_HEADS, HEAD_DIM, HEAD_DIM), lambda i: (i, layer, 0, 0, 0, 0))
    out_specs = [pl.BlockSpec((1, s, W_GRP), lambda i: (i, 0, 0))]
    out_shape = [jax.ShapeDtypeStruct((b, s, W_GRP), F32)]
    if zero_init:
        in_specs.append(pl.BlockSpec(memory_space=pl.ANY))
        out_specs.append(state_spec)
        out_shape.append(jax.ShapeDtypeStruct(state.shape, state.dtype))
    else:
        in_specs.append(state_spec)
    return pl.pallas_call(
        functools.partial(_gla_kernel, n_chunks=n_chunks, zero_init=zero_init),
        grid=(b,),
        in_specs=in_specs,
        out_specs=out_specs,
        out_shape=out_shape,
        input_output_aliases={len(args) - 1: 1} if zero_init else {},
        scratch_shapes=[pltpu.VMEM((2, s, W_GRP), BF16),
                        pltpu.VMEM((2, n_chunks, HEAD_DIM, W_GRP), F32),
                        pltpu.VMEM((2, max(n_chunks, 8), W_GRP), F32)],
        compiler_params=_cparams("arbitrary"),
        name="gla_zero" if zero_init else "gla",
    )(*args)


def _outproj_kernel(x_ref, mod_ref, oa_ref, oc_ref, od_ref, ga_ref, bb_ref, cb_ref, hb_ref, gb_ref,
                    gc_ref, gd_ref, cprev_ref, hprev_ref, cnext_ref, hnext_ref, cw_ref, w_ref, o_ref, *, seq_len):
    tm = x_ref.shape[0]
    f32 = lambda ref: ref[...].astype(F32)
    last = cprev_ref.shape[0] - 1
    u = f32(cb_ref) * f32(hb_ref)
    rid = lax.broadcasted_iota(jnp.int32, u.shape, 0)
    pos = (pl.program_id(0) * tm + rid) % seq_len
    u_prev = jnp.where(rid == 0, f32(cprev_ref)[last:, :] * f32(hprev_ref)[last:, :], pltpu.roll(u, 1, axis=0))
    u_next = jnp.where(rid == tm - 1, f32(cnext_ref)[0:1, :] * f32(hnext_ref)[0:1, :], pltpu.roll(u, tm - 1, axis=0))
    u_prev = jnp.where(pos == 0, 0.0, u_prev)
    u_next = jnp.where(pos == seq_len - 1, 0.0, u_next)
    cw = cw_ref[0]
    conv = u_prev * cw[0:1, :] + u * cw[1:2, :] + u_next * cw[2:3, :]
    y_a = oa_ref[...] * _silu(f32(ga_ref))
    y_b = f32(bb_ref) * conv * _silu(f32(gb_ref))
    y_c = oc_ref[...] * _silu(f32(gc_ref))
    y_d = od_ref[...] * _silu(f32(gd_ref))
    y = jnp.concatenate([y_a, y_b, y_c, y_d], axis=1).astype(BF16)
    gate = mod_ref[0, :, 2 * D_MODEL:3 * D_MODEL]
    o_ref[...] = x_ref[...] + gate * jnp.dot(y, w_ref[0], preferred_element_type=F32)


def _outproj(x, mod, layer, o_a, o_c, o_d, p, conv_w, w_out, *, tm):
    b, s, _ = x.shape
    n = b * s
    per_batch = mod.shape[0] > 1
    assert s % tm == 0 or (tm % s == 0 and not per_batch)
    tiles_per_seq = max(s // tm, 1)
    flat = lambda a: a.reshape(n, a.shape[-1])
    tile = lambda width: pl.BlockSpec((tm, width), lambda t: (t, 0))
    col = lambda g: pl.BlockSpec((tm, W_GRP), lambda t, g=g: (t, g))
    halo = 16
    th = tm // halo
    prev = lambda g: pl.BlockSpec((halo, W_GRP), lambda t, g=g: (jnp.maximum(t * th - 1, 0), g))
    nxt = lambda g: pl.BlockSpec((halo, W_GRP), lambda t, g=g: (jnp.minimum((t + 1) * th, n // halo - 1), g))
    out = pl.pallas_call(
        functools.partial(_outproj_kernel, seq_len=s),
        grid=(n // tm,),
        in_specs=[tile(D_MODEL),
                  pl.BlockSpec((1, 1, 3 * D_MODEL),
                               (lambda t: (t // tiles_per_seq, 0, 0)) if per_batch else (lambda t: (0, 0, 0))),
                  tile(W_GRP), tile(W_GRP), tile(W_GRP),
                  *[col(P16[g]) for g in (G_GA, G_BB, G_CB, G_HB, G_GB, G_GC, G_GD)],
                  prev(P16[G_CB]), prev(P16[G_HB]), nxt(P16[G_CB]), nxt(P16[G_HB]),
                  pl.BlockSpec((1, 3, W_GRP), lambda t: (layer, 0, 0)),
                  pl.BlockSpec((1, D_MODEL, D_MODEL), lambda t: (layer, 0, 0))],
        out_specs=tile(D_MODEL),
        out_shape=jax.ShapeDtypeStruct((n, D_MODEL), F32),
        compiler_params=_cparams("arbitrary"),
        name="outproj",
    )(flat(x), mod, flat(o_a), flat(o_c), flat(o_d), *([flat(p)] * 11), conv_w, w_out)
    return out.reshape(b, s, D_MODEL)


def kernel(x_prompt, x_sample, cache_k, cache_v, state_gla, c, c_ctx, norm_g, w_mod, b_mod, w_in, q_norm_g,
           k_norm_g, rpb, conv_w, gla_up_f, gla_bias_f, gla_up_b, gla_bias_b, gla_norm_g, w_out):
    dec_batch = c.shape[0]
    pad_rows = 8 - (1 + dec_batch) % 8
    cvec = jnp.concatenate([c_ctx[None], c, jnp.zeros((pad_rows, D_MODEL), F32)], axis=0)
    mod = _modulation(cvec, w_mod, b_mod)
    mod_ctx = mod[:, 0:1].reshape(DEPTH, 1, 1, 3 * D_MODEL)
    mod_smp = mod[:, 1:1 + dec_batch].reshape(DEPTH, dec_batch, 1, 3 * D_MODEL)

    w_in_b = jnp.pad(w_in, ((0, 0), (0, 0), (0, N_IN_PAD - N_IN))).astype(BF16)
    w_out_b = w_out.astype(BF16)
    seg = _seg_ones()
    norm_g3 = norm_g.reshape(DEPTH, 1, D_MODEL)
    qn = jnp.tile(q_norm_g, (1, N_HEADS)).reshape(DEPTH, 1, W_GRP)
    kn = jnp.tile(k_norm_g, (1, N_HEADS)).reshape(DEPTH, 1, W_GRP)
    gn = jnp.tile(gla_norm_g, (1, N_HEADS)).reshape(DEPTH, 1, W_GRP)
    gate_up = jnp.zeros((DEPTH, LANES, 2 * W_GRP), F32)
    gate_up = gate_up.at[:, :GATE_RANK, :W_GRP].set(gla_up_f).at[:, GATE_RANK:2 * GATE_RANK, W_GRP:].set(gla_up_b)
    gate_b = jnp.concatenate([gla_bias_f, gla_bias_b], axis=-1).reshape(DEPTH, 1, 2 * W_GRP)
    bias = _bias_tables(rpb)
    cache_k_b = cache_k.astype(BF16)
    cache_v_b = jnp.concatenate([cache_v.astype(BF16), jnp.ones(cache_v.shape, BF16)], axis=-1)

    x = x_prompt
    batch, seq, _ = x.shape
    new_cache_k = jnp.zeros((batch, DEPTH, N_HEADS, seq, HEAD_DIM), F32)
    new_cache_v = jnp.zeros((batch, DEPTH, N_HEADS, seq, HEAD_DIM), F32)
    new_state = jnp.zeros((batch, DEPTH, 2, N_HEADS, HEAD_DIM, HEAD_DIM), F32)
    for l in range(DEPTH):
        p32, p16, new_cache_k, new_cache_v = _inproj(x, mod_ctx[l], l, norm_g3, w_in_b, qn, kn, seg,
                                                     tm=TOKEN_TILE, caches=(new_cache_k, new_cache_v))
        o_a = _attn_ctx(p16)
        o_c = _fft_ctx(p32)
        o_d, new_state = _gla(p32, p16, gate_up, gate_b, gn, seg, l, new_state, zero_init=True)
        x = _outproj(x, mod_ctx[l], l, o_a, o_c, o_d, p16, conv_w, w_out_b, tm=TOKEN_TILE)
    y_prompt = x

    xs = x_sample
    for l in range(DEPTH):
        p32, p16, k_hm, v_hm = _inproj(xs, mod_smp[l], l, norm_g3, w_in_b, qn, kn, seg, tm=TOKEN_TILE)
        o_a = _attn_nbr(p16, k_hm, v_hm, cache_k_b, cache_v_b, bias, l)
        o_c = _fft_grid(p32)
        (o_d,) = _gla(p32, p16, gate_up, gate_b, gn, seg, l, state_gla, zero_init=False)
        xs = _outproj(xs, mod_smp[l], l, o_a, o_c, o_d, p16, conv_w, w_out_b, tm=TOKEN_TILE)
    return (y_prompt, xs, new_cache_k, new_cache_v, new_state)
```

```python
import functools
import math

import numpy as np
import jax
import jax.numpy as jnp
from jax import lax
from jax.experimental import pallas as pl
from jax.experimental.pallas import tpu as pltpu

F32 = jnp.float32
BF16 = jnp.bfloat16

D_MODEL = 1024
DEPTH = 4
GRID_W = 64
W_GRP = 256
HEAD_DIM = 64
N_HEADS = 4
WIN_H = 8
WIN_W = 16
N_FOURIER = 4
GATE_RANK = 16
GLA_TAU = 16.0
CHUNK = 64
RMS_EPS = 1e-6
N_IN = 14 * W_GRP + 2 * GATE_RANK
LANES = 128
N_IN_PAD = -(-N_IN // LANES) * LANES
G_QA, G_KA, G_VA, G_GA, G_BB, G_CB, G_HB, G_GB, G_UC, G_GC, G_QD, G_KD, G_VD, G_GD = range(14)
F32_GROUPS = (G_UC, G_QD, G_KD)
BF16_GROUPS = (G_QA, G_KA, G_VA, G_VD, G_GA, G_BB, G_CB, G_HB, G_GB, G_GC, G_GD)
P32 = {g: i for i, g in enumerate(F32_GROUPS)}
P16 = {g: i for i, g in enumerate(BF16_GROUPS)}
N_P32 = len(F32_GROUPS) * W_GRP
N_P16 = len(BF16_GROUPS) * W_GRP + LANES
GATE_BLK = len(BF16_GROUPS) * W_GRP // LANES
NEG_BIG = -1e30
VMEM_LIMIT = 56 * 1024 * 1024
TOKEN_TILE = 512
OUTPROJ_TILE = 1024


def _cparams(*sem):
    return pltpu.CompilerParams(dimension_semantics=sem, vmem_limit_bytes=VMEM_LIMIT)


def _bdot(a, b):
    return jnp.dot(a.astype(BF16), b.astype(BF16), preferred_element_type=F32)


def _bdot_nt(a, b):
    return lax.dot_general(a.astype(BF16), b.astype(BF16), (((1,), (1,)), ((), ())),
                           preferred_element_type=F32)


def _split2(x):
    hi = x.astype(BF16)
    lo = (x - hi.astype(F32)).astype(BF16)
    return hi, lo


def _split3(x):
    h1 = x.astype(BF16)
    r = x - h1.astype(F32)
    h2 = r.astype(BF16)
    h3 = (r - h2.astype(F32)).astype(BF16)
    return h1, h2, h3


def _silu(x):
    return x / (1.0 + jnp.exp(-x))


def _head_rms(t, seg, g):
    n = t.shape[0]
    both = jnp.dot(jnp.concatenate(_split2(t * t), axis=0), seg, preferred_element_type=F32)
    ss = both[:n] + both[n:]
    return t * lax.rsqrt(ss * (1.0 / HEAD_DIM) + RMS_EPS) * g


def _dft_cos_sin(n):
    k = np.arange(n)
    ang = 2.0 * np.pi * ((k[:, None] * k[None, :]) % n) / n
    return np.cos(ang), np.sin(ang)


def _block_diag(m, reps):
    n = m.shape[0]
    out = np.zeros((n * reps, n * reps), m.dtype)
    for i in range(reps):
        out[i * n:(i + 1) * n, i * n:(i + 1) * n] = m
    return out


def _seg_ones():
    return jnp.asarray(_block_diag(np.ones((HEAD_DIM, HEAD_DIM)), N_HEADS), BF16)


def _mod_kernel(c_ref, w_ref, b_ref, o_ref):
    o_ref[0] = _bdot(_silu(c_ref[...]), w_ref[0]) + b_ref[0]


def _modulation(cvec, w_mod, b_mod):
    rows = cvec.shape[0]
    return pl.pallas_call(
        _mod_kernel,
        grid=(DEPTH, 3),
        in_specs=[pl.BlockSpec((rows, D_MODEL), lambda l, j: (0, 0)),
                  pl.BlockSpec((1, D_MODEL, D_MODEL), lambda l, j: (l, 0, j)),
                  pl.BlockSpec((1, 1, D_MODEL), lambda l, j: (l, 0, j))],
        out_specs=pl.BlockSpec((1, rows, D_MODEL), lambda l, j: (l, 0, j)),
        out_shape=jax.ShapeDtypeStruct((DEPTH, rows, 3 * D_MODEL), F32),
        compiler_params=_cparams("arbitrary", "arbitrary"),
        name="adaln_mod",
    )(cvec, w_mod, b_mod.reshape(DEPTH, 1, 3 * D_MODEL))


INPROJ_ORDER = (2, 0, 3, 1, 4, 5, 6, 7, 8, 9, 10, 11, 12, 13)
INPROJ_PIECES = 8


def _inproj_kernel(x_ref, mod_ref, g_ref, w_ref, wg_ref, qn_ref, kn_ref, seg_ref, *rest,
                   for_window_attn, n_seq):
    p32_ref, p16_ref, k_ref, v_ref, h_even, h_odd = rest[-6:]
    t = pl.program_id(0)
    rows = p32_ref.shape[0] // n_seq

    def normed(x_ref, mod_ref, rows=slice(None)):
        x = x_ref[rows, :]
        ms = jnp.mean(x * x, axis=-1, keepdims=True)
        y = x * lax.rsqrt(ms + RMS_EPS) * g_ref[0]
        shift = mod_ref[0, :, 0:D_MODEL]
        scale = mod_ref[0, :, D_MODEL:2 * D_MODEL]
        return (y * (1.0 + scale) + shift).astype(BF16)

    def store_heads(ref, val, ones):
        for q in range(n_seq):
            for h in range(N_HEADS):
                th = val[q * rows:(q + 1) * rows, h * HEAD_DIM:(h + 1) * HEAD_DIM]
                if ones:
                    th = jnp.concatenate([th, jnp.ones_like(th)], axis=1)
                if for_window_attn:
                    ref[0, h] = th.astype(ref.dtype)
                else:
                    ref[q, 0, h] = th

    def step(h_cur, h_nxt):
        seg = seg_ref[...]
        piece = p32_ref.shape[0] // INPROJ_PIECES
        first_late = len(INPROJ_ORDER) - INPROJ_PIECES
        for idx, j in enumerate(INPROJ_ORDER):
            if idx >= first_late:
                part = slice((idx - first_late) * piece, (idx - first_late + 1) * piece)
                h_nxt[part, :] = normed(x_ref, mod_ref, part)
            cs = slice(j * W_GRP, (j + 1) * W_GRP)
            y = jnp.dot(h_cur[...], w_ref[0, :, cs], preferred_element_type=F32)
            if j == G_QA:
                y = _head_rms(y, seg, qn_ref[0]) * (HEAD_DIM ** -0.5)
            elif j == G_KA:
                y = _head_rms(y, seg, kn_ref[0])
            if j in P32:
                p32_ref[:, P32[j] * W_GRP:(P32[j] + 1) * W_GRP] = y
            else:
                p16_ref[:, P16[j] * W_GRP:(P16[j] + 1) * W_GRP] = y.astype(BF16)
            if j == G_KA:
                store_heads(k_ref, y, False)
            elif j == G_VA:
                store_heads(v_ref, y, for_window_attn)
        p16_ref[:, GATE_BLK * LANES:N_P16] = jnp.dot(h_cur[...], wg_ref[0], preferred_element_type=F32).astype(BF16)

    @pl.when(t == 0)
    def _():
        h_even[...] = normed(x_ref, mod_ref)
        for ref in (p32_ref, p16_ref, k_ref, v_ref):
            ref[...] = jnp.zeros(ref.shape, ref.dtype)

    @pl.when(t % 2 == 1)
    def _():
        step(h_even, h_odd)

    @pl.when(jnp.logical_and(t > 0, t % 2 == 0))
    def _():
        step(h_odd, h_even)


def _inproj(x, mod, layer, norm_g, w_in, qn, kn, seg, *, tm, caches=None):
    b, s, _ = x.shape
    for_window_attn = caches is None
    assert s % tm == 0 if for_window_attn else tm % s == 0
    n_tiles = b * s // tm
    tiles_per_seq = max(s // tm, 1)
    n_seq = max(tm // s, 1)
    per_batch = mod.shape[0] > 1
    prep = lambda t: jnp.minimum(t, n_tiles - 1)
    done = lambda t: jnp.maximum(t - 1, 0)
    lay = lambda *dims: pl.BlockSpec((1,) + dims, lambda t: (layer,) + (0,) * len(dims))
    in_specs = [pl.BlockSpec((tm, D_MODEL), lambda t: (prep(t), 0)),
                pl.BlockSpec((1, 1, 3 * D_MODEL),
                             (lambda t: (prep(t) // tiles_per_seq, 0, 0)) if per_batch else (lambda t: (0, 0, 0))),
                lay(1, D_MODEL),
                pl.BlockSpec((1, D_MODEL, 14 * W_GRP), lambda t: (layer, 0, 0)),
                pl.BlockSpec((1, D_MODEL, LANES), lambda t: (layer, 0, 14 * W_GRP // LANES)),
                lay(1, W_GRP), lay(1, W_GRP),
                pl.BlockSpec((W_GRP, W_GRP), lambda t: (0, 0))]
    x2 = x.reshape(b * s, D_MODEL)
    args = [x2, mod, norm_g, w_in, w_in, qn, kn, seg]
    out_shape = [jax.ShapeDtypeStruct((b * s, N_P32), F32), jax.ShapeDtypeStruct((b * s, N_P16), BF16)]
    out_specs = [pl.BlockSpec((tm, N_P32), lambda t: (done(t), 0)), pl.BlockSpec((tm, N_P16), lambda t: (done(t), 0))]
    aliases = {}
    if for_window_attn:
        for width in (HEAD_DIM, 2 * HEAD_DIM):
            out_shape.append(jax.ShapeDtypeStruct((b, N_HEADS, s, width), BF16))
            out_specs.append(pl.BlockSpec((1, N_HEADS, tm, width),
                                          lambda t: (done(t) // tiles_per_seq, 0, done(t) % tiles_per_seq, 0)))
    else:
        for i, cache in enumerate(caches):
            aliases[len(args)] = 2 + i
            args.append(cache)
            in_specs.append(pl.BlockSpec(memory_space=pl.ANY))
            out_shape.append(jax.ShapeDtypeStruct(cache.shape, cache.dtype))
            out_specs.append(pl.BlockSpec((n_seq, 1, N_HEADS, s, HEAD_DIM), lambda t: (done(t), layer, 0, 0, 0)))
    p32, p16, k, v = pl.pallas_call(
        functools.partial(_inproj_kernel, for_window_attn=for_window_attn, n_seq=n_seq),
        grid=(n_tiles + 1,),
        in_specs=in_specs,
        out_specs=out_specs,
        out_shape=out_shape,
        input_output_aliases=aliases,
        scratch_shapes=[pltpu.VMEM((tm, D_MODEL), BF16)] * 2,
        compiler_params=_cparams("arbitrary"),
        name="inproj_grid" if for_window_attn else "inproj_ctx",
    )(*args)
    return p32.reshape(b, s, N_P32), p16.reshape(b, s, N_P16), k, v


def _attn_ctx_kernel(q_ref, k_ref, v_ref, o_ref):
    for h in range(N_HEADS):
        hs = slice(h * HEAD_DIM, (h + 1) * HEAD_DIM)
        s = _bdot_nt(q_ref[0, :, hs], k_ref[0, :, hs])
        e = jnp.exp(s - jnp.max(s, axis=-1, keepdims=True))
        l = jnp.sum(e, axis=-1, keepdims=True)
        o_ref[0, :, hs] = _bdot(e, v_ref[0, :, hs]) / l


def _attn_ctx(p):
    b, s, _ = p.shape
    col = lambda g: pl.BlockSpec((1, s, W_GRP), lambda i, g=g: (i, 0, g))
    return pl.pallas_call(
        _attn_ctx_kernel,
        grid=(b,),
        in_specs=[col(P16[G_QA]), col(P16[G_KA]), col(P16[G_VA])],
        out_specs=pl.BlockSpec((1, s, W_GRP), lambda i: (i, 0, 0)),
        out_shape=jax.ShapeDtypeStruct((b, s, W_GRP), F32),
        compiler_params=_cparams("arbitrary"),
        name="attn_ctx",
    )(p, p, p)


def _bias_kernel(rpb_ref, o_ref):
    lh = pl.program_id(0)
    q = lax.broadcasted_iota(jnp.int32, (GRID_W, GRID_W), 0)
    kc = lax.broadcasted_iota(jnp.int32, (GRID_W, GRID_W), 1)
    dc = jnp.clip(kc - q + (WIN_W - 1), 0, 2 * WIN_W - 2)
    cs = jnp.clip(q - WIN_W // 2, 0, GRID_W - WIN_W)
    in_win = jnp.where(kc >= cs, jnp.where(kc < cs + WIN_W, 1, 0), 0) == 1
    n_dr, n_dc = 2 * WIN_H - 1, 2 * WIN_W - 1
    tiles = []
    for dr in range(n_dr):
        base = (lh * n_dr + dr) * n_dc
        t = jnp.zeros((GRID_W, GRID_W), F32)
        for d in range(n_dc):
            t = jnp.where(dc == d, rpb_ref[base + d], t)
        tiles.append(jnp.where(in_win, t, NEG_BIG))
    for dl in range(WIN_H):
        for i in range(WIN_H):
            o_ref[0, dl, :, i * GRID_W:(i + 1) * GRID_W] = tiles[dl + i]


def _bias_tables(rpb):
    out = pl.pallas_call(
        _bias_kernel,
        grid=(DEPTH * N_HEADS,),
        in_specs=[pl.BlockSpec(memory_space=pltpu.SMEM)],
        out_specs=pl.BlockSpec((1, WIN_H, GRID_W, WIN_H * GRID_W), lambda i: (i, 0, 0, 0)),
        out_shape=jax.ShapeDtypeStruct((DEPTH * N_HEADS, WIN_H, GRID_W, WIN_H * GRID_W), F32),
        compiler_params=_cparams("arbitrary"),
        name="rpb_tables",
    )(rpb.reshape(-1))
    return out.reshape(DEPTH, N_HEADS, WIN_H, GRID_W, WIN_H * GRID_W)


ATTN_ROW_GROUP = 8


def _attn_nbr_kernel(q_ref, k_ref, v_ref, ck_ref, cv_ref, bias_ref, o_ref, *, rows_per_step):
    rb = pl.program_id(1)
    n_rows = k_ref.shape[2] // GRID_W
    kh = min(WIN_H, n_rows)
    n_lat = kh * GRID_W
    heads = [slice(h * HEAD_DIM, (h + 1) * HEAD_DIM) for h in range(N_HEADS)]
    units = [(g, h) for g in range(ATTN_ROW_GROUP) for h in range(N_HEADS)]
    low_half = lax.broadcasted_iota(jnp.int32, (GRID_W, 2 * HEAD_DIM), 1) < HEAD_DIM

    def rows_body(jj, carry):
        q0, k0, dl, q = [], [], [], []
        for g in range(ATTN_ROW_GROUP):
            j = jj * ATTN_ROW_GROUP + g
            r = rb * rows_per_step + j
            rs = jnp.clip(r - kh // 2, 0, n_rows - kh)
            dl.append(rs - r + (WIN_H - 1))
            k0.append(pl.multiple_of(rs * GRID_W, GRID_W))
            q0.append(pl.multiple_of(j * GRID_W, GRID_W))
            q.append(q_ref[0, pl.ds(q0[g], GRID_W), :].astype(BF16))
        s_lat = [_bdot_nt(q[g][:, heads[h]], k_ref[0, h, pl.ds(k0[g], n_lat), :]) + bias_ref[0, h, dl[g]]
                 for g, h in units]
        q_all = jnp.concatenate(q, axis=0)
        s_ctx_all = [_bdot_nt(q_all[:, hs], ck_ref[0, 0, h]) for h, hs in enumerate(heads)]
        s_ctx = [s_ctx_all[h][g * GRID_W:(g + 1) * GRID_W] for g, h in units]
        m = [jnp.maximum(jnp.max(a, axis=-1, keepdims=True), jnp.max(c, axis=-1, keepdims=True))
             for a, c in zip(s_lat, s_ctx)]
        e_lat = [jnp.exp(a - mu).astype(BF16) for a, mu in zip(s_lat, m)]
        e_ctx = [jnp.exp(c - mu).astype(BF16) for c, mu in zip(s_ctx, m)]
        pv_ctx = [jnp.dot(jnp.concatenate([e_ctx[g * N_HEADS + h] for g in range(ATTN_ROW_GROUP)], axis=0),
                          cv_ref[0, 0, h], preferred_element_type=F32) for h in range(N_HEADS)]
        res = [jnp.dot(e_lat[u], v_ref[0, h, pl.ds(k0[g], n_lat), :], preferred_element_type=F32)
               + pv_ctx[h][g * GRID_W:(g + 1) * GRID_W]
               for u, (g, h) in enumerate(units)]
        for g in range(ATTN_ROW_GROUP):
            pairs = []
            for h in range(0, N_HEADS, 2):
                even, odd = res[g * N_HEADS + h], res[g * N_HEADS + h + 1]
                pairs.append(jnp.where(low_half, even / pltpu.roll(even, HEAD_DIM, axis=1),
                                       pltpu.roll(odd, HEAD_DIM, axis=1) / odd))
            o_ref[0, pl.ds(q0[g], GRID_W), :] = jnp.concatenate(pairs, axis=1)
        return carry

    lax.fori_loop(0, rows_per_step // ATTN_ROW_GROUP, rows_body, 0)


def _attn_nbr(p, k_hm, v_hm, cache_k, cache_v, bias, layer, *, rows_per_step=16):
    b, s, _ = p.shape
    tq = rows_per_step * GRID_W
    past = cache_v.shape[3]
    ctx_spec = lambda *dims: pl.BlockSpec((1, 1, N_HEADS) + dims, lambda i, j: (i, layer, 0, 0, 0))
    return pl.pallas_call(
        functools.partial(_attn_nbr_kernel, rows_per_step=rows_per_step),
        grid=(b, s // tq),
        in_specs=[pl.BlockSpec((1, tq, W_GRP), lambda i, j: (i, j, P16[G_QA])),
                  pl.BlockSpec((1, N_HEADS, s, HEAD_DIM), lambda i, j: (i, 0, 0, 0)),
                  pl.BlockSpec((1, N_HEADS, s, 2 * HEAD_DIM), lambda i, j: (i, 0, 0, 0)),
                  ctx_spec(past, HEAD_DIM), ctx_spec(past, 2 * HEAD_DIM),
                  pl.BlockSpec((1, N_HEADS, WIN_H, GRID_W, WIN_H * GRID_W), lambda i, j: (layer, 0, 0, 0, 0))],
        out_specs=pl.BlockSpec((1, tq, W_GRP), lambda i, j: (i, j, 0)),
        out_shape=jax.ShapeDtypeStruct((b, s, W_GRP), F32),
        compiler_params=_cparams("arbitrary", "arbitrary"),
        name="attn_nbr",
    )(p, k_hm, v_hm, cache_k, cache_v, bias)


def _fft_ctx_consts(s):
    c, sn = _dft_cos_sin(s)
    cc, sc = _dft_cos_sin(W_GRP // N_FOURIER)
    m1 = np.concatenate([c, -sn], axis=0)
    m2 = np.concatenate([_block_diag(cc, N_FOURIER), _block_diag(sc, N_FOURIER)], axis=0)
    return jnp.asarray(m1, F32), jnp.asarray(m2, F32)


def _fft_ctx_kernel(u_ref, m1_ref, m2_ref, o_ref, *, scale):
    s = u_ref.shape[1]
    t = _bdot(m1_ref[...], u_ref[0])
    x = jnp.concatenate([t[:s], t[s:]], axis=1)
    o_ref[0] = _bdot(x, m2_ref[...]) * scale


def _fft_ctx(p):
    b, s, _ = p.shape
    consts = _fft_ctx_consts(s)
    full = lambda a: pl.BlockSpec(a.shape, lambda i: (0,) * a.ndim)
    return pl.pallas_call(
        functools.partial(_fft_ctx_kernel, scale=1.0 / math.sqrt(s * (W_GRP // N_FOURIER))),
        grid=(b,),
        in_specs=[pl.BlockSpec((1, s, W_GRP), lambda i: (i, 0, P32[G_UC]))] + [full(a) for a in consts],
        out_specs=pl.BlockSpec((1, s, W_GRP), lambda i: (i, 0, 0)),
        out_shape=jax.ShapeDtypeStruct((b, s, W_GRP), F32),
        compiler_params=_cparams("arbitrary"),
        name="fft_ctx",
    )(p, *consts)


def _fft_grid_consts():
    n = GRID_W
    c, sn = _dft_cos_sin(n)
    cc, sc = _dft_cos_sin(W_GRP // N_FOURIER)
    m1 = np.concatenate([c, -sn], axis=0)
    m2 = np.block([[c, sn], [-sn, c]])
    reps = LANES // (W_GRP // N_FOURIER)
    m3 = np.concatenate([_block_diag(cc, reps), _block_diag(sc, reps)], axis=0)
    k = np.arange(n)
    ang = 2.0 * np.pi * (k[:, None] * k[None, :]) / (n * n)
    twr = np.broadcast_to(np.cos(ang)[:, :, None], (n, n, LANES))
    twi = np.broadcast_to(-np.sin(ang)[:, :, None], (n, n, LANES))
    return tuple(jnp.asarray(m, F32) for m in (m1, m2, m3, twr, twi))


FFT_UNROLL = 8
FFT_PITCH = 72


def _fft_grid_kernel(u_ref, m1_ref, m2_ref, m3_ref, twr_ref, twi_ref, o_ref, tr_scr, ti_scr, *, scale):
    n = GRID_W
    m1, m2, m3 = (r[...].astype(BF16) for r in (m1_ref, m2_ref, m3_ref))

    def stage1(b, carry):
        a_rows = u_ref[0, pl.ds(b, n, stride=n), :]
        t = _bdot(m1, a_rows)
        tr, ti = t[:n], t[n:]
        wr, wi = twr_ref[b], twi_ref[b]
        tr_scr[pl.ds(b, n, stride=FFT_PITCH), :] = tr * wr - ti * wi
        ti_scr[pl.ds(b, n, stride=FFT_PITCH), :] = tr * wi + ti * wr
        return carry

    lax.fori_loop(0, n, stage1, 0, unroll=FFT_UNROLL)

    def stage2(dd, carry):
        xcs = []
        for u in range(FFT_UNROLL):
            r0 = pl.multiple_of((dd * FFT_UNROLL + u) * FFT_PITCH, 8)
            g = jnp.concatenate([tr_scr[pl.ds(r0, n), :], ti_scr[pl.ds(r0, n), :]], axis=0)
            x = _bdot(m2, g)
            xcs.append(jnp.concatenate([x[:n], x[n:]], axis=1))
        out = _bdot(jnp.concatenate(xcs, axis=0), m3) * scale
        for u in range(FFT_UNROLL):
            o_ref[0, pl.ds(dd * FFT_UNROLL + u, n, stride=n), :] = out[u * n:(u + 1) * n]
        return carry

    lax.fori_loop(0, n // FFT_UNROLL, stage2, 0)


def _fft_grid(p):
    b, s, _ = p.shape
    assert s == GRID_W * GRID_W
    consts = _fft_grid_consts()
    full = lambda a: pl.BlockSpec(a.shape, lambda i, j: (0,) * a.ndim)
    blk0 = P32[G_UC] * W_GRP // LANES
    return pl.pallas_call(
        functools.partial(_fft_grid_kernel, scale=1.0 / math.sqrt(s * (W_GRP // N_FOURIER))),
        grid=(b, W_GRP // LANES),
        in_specs=[pl.BlockSpec((1, s, LANES), lambda i, j: (i, 0, blk0 + j))] + [full(a) for a in consts],
        out_specs=pl.BlockSpec((1, s, LANES), lambda i, j: (i, 0, j)),
        out_shape=jax.ShapeDtypeStruct((b, s, W_GRP), F32),
        scratch_shapes=[pltpu.VMEM((GRID_W * FFT_PITCH, LANES), F32)] * 2,
        compiler_params=_cparams("arbitrary", "arbitrary"),
        name="fft_grid",
    )(p, *consts)


GLA_CHUNKS_PER_TRIP = 8


def _gla_kernel(*refs, n_chunks, n_seq, group, zero_init):
    if zero_init:
        (q_ref, k_ref, v_ref, a_ref, up_ref, gb_ref, gn_ref, seg_ref, _,
         o_ref, sfin_ref, qd_scr, st_scr, dec_scr) = refs
        s0_ref = None
    else:
        (q_ref, k_ref, v_ref, a_ref, up_ref, gb_ref, gn_ref, seg_ref, s0_ref,
         o_ref, qd_scr, st_scr, dec_scr) = refs
        sfin_ref = None
    c = CHUNK
    row = lax.broadcasted_iota(jnp.int32, (c, c), 0)
    col = lax.broadcasted_iota(jnp.int32, (c, c), 1)
    heads = [slice(h * HEAD_DIM, (h + 1) * HEAD_DIM) for h in range(N_HEADS)]
    wrow = lax.broadcasted_iota(jnp.int32, (W_GRP, W_GRP), 0)
    wcol = lax.broadcasted_iota(jnp.int32, (W_GRP, W_GRP), 1)
    head_diag = (wrow // HEAD_DIM) == (wcol // HEAD_DIM)

    def blockdiag(x):
        return jnp.where(head_diag, jnp.concatenate([x] * N_HEADS, axis=0), jnp.zeros((), x.dtype))

    trow = lax.broadcasted_iota(jnp.int32, (c, W_GRP), 0)
    tcol = lax.broadcasted_iota(jnp.int32, (c, W_GRP), 1) % c
    keep = [trow >= tcol, trow <= tcol]
    eye_tiled = jnp.where(trow == tcol, 1.0, 0.0).astype(BF16)

    def head_transpose_exact(x):
        return sum(_bdot_nt(eye_tiled, blockdiag(part)) for part in _split3(x))

    sum_ops = [jnp.where(row >= col, 1.0, 0.0).astype(BF16),
               jnp.where(row <= col, 1.0, 0.0).astype(BF16)]
    dirs = (0, 1)

    def chunk_rows(n):
        return pl.ds(pl.multiple_of(n * c, c), c)

    def phase1(i, carry):
        ns = [i * group + g for g in range(group)]
        rows = [chunk_rows(n) for n in ns]
        x_all = _bdot(jnp.concatenate([a_ref[0, r, :] for r in rows], axis=0), up_ref[0]) + gb_ref[0]
        x = [[x_all[g * c:(g + 1) * c, d * W_GRP:(d + 1) * W_GRP] for d in dirs] for g in range(group)]
        la = [[(jnp.minimum(xd, 0.0) - jnp.log(1.0 + jnp.exp(-jnp.abs(xd)))) * (1.0 / GLA_TAU) for xd in xg]
              for xg in x]
        cum = [[sum(jnp.dot(sum_ops[d], part, preferred_element_type=F32) for part in _split3(lg[d]))
                for d in dirs] for lg in la]
        tot_row = [[cg[0][c - 1:c, :], cg[1][0:1, :]] for cg in cum]
        q = [q_ref[0, r, :] * (HEAD_DIM ** -0.5) for r in rows]
        k = [k_ref[0, r, :] for r in rows]
        v_bd = [blockdiag(v_ref[0, r, :].astype(BF16)) for r in rows]
        q_dec = [[(q[g] * jnp.exp(cum[g][d])).astype(BF16) for d in dirs] for g in range(group)]
        k_inv = [[blockdiag((k[g] * jnp.exp(-cum[g][d])).astype(BF16)) for d in dirs] for g in range(group)]
        k_end = [[blockdiag((k[g] * jnp.exp(tot_row[g][d] - cum[g][d])).astype(BF16)) for d in dirs]
                 for g in range(group)]
        scores = [[_bdot_nt(q_dec[g][d], k_inv[g][d]) for d in dirs] for g in range(group)]
        v_t = [_bdot_nt(eye_tiled, vb).astype(BF16) for vb in v_bd]
        upd = [[_bdot(v_t[g], k_end[g][d]) for d in dirs] for g in range(group)]
        att = [[jnp.where(keep[d], scores[g][d], 0.0).astype(BF16) for d in dirs] for g in range(group)]
        for g in range(group):
            both = _bdot(jnp.concatenate(att[g], axis=0), v_bd[g])
            o_ref[0, rows[g], :] = both[:c] + both[c:]
            for d in dirs:
                qd_scr[d, rows[g], :] = q_dec[g][d]
                dec_scr[d, pl.ds(ns[g], 1), :] = jnp.exp(tot_row[g][d])
                st_scr[d, ns[g]] = upd[g][d]
        return carry

    lax.fori_loop(0, n_chunks // group, phase1, 0)

    def init_state(d):
        if zero_init:
            return jnp.zeros((HEAD_DIM, W_GRP), F32)
        return head_transpose_exact(jnp.concatenate([s0_ref[0, 0, d, h] for h in range(N_HEADS)], axis=1))

    per_seq = n_chunks // n_seq
    finals = []
    for q in range(n_seq):
        finals.append([])
        for d in range(2):
            def scan_body(i, st, d=d, q=q):
                n = q * per_seq + (i if d == 0 else per_seq - 1 - i)
                upd = st_scr[d, n]
                st_scr[d, n] = st
                return st * dec_scr[d, pl.ds(n, 1), :] + upd
            finals[q].append(lax.fori_loop(0, per_seq, scan_body, init_state(d)))

    def phase3(i, carry):
        ns = [i * group + g for g in range(group)]
        rows = [chunk_rows(n) for n in ns]
        st_bd = [[blockdiag(st_scr[d, n].astype(BF16)) for d in dirs] for n in ns]
        inter = [[_bdot_nt(qd_scr[d, rows[g], :], st_bd[g][d]) for d in dirs] for g in range(group)]
        o = jnp.concatenate([o_ref[0, rows[g], :] + inter[g][0] + inter[g][1] for g in range(group)], axis=0)
        o = _head_rms(o, seg_ref[...], gn_ref[0])
        for g in range(group):
            o_ref[0, rows[g], :] = o[g * c:(g + 1) * c]
        return carry

    lax.fori_loop(0, n_chunks // group, phase3, 0)
    if sfin_ref is not None:
        for q in range(n_seq):
            for d in range(2):
                s_fin = head_transpose_exact(finals[q][d])
                for h, hs in enumerate(heads):
                    sfin_ref[q, 0, d, h] = s_fin[:, hs]


def _gla(p32, p16, gate_up, gate_b, gn, seg, layer, state, *, zero_init):
    b, s, _ = p32.shape
    n_seq = max(1, min(b, GLA_CHUNKS_PER_TRIP * CHUNK // s)) if zero_init else 1
    assert b % n_seq == 0
    rows = n_seq * s
    n_chunks = rows // CHUNK
    group = min(GLA_CHUNKS_PER_TRIP, n_chunks)
    assert n_chunks % group == 0
    fold = lambda a: a.reshape(b // n_seq, rows, a.shape[-1])
    col = lambda g: pl.BlockSpec((1, rows, W_GRP), lambda i, g=g: (i, 0, g))
    lay = lambda shape: pl.BlockSpec((1,) + shape, lambda i: (layer,) + (0,) * len(shape))
    in_specs = [col(P32[G_QD]), col(P32[G_KD]), col(P16[G_VD]),
                pl.BlockSpec((1, rows, LANES), lambda i: (i, 0, GATE_BLK)),
                lay((LANES, 2 * W_GRP)), lay((1, 2 * W_GRP)),
                lay((1, W_GRP)), pl.BlockSpec((W_GRP, W_GRP), lambda i: (0, 0))]
    args = [fold(p32), fold(p32), fold(p16), fold(p16), gate_up, gate_b, gn, seg, state]
    state_spec = pl.BlockSpec((n_seq, 1, 2, N_HEADS, HEAD_DIM, HEAD_DIM), lambda i: (i, layer, 0, 0, 0, 0))
    out_specs = [pl.BlockSpec((1, rows, W_GRP), lambda i: (i, 0, 0))]
    out_shape = [jax.ShapeDtypeStruct((b // n_seq, rows, W_GRP), F32)]
    if zero_init:
        in_specs.append(pl.BlockSpec(memory_space=pl.ANY))
        out_specs.append(state_spec)
        out_shape.append(jax.ShapeDtypeStruct(state.shape, state.dtype))
    else:
        in_specs.append(state_spec)
    o, *rest = pl.pallas_call(
        functools.partial(_gla_kernel, n_chunks=n_chunks, n_seq=n_seq, group=group, zero_init=zero_init),
        grid=(b // n_seq,),
        in_specs=in_specs,
        out_specs=out_specs,
        out_shape=out_shape,
        input_output_aliases={len(args) - 1: 1} if zero_init else {},
        scratch_shapes=[pltpu.VMEM((2, rows, W_GRP), BF16),
                        pltpu.VMEM((2, n_chunks, HEAD_DIM, W_GRP), F32),
                        pltpu.VMEM((2, max(n_chunks, 8), W_GRP), F32)],
        compiler_params=_cparams("arbitrary"),
        name="gla_zero" if zero_init else "gla",
    )(*args)
    return (o.reshape(b, s, W_GRP), *rest)


def _outproj_kernel(x_ref, mod_ref, oa_ref, oc_ref, od_ref, ga_ref, bb_ref, cb_ref, hb_ref, gb_ref,
                    gc_ref, gd_ref, cprev_ref, hprev_ref, cnext_ref, hnext_ref, cw_ref, w_ref, o_ref, *, seq_len):
    tm = x_ref.shape[0]
    f32 = lambda ref: ref[...].astype(F32)
    last = cprev_ref.shape[0] - 1
    u = f32(cb_ref) * f32(hb_ref)
    rid = lax.broadcasted_iota(jnp.int32, u.shape, 0)
    pos = (pl.program_id(0) * tm + rid) % seq_len
    u_prev = jnp.where(rid == 0, f32(cprev_ref)[last:, :] * f32(hprev_ref)[last:, :], pltpu.roll(u, 1, axis=0))
    u_next = jnp.where(rid == tm - 1, f32(cnext_ref)[0:1, :] * f32(hnext_ref)[0:1, :], pltpu.roll(u, tm - 1, axis=0))
    u_prev = jnp.where(pos == 0, 0.0, u_prev)
    u_next = jnp.where(pos == seq_len - 1, 0.0, u_next)
    cw = cw_ref[0]
    conv = u_prev * cw[0:1, :] + u * cw[1:2, :] + u_next * cw[2:3, :]
    y_a = oa_ref[...] * _silu(f32(ga_ref))
    y_b = f32(bb_ref) * conv * _silu(f32(gb_ref))
    y_c = oc_ref[...] * _silu(f32(gc_ref))
    y_d = od_ref[...] * _silu(f32(gd_ref))
    y = jnp.concatenate([y_a, y_b, y_c, y_d], axis=1).astype(BF16)
    gate = mod_ref[0, :, 2 * D_MODEL:3 * D_MODEL]
    o_ref[...] = x_ref[...] + gate * jnp.dot(y, w_ref[0], preferred_element_type=F32)


def _outproj(x, mod, layer, o_a, o_c, o_d, p, conv_w, w_out, *, tm):
    b, s, _ = x.shape
    n = b * s
    per_batch = mod.shape[0] > 1
    assert s % tm == 0 or (tm % s == 0 and not per_batch)
    tiles_per_seq = max(s // tm, 1)
    flat = lambda a: a.reshape(n, a.shape[-1])
    tile = lambda width: pl.BlockSpec((tm, width), lambda t: (t, 0))
    col = lambda g: pl.BlockSpec((tm, W_GRP), lambda t, g=g: (t, g))
    halo = 16
    th = tm // halo
    prev = lambda g: pl.BlockSpec((halo, W_GRP), lambda t, g=g: (jnp.maximum(t * th - 1, 0), g))
    nxt = lambda g: pl.BlockSpec((halo, W_GRP), lambda t, g=g: (jnp.minimum((t + 1) * th, n // halo - 1), g))
    out = pl.pallas_call(
        functools.partial(_outproj_kernel, seq_len=s),
        grid=(n // tm,),
        in_specs=[tile(D_MODEL),
                  pl.BlockSpec((1, 1, 3 * D_MODEL),
                               (lambda t: (t // tiles_per_seq, 0, 0)) if per_batch else (lambda t: (0, 0, 0))),
                  tile(W_GRP), tile(W_GRP), tile(W_GRP),
                  *[col(P16[g]) for g in (G_GA, G_BB, G_CB, G_HB, G_GB, G_GC, G_GD)],
                  prev(P16[G_CB]), prev(P16[G_HB]), nxt(P16[G_CB]), nxt(P16[G_HB]),
                  pl.BlockSpec((1, 3, W_GRP), lambda t: (layer, 0, 0)),
                  pl.BlockSpec((1, D_MODEL, D_MODEL), lambda t: (layer, 0, 0))],
        out_specs=tile(D_MODEL),
        out_shape=jax.ShapeDtypeStruct((n, D_MODEL), F32),
        compiler_params=_cparams("arbitrary"),
        name="outproj",
    )(flat(x), mod, flat(o_a), flat(o_c), flat(o_d), *([flat(p)] * 11), conv_w, w_out)
    return out.reshape(b, s, D_MODEL)


def kernel(x_prompt, x_sample, cache_k, cache_v, state_gla, c, c_ctx, norm_g, w_mod, b_mod, w_in, q_norm_g,
           k_norm_g, rpb, conv_w, gla_up_f, gla_bias_f, gla_up_b, gla_bias_b, gla_norm_g, w_out):
    dec_batch = c.shape[0]
    pad_rows = 8 - (1 + dec_batch) % 8
    cvec = jnp.concatenate([c_ctx[None], c, jnp.zeros((pad_rows, D_MODEL), F32)], axis=0)
    mod = _modulation(cvec, w_mod, b_mod)
    mod_ctx = mod[:, 0:1].reshape(DEPTH, 1, 1, 3 * D_MODEL)
    mod_smp = mod[:, 1:1 + dec_batch].reshape(DEPTH, dec_batch, 1, 3 * D_MODEL)

    w_in_b = jnp.pad(w_in, ((0, 0), (0, 0), (0, N_IN_PAD - N_IN))).astype(BF16)
    w_out_b = w_out.astype(BF16)
    seg = _seg_ones()
    norm_g3 = norm_g.reshape(DEPTH, 1, D_MODEL)
    qn = jnp.tile(q_norm_g, (1, N_HEADS)).reshape(DEPTH, 1, W_GRP)
    kn = jnp.tile(k_norm_g, (1, N_HEADS)).reshape(DEPTH, 1, W_GRP)
    gn = jnp.tile(gla_norm_g, (1, N_HEADS)).reshape(DEPTH, 1, W_GRP)
    gate_up = jnp.zeros((DEPTH, LANES, 2 * W_GRP), F32)
    gate_up = gate_up.at[:, :GATE_RANK, :W_GRP].set(gla_up_f).at[:, GATE_RANK:2 * GATE_RANK, W_GRP:].set(gla_up_b)
    gate_b = jnp.concatenate([gla_bias_f, gla_bias_b], axis=-1).reshape(DEPTH, 1, 2 * W_GRP)
    bias = _bias_tables(rpb)
    cache_k_b = cache_k.astype(BF16)
    cache_v_b = jnp.concatenate([cache_v.astype(BF16), jnp.ones(cache_v.shape, BF16)], axis=-1)

    x = x_prompt
    batch, seq, _ = x.shape
    new_cache_k = jnp.zeros((batch, DEPTH, N_HEADS, seq, HEAD_DIM), F32)
    new_cache_v = jnp.zeros((batch, DEPTH, N_HEADS, seq, HEAD_DIM), F32)
    new_state = jnp.zeros((batch, DEPTH, 2, N_HEADS, HEAD_DIM, HEAD_DIM), F32)
    for l in range(DEPTH):
        p32, p16, new_cache_k, new_cache_v = _inproj(x, mod_ctx[l], l, norm_g3, w_in_b, qn, kn, seg,
                                                     tm=TOKEN_TILE, caches=(new_cache_k, new_cache_v))
        o_a = _attn_ctx(p16)
        o_c = _fft_ctx(p32)
        o_d, new_state = _gla(p32, p16, gate_up, gate_b, gn, seg, l, new_state, zero_init=True)
        x = _outproj(x, mod_ctx[l], l, o_a, o_c, o_d, p16, conv_w, w_out_b, tm=OUTPROJ_TILE)
    y_prompt = x

    xs = x_sample
    for l in range(DEPTH):
        p32, p16, k_hm, v_hm = _inproj(xs, mod_smp[l], l, norm_g3, w_in_b, qn, kn, seg, tm=TOKEN_TILE)
        o_a = _attn_nbr(p16, k_hm, v_hm, cache_k_b, cache_v_b, bias, l)
        o_c = _fft_grid(p32)
        (o_d,) = _gla(p32, p16, gate_up, gate_b, gn, seg, l, state_gla, zero_init=False)
        xs = _outproj(xs, mod_smp[l], l, o_a, o_c, o_d, p16, conv_w, w_out_b, tm=OUTPROJ_TILE)
    return (y_prompt, xs, new_cache_k, new_cache_v, new_state)
```

```python
import functools
import math

import numpy as np
import jax
import jax.numpy as jnp
from jax import lax
from jax.experimental import pallas as pl
from jax.experimental.pallas import tpu as pltpu

F32 = jnp.float32
BF16 = jnp.bfloat16

D_MODEL = 1024
DEPTH = 4
GRID_W = 64
W_GRP = 256
HEAD_DIM = 64
N_HEADS = 4
WIN_H = 8
WIN_W = 16
N_FOURIER = 4
GATE_RANK = 16
GLA_TAU = 16.0
CHUNK = 64
RMS_EPS = 1e-6
N_IN = 14 * W_GRP + 2 * GATE_RANK
LANES = 128
N_IN_PAD = -(-N_IN // LANES) * LANES
G_QA, G_KA, G_VA, G_GA, G_BB, G_CB, G_HB, G_GB, G_UC, G_GC, G_QD, G_KD, G_VD, G_GD = range(14)
F32_GROUPS = (G_UC, G_QD, G_KD)
BF16_GROUPS = (G_QA, G_KA, G_VA, G_VD, G_GA, G_BB, G_CB, G_HB, G_GB, G_GC, G_GD)
P32 = {g: i for i, g in enumerate(F32_GROUPS)}
P16 = {g: i for i, g in enumerate(BF16_GROUPS)}
N_P32 = len(F32_GROUPS) * W_GRP
N_P16 = len(BF16_GROUPS) * W_GRP + LANES
GATE_BLK = len(BF16_GROUPS) * W_GRP // LANES
NEG_BIG = -1e30
VMEM_LIMIT = 56 * 1024 * 1024
TOKEN_TILE = 512
OUTPROJ_TILE = 1024


def _cparams(*sem):
    return pltpu.CompilerParams(dimension_semantics=sem, vmem_limit_bytes=VMEM_LIMIT)


def _bdot(a, b):
    return jnp.dot(a.astype(BF16), b.astype(BF16), preferred_element_type=F32)


def _bdot_nt(a, b):
    return lax.dot_general(a.astype(BF16), b.astype(BF16), (((1,), (1,)), ((), ())),
                           preferred_element_type=F32)


def _split2(x):
    hi = x.astype(BF16)
    lo = (x - hi.astype(F32)).astype(BF16)
    return hi, lo


def _split3(x):
    h1 = x.astype(BF16)
    r = x - h1.astype(F32)
    h2 = r.astype(BF16)
    h3 = (r - h2.astype(F32)).astype(BF16)
    return h1, h2, h3


def _silu(x):
    return x / (1.0 + jnp.exp(-x))


def _head_rms(t, seg, g):
    n = t.shape[0]
    both = jnp.dot(jnp.concatenate(_split2(t * t), axis=0), seg, preferred_element_type=F32)
    ss = both[:n] + both[n:]
    return t * lax.rsqrt(ss * (1.0 / HEAD_DIM) + RMS_EPS) * g


def _dft_cos_sin(n):
    k = np.arange(n)
    ang = 2.0 * np.pi * ((k[:, None] * k[None, :]) % n) / n
    return np.cos(ang), np.sin(ang)


def _block_diag(m, reps):
    n = m.shape[0]
    out = np.zeros((n * reps, n * reps), m.dtype)
    for i in range(reps):
        out[i * n:(i + 1) * n, i * n:(i + 1) * n] = m
    return out


def _seg_ones():
    return jnp.asarray(_block_diag(np.ones((HEAD_DIM, HEAD_DIM)), N_HEADS), BF16)


def _mod_kernel(c_ref, w_ref, b_ref, o_ref):
    o_ref[0] = _bdot(_silu(c_ref[...]), w_ref[0]) + b_ref[0]


def _modulation(cvec, w_mod, b_mod):
    rows = cvec.shape[0]
    return pl.pallas_call(
        _mod_kernel,
        grid=(DEPTH, 3),
        in_specs=[pl.BlockSpec((rows, D_MODEL), lambda l, j: (0, 0)),
                  pl.BlockSpec((1, D_MODEL, D_MODEL), lambda l, j: (l, 0, j)),
                  pl.BlockSpec((1, 1, D_MODEL), lambda l, j: (l, 0, j))],
        out_specs=pl.BlockSpec((1, rows, D_MODEL), lambda l, j: (l, 0, j)),
        out_shape=jax.ShapeDtypeStruct((DEPTH, rows, 3 * D_MODEL), F32),
        compiler_params=_cparams("arbitrary", "arbitrary"),
        name="adaln_mod",
    )(cvec, w_mod, b_mod.reshape(DEPTH, 1, 3 * D_MODEL))


INPROJ_ORDER = (2, 0, 3, 1, 4, 5, 6, 7, 8, 9, 10, 11, 12, 13)
INPROJ_PIECES = 8


def _inproj_kernel(x_ref, mod_ref, g_ref, w_ref, wg_ref, qn_ref, kn_ref, seg_ref, *rest,
                   for_window_attn, n_seq):
    p32_ref, p16_ref, k_ref, v_ref, h_even, h_odd = rest[-6:]
    t = pl.program_id(0)
    rows = p32_ref.shape[0] // n_seq

    def normed(x_ref, mod_ref, rows=slice(None)):
        x = x_ref[rows, :]
        ms = jnp.mean(x * x, axis=-1, keepdims=True)
        y = x * lax.rsqrt(ms + RMS_EPS) * g_ref[0]
        shift = mod_ref[0, :, 0:D_MODEL]
        scale = mod_ref[0, :, D_MODEL:2 * D_MODEL]
        return (y * (1.0 + scale) + shift).astype(BF16)

    def store_heads(ref, val, ones):
        for q in range(n_seq):
            for h in range(N_HEADS):
                th = val[q * rows:(q + 1) * rows, h * HEAD_DIM:(h + 1) * HEAD_DIM]
                if ones:
                    th = jnp.concatenate([th, jnp.ones_like(th)], axis=1)
                if for_window_attn:
                    ref[0, h] = th.astype(ref.dtype)
                else:
                    ref[q, 0, h] = th

    def step(h_cur, h_nxt):
        seg = seg_ref[...]
        piece = p32_ref.shape[0] // INPROJ_PIECES
        first_late = len(INPROJ_ORDER) - INPROJ_PIECES
        for idx, j in enumerate(INPROJ_ORDER):
            if idx >= first_late:
                part = slice((idx - first_late) * piece, (idx - first_late + 1) * piece)
                h_nxt[part, :] = normed(x_ref, mod_ref, part)
            cs = slice(j * W_GRP, (j + 1) * W_GRP)
            y = jnp.dot(h_cur[...], w_ref[0, :, cs], preferred_element_type=F32)
            if j == G_QA:
                y = _head_rms(y, seg, qn_ref[0]) * (HEAD_DIM ** -0.5)
            elif j == G_KA:
                y = _head_rms(y, seg, kn_ref[0])
            if j in P32:
                p32_ref[:, P32[j] * W_GRP:(P32[j] + 1) * W_GRP] = y
            else:
                p16_ref[:, P16[j] * W_GRP:(P16[j] + 1) * W_GRP] = y.astype(BF16)
            if j == G_KA:
                store_heads(k_ref, y, False)
            elif j == G_VA:
                store_heads(v_ref, y, for_window_attn)
        p16_ref[:, GATE_BLK * LANES:N_P16] = jnp.dot(h_cur[...], wg_ref[0], preferred_element_type=F32).astype(BF16)

    @pl.when(t == 0)
    def _():
        h_even[...] = normed(x_ref, mod_ref)
        for ref in (p32_ref, p16_ref, k_ref, v_ref):
            ref[...] = jnp.zeros(ref.shape, ref.dtype)

    @pl.when(t % 2 == 1)
    def _():
        step(h_even, h_odd)

    @pl.when(jnp.logical_and(t > 0, t % 2 == 0))
    def _():
        step(h_odd, h_even)


def _inproj(x, mod, layer, norm_g, w_in, qn, kn, seg, *, tm, caches=None):
    b, s, _ = x.shape
    for_window_attn = caches is None
    assert s % tm == 0 if for_window_attn else tm % s == 0
    n_tiles = b * s // tm
    tiles_per_seq = max(s // tm, 1)
    n_seq = max(tm // s, 1)
    per_batch = mod.shape[0] > 1
    prep = lambda t: jnp.minimum(t, n_tiles - 1)
    done = lambda t: jnp.maximum(t - 1, 0)
    lay = lambda *dims: pl.BlockSpec((1,) + dims, lambda t: (layer,) + (0,) * len(dims))
    in_specs = [pl.BlockSpec((tm, D_MODEL), lambda t: (prep(t), 0)),
                pl.BlockSpec((1, 1, 3 * D_MODEL),
                             (lambda t: (prep(t) // tiles_per_seq, 0, 0)) if per_batch else (lambda t: (0, 0, 0))),
                lay(1, D_MODEL),
                pl.BlockSpec((1, D_MODEL, 14 * W_GRP), lambda t: (layer, 0, 0)),
                pl.BlockSpec((1, D_MODEL, LANES), lambda t: (layer, 0, 14 * W_GRP // LANES)),
                lay(1, W_GRP), lay(1, W_GRP),
                pl.BlockSpec((W_GRP, W_GRP), lambda t: (0, 0))]
    x2 = x.reshape(b * s, D_MODEL)
    args = [x2, mod, norm_g, w_in, w_in, qn, kn, seg]
    out_shape = [jax.ShapeDtypeStruct((b * s, N_P32), F32), jax.ShapeDtypeStruct((b * s, N_P16), BF16)]
    out_specs = [pl.BlockSpec((tm, N_P32), lambda t: (done(t), 0)), pl.BlockSpec((tm, N_P16), lambda t: (done(t), 0))]
    aliases = {}
    if for_window_attn:
        for width in (HEAD_DIM, 2 * HEAD_DIM):
            out_shape.append(jax.ShapeDtypeStruct((b, N_HEADS, s, width), BF16))
            out_specs.append(pl.BlockSpec((1, N_HEADS, tm, width),
                                          lambda t: (done(t) // tiles_per_seq, 0, done(t) % tiles_per_seq, 0)))
    else:
        for i, cache in enumerate(caches):
            aliases[len(args)] = 2 + i
            args.append(cache)
            in_specs.append(pl.BlockSpec(memory_space=pl.ANY))
            out_shape.append(jax.ShapeDtypeStruct(cache.shape, cache.dtype))
            out_specs.append(pl.BlockSpec((n_seq, 1, N_HEADS, s, HEAD_DIM), lambda t: (done(t), layer, 0, 0, 0)))
    p32, p16, k, v = pl.pallas_call(
        functools.partial(_inproj_kernel, for_window_attn=for_window_attn, n_seq=n_seq),
        grid=(n_tiles + 1,),
        in_specs=in_specs,
        out_specs=out_specs,
        out_shape=out_shape,
        input_output_aliases=aliases,
        scratch_shapes=[pltpu.VMEM((tm, D_MODEL), BF16)] * 2,
        compiler_params=_cparams("arbitrary"),
        name="inproj_grid" if for_window_attn else "inproj_ctx",
    )(*args)
    return p32.reshape(b, s, N_P32), p16.reshape(b, s, N_P16), k, v


CTX_SEQS_PER_STEP = 2


def _attn_ctx_kernel(q_ref, k_ref, v_ref, o_ref):
    n_seq, s, _ = q_ref.shape
    heads = [slice(h * HEAD_DIM, (h + 1) * HEAD_DIM) for h in range(N_HEADS)]
    units = [(b, h) for b in range(n_seq) for h in range(N_HEADS)]
    low_half = lax.broadcasted_iota(jnp.int32, (s, 2 * HEAD_DIM), 1) < HEAD_DIM
    ones = jnp.ones((s, HEAD_DIM), BF16)
    q, k, v = ([ref[b] for b in range(n_seq)] for ref in (q_ref, k_ref, v_ref))
    scores = [_bdot_nt(q[b][:, heads[h]], k[b][:, heads[h]]) for b, h in units]
    e = [jnp.exp(x - jnp.max(x, axis=-1, keepdims=True)).astype(BF16) for x in scores]
    res = [jnp.dot(e[u], jnp.concatenate([v[b][:, heads[h]], ones], axis=1), preferred_element_type=F32)
           for u, (b, h) in enumerate(units)]
    for b in range(n_seq):
        pairs = []
        for h in range(0, N_HEADS, 2):
            even, odd = res[b * N_HEADS + h], res[b * N_HEADS + h + 1]
            pairs.append(jnp.where(low_half, even / pltpu.roll(even, HEAD_DIM, axis=1),
                                   pltpu.roll(odd, HEAD_DIM, axis=1) / odd))
        o_ref[b] = jnp.concatenate(pairs, axis=1)


def _attn_ctx(p):
    b, s, _ = p.shape
    n_seq = CTX_SEQS_PER_STEP
    assert b % n_seq == 0
    col = lambda g: pl.BlockSpec((n_seq, s, W_GRP), lambda i, g=g: (i, 0, g))
    return pl.pallas_call(
        _attn_ctx_kernel,
        grid=(b // n_seq,),
        in_specs=[col(P16[G_QA]), col(P16[G_KA]), col(P16[G_VA])],
        out_specs=pl.BlockSpec((n_seq, s, W_GRP), lambda i: (i, 0, 0)),
        out_shape=jax.ShapeDtypeStruct((b, s, W_GRP), F32),
        compiler_params=_cparams("arbitrary"),
        name="attn_ctx",
    )(p, p, p)


def _bias_kernel(rpb_ref, o_ref):
    lh = pl.program_id(0)
    q = lax.broadcasted_iota(jnp.int32, (GRID_W, GRID_W), 0)
    kc = lax.broadcasted_iota(jnp.int32, (GRID_W, GRID_W), 1)
    dc = jnp.clip(kc - q + (WIN_W - 1), 0, 2 * WIN_W - 2)
    cs = jnp.clip(q - WIN_W // 2, 0, GRID_W - WIN_W)
    in_win = jnp.where(kc >= cs, jnp.where(kc < cs + WIN_W, 1, 0), 0) == 1
    n_dr, n_dc = 2 * WIN_H - 1, 2 * WIN_W - 1
    tiles = []
    for dr in range(n_dr):
        base = (lh * n_dr + dr) * n_dc
        t = jnp.zeros((GRID_W, GRID_W), F32)
        for d in range(n_dc):
            t = jnp.where(dc == d, rpb_ref[base + d], t)
        tiles.append(jnp.where(in_win, t, NEG_BIG))
    for dl in range(WIN_H):
        for i in range(WIN_H):
            o_ref[0, dl, :, i * GRID_W:(i + 1) * GRID_W] = tiles[dl + i]


def _bias_tables(rpb):
    out = pl.pallas_call(
        _bias_kernel,
        grid=(DEPTH * N_HEADS,),
        in_specs=[pl.BlockSpec(memory_space=pltpu.SMEM)],
        out_specs=pl.BlockSpec((1, WIN_H, GRID_W, WIN_H * GRID_W), lambda i: (i, 0, 0, 0)),
        out_shape=jax.ShapeDtypeStruct((DEPTH * N_HEADS, WIN_H, GRID_W, WIN_H * GRID_W), F32),
        compiler_params=_cparams("arbitrary"),
        name="rpb_tables",
    )(rpb.reshape(-1))
    return out.reshape(DEPTH, N_HEADS, WIN_H, GRID_W, WIN_H * GRID_W)


ATTN_ROW_GROUP = 8


def _attn_nbr_kernel(q_ref, k_ref, v_ref, ck_ref, cv_ref, bias_ref, o_ref, *, rows_per_step):
    rb = pl.program_id(1)
    n_rows = k_ref.shape[2] // GRID_W
    kh = min(WIN_H, n_rows)
    n_lat = kh * GRID_W
    heads = [slice(h * HEAD_DIM, (h + 1) * HEAD_DIM) for h in range(N_HEADS)]
    units = [(g, h) for g in range(ATTN_ROW_GROUP) for h in range(N_HEADS)]
    low_half = lax.broadcasted_iota(jnp.int32, (GRID_W, 2 * HEAD_DIM), 1) < HEAD_DIM

    def rows_body(jj, carry):
        q0, k0, dl, q = [], [], [], []
        for g in range(ATTN_ROW_GROUP):
            j = jj * ATTN_ROW_GROUP + g
            r = rb * rows_per_step + j
            rs = jnp.clip(r - kh // 2, 0, n_rows - kh)
            dl.append(rs - r + (WIN_H - 1))
            k0.append(pl.multiple_of(rs * GRID_W, GRID_W))
            q0.append(pl.multiple_of(j * GRID_W, GRID_W))
            q.append(q_ref[0, pl.ds(q0[g], GRID_W), :].astype(BF16))
        s_lat = [_bdot_nt(q[g][:, heads[h]], k_ref[0, h, pl.ds(k0[g], n_lat), :]) + bias_ref[0, h, dl[g]]
                 for g, h in units]
        q_all = jnp.concatenate(q, axis=0)
        s_ctx_all = [_bdot_nt(q_all[:, hs], ck_ref[0, 0, h]) for h, hs in enumerate(heads)]
        s_ctx = [s_ctx_all[h][g * GRID_W:(g + 1) * GRID_W] for g, h in units]
        m = [jnp.maximum(jnp.max(a, axis=-1, keepdims=True), jnp.max(c, axis=-1, keepdims=True))
             for a, c in zip(s_lat, s_ctx)]
        e_lat = [jnp.exp(a - mu).astype(BF16) for a, mu in zip(s_lat, m)]
        e_ctx = [jnp.exp(c - mu).astype(BF16) for c, mu in zip(s_ctx, m)]
        pv_ctx = [jnp.dot(jnp.concatenate([e_ctx[g * N_HEADS + h] for g in range(ATTN_ROW_GROUP)], axis=0),
                          cv_ref[0, 0, h], preferred_element_type=F32) for h in range(N_HEADS)]
        res = [jnp.dot(e_lat[u], v_ref[0, h, pl.ds(k0[g], n_lat), :], preferred_element_type=F32)
               + pv_ctx[h][g * GRID_W:(g + 1) * GRID_W]
               for u, (g, h) in enumerate(units)]
        for g in range(ATTN_ROW_GROUP):
            pairs = []
            for h in range(0, N_HEADS, 2):
                even, odd = res[g * N_HEADS + h], res[g * N_HEADS + h + 1]
                pairs.append(jnp.where(low_half, even / pltpu.roll(even, HEAD_DIM, axis=1),
                                       pltpu.roll(odd, HEAD_DIM, axis=1) / odd))
            o_ref[0, pl.ds(q0[g], GRID_W), :] = jnp.concatenate(pairs, axis=1)
        return carry

    lax.fori_loop(0, rows_per_step // ATTN_ROW_GROUP, rows_body, 0)


def _attn_nbr(p, k_hm, v_hm, cache_k, cache_v, bias, layer, *, rows_per_step=32):
    b, s, _ = p.shape
    tq = rows_per_step * GRID_W
    past = cache_v.shape[3]
    ctx_spec = lambda *dims: pl.BlockSpec((1, 1, N_HEADS) + dims, lambda i, j: (i, layer, 0, 0, 0))
    return pl.pallas_call(
        functools.partial(_attn_nbr_kernel, rows_per_step=rows_per_step),
        grid=(b, s // tq),
        in_specs=[pl.BlockSpec((1, tq, W_GRP), lambda i, j: (i, j, P16[G_QA])),
                  pl.BlockSpec((1, N_HEADS, s, HEAD_DIM), lambda i, j: (i, 0, 0, 0)),
                  pl.BlockSpec((1, N_HEADS, s, 2 * HEAD_DIM), lambda i, j: (i, 0, 0, 0)),
                  ctx_spec(past, HEAD_DIM), ctx_spec(past, 2 * HEAD_DIM),
                  pl.BlockSpec((1, N_HEADS, WIN_H, GRID_W, WIN_H * GRID_W), lambda i, j: (layer, 0, 0, 0, 0))],
        out_specs=pl.BlockSpec((1, tq, W_GRP), lambda i, j: (i, j, 0)),
        out_shape=jax.ShapeDtypeStruct((b, s, W_GRP), F32),
        compiler_params=_cparams("arbitrary", "arbitrary"),
        name="attn_nbr",
    )(p, k_hm, v_hm, cache_k, cache_v, bias)


def _fft_ctx_consts(s):
    c, sn = _dft_cos_sin(s)
    cc, sc = _dft_cos_sin(W_GRP // N_FOURIER)
    m1 = np.concatenate([c, -sn], axis=0)
    m2 = np.concatenate([_block_diag(cc, N_FOURIER), _block_diag(sc, N_FOURIER)], axis=0)
    return jnp.asarray(m1, F32), jnp.asarray(m2, F32)


def _fft_ctx_kernel(u_ref, m1_ref, m2_ref, o_ref, *, scale):
    n_seq, s, _ = u_ref.shape
    m1 = m1_ref[...].astype(BF16)
    t = [_bdot(m1, u_ref[b]) for b in range(n_seq)]
    x = jnp.concatenate([jnp.concatenate([tb[:s], tb[s:]], axis=1) for tb in t], axis=0)
    out = _bdot(x, m2_ref[...]) * scale
    for b in range(n_seq):
        o_ref[b] = out[b * s:(b + 1) * s]


def _fft_ctx(p):
    b, s, _ = p.shape
    n_seq = CTX_SEQS_PER_STEP
    assert b % n_seq == 0
    consts = _fft_ctx_consts(s)
    full = lambda a: pl.BlockSpec(a.shape, lambda i: (0,) * a.ndim)
    return pl.pallas_call(
        functools.partial(_fft_ctx_kernel, scale=1.0 / math.sqrt(s * (W_GRP // N_FOURIER))),
        grid=(b // n_seq,),
        in_specs=[pl.BlockSpec((n_seq, s, W_GRP), lambda i: (i, 0, P32[G_UC]))] + [full(a) for a in consts],
        out_specs=pl.BlockSpec((n_seq, s, W_GRP), lambda i: (i, 0, 0)),
        out_shape=jax.ShapeDtypeStruct((b, s, W_GRP), F32),
        compiler_params=_cparams("arbitrary"),
        name="fft_ctx",
    )(p, *consts)


def _fft_grid_consts():
    n = GRID_W
    c, sn = _dft_cos_sin(n)
    cc, sc = _dft_cos_sin(W_GRP // N_FOURIER)
    m1 = np.concatenate([c, -sn], axis=0)
    m2 = np.block([[c, sn], [-sn, c]])
    reps = LANES // (W_GRP // N_FOURIER)
    m3 = np.concatenate([_block_diag(cc, reps), _block_diag(sc, reps)], axis=0)
    k = np.arange(n)
    ang = 2.0 * np.pi * (k[:, None] * k[None, :]) / (n * n)
    twr = np.broadcast_to(np.cos(ang)[:, :, None], (n, n, LANES))
    twi = np.broadcast_to(-np.sin(ang)[:, :, None], (n, n, LANES))
    return tuple(jnp.asarray(m, F32) for m in (m1, m2, m3, twr, twi))


FFT_UNROLL = 8
FFT_PITCH = 72


def _fft_grid_kernel(u_ref, m1_ref, m2_ref, m3_ref, twr_ref, twi_ref, o_ref, tr_scr, ti_scr, *, scale):
    n = GRID_W
    m1, m2, m3 = (r[...].astype(BF16) for r in (m1_ref, m2_ref, m3_ref))

    def stage1(b, carry):
        a_rows = u_ref[0, pl.ds(b, n, stride=n), :]
        t = _bdot(m1, a_rows)
        tr, ti = t[:n], t[n:]
        wr, wi = twr_ref[b], twi_ref[b]
        tr_scr[pl.ds(b, n, stride=FFT_PITCH), :] = tr * wr - ti * wi
        ti_scr[pl.ds(b, n, stride=FFT_PITCH), :] = tr * wi + ti * wr
        return carry

    lax.fori_loop(0, n, stage1, 0, unroll=FFT_UNROLL)

    def stage2(dd, carry):
        xcs = []
        for u in range(FFT_UNROLL):
            r0 = pl.multiple_of((dd * FFT_UNROLL + u) * FFT_PITCH, 8)
            g = jnp.concatenate([tr_scr[pl.ds(r0, n), :], ti_scr[pl.ds(r0, n), :]], axis=0)
            x = _bdot(m2, g)
            xcs.append(jnp.concatenate([x[:n], x[n:]], axis=1))
        out = _bdot(jnp.concatenate(xcs, axis=0), m3) * scale
        for u in range(FFT_UNROLL):
            o_ref[0, pl.ds(dd * FFT_UNROLL + u, n, stride=n), :] = out[u * n:(u + 1) * n]
        return carry

    lax.fori_loop(0, n // FFT_UNROLL, stage2, 0)


def _fft_grid(p):
    b, s, _ = p.shape
    assert s == GRID_W * GRID_W
    consts = _fft_grid_consts()
    full = lambda a: pl.BlockSpec(a.shape, lambda i, j: (0,) * a.ndim)
    blk0 = P32[G_UC] * W_GRP // LANES
    return pl.pallas_call(
        functools.partial(_fft_grid_kernel, scale=1.0 / math.sqrt(s * (W_GRP // N_FOURIER))),
        grid=(b, W_GRP // LANES),
        in_specs=[pl.BlockSpec((1, s, LANES), lambda i, j: (i, 0, blk0 + j))] + [full(a) for a in consts],
        out_specs=pl.BlockSpec((1, s, LANES), lambda i, j: (i, 0, j)),
        out_shape=jax.ShapeDtypeStruct((b, s, W_GRP), F32),
        scratch_shapes=[pltpu.VMEM((GRID_W * FFT_PITCH, LANES), F32)] * 2,
        compiler_params=_cparams("arbitrary", "arbitrary"),
        name="fft_grid",
    )(p, *consts)


GLA_CHUNKS_PER_TRIP = 8


def _gla_kernel(*refs, n_chunks, n_seq, group, zero_init):
    if zero_init:
        (q_ref, k_ref, v_ref, a_ref, up_ref, gb_ref, gn_ref, seg_ref, _,
         o_ref, sfin_ref, qd_scr, st_scr, dec_scr) = refs
        s0_ref = None
    else:
        (q_ref, k_ref, v_ref, a_ref, up_ref, gb_ref, gn_ref, seg_ref, s0_ref,
         o_ref, qd_scr, st_scr, dec_scr) = refs
        sfin_ref = None
    c = CHUNK
    row = lax.broadcasted_iota(jnp.int32, (c, c), 0)
    col = lax.broadcasted_iota(jnp.int32, (c, c), 1)
    heads = [slice(h * HEAD_DIM, (h + 1) * HEAD_DIM) for h in range(N_HEADS)]
    wrow = lax.broadcasted_iota(jnp.int32, (W_GRP, W_GRP), 0)
    wcol = lax.broadcasted_iota(jnp.int32, (W_GRP, W_GRP), 1)
    head_diag = (wrow // HEAD_DIM) == (wcol // HEAD_DIM)

    def blockdiag(x):
        return jnp.where(head_diag, jnp.concatenate([x] * N_HEADS, axis=0), jnp.zeros((), x.dtype))

    trow = lax.broadcasted_iota(jnp.int32, (c, W_GRP), 0)
    tcol = lax.broadcasted_iota(jnp.int32, (c, W_GRP), 1) % c
    keep = [trow >= tcol, trow <= tcol]
    eye_tiled = jnp.where(trow == tcol, 1.0, 0.0).astype(BF16)

    def head_transpose_exact(x):
        return sum(_bdot_nt(eye_tiled, blockdiag(part)) for part in _split3(x))

    sum_ops = [jnp.where(row >= col, 1.0, 0.0).astype(BF16),
               jnp.where(row <= col, 1.0, 0.0).astype(BF16)]
    dirs = (0, 1)

    def chunk_rows(n):
        return pl.ds(pl.multiple_of(n * c, c), c)

    def phase1(i, carry):
        ns = [i * group + g for g in range(group)]
        rows = [chunk_rows(n) for n in ns]
        x_all = _bdot(jnp.concatenate([a_ref[0, r, :] for r in rows], axis=0), up_ref[0]) + gb_ref[0]
        x = [[x_all[g * c:(g + 1) * c, d * W_GRP:(d + 1) * W_GRP] for d in dirs] for g in range(group)]
        la = [[(jnp.minimum(xd, 0.0) - jnp.log(1.0 + jnp.exp(-jnp.abs(xd)))) * (1.0 / GLA_TAU) for xd in xg]
              for xg in x]
        cum = [[sum(jnp.dot(sum_ops[d], part, preferred_element_type=F32) for part in _split3(lg[d]))
                for d in dirs] for lg in la]
        tot_row = [[cg[0][c - 1:c, :], cg[1][0:1, :]] for cg in cum]
        q = [q_ref[0, r, :] * (HEAD_DIM ** -0.5) for r in rows]
        k = [k_ref[0, r, :] for r in rows]
        v_bd = [blockdiag(v_ref[0, r, :].astype(BF16)) for r in rows]
        q_dec = [[(q[g] * jnp.exp(cum[g][d])).astype(BF16) for d in dirs] for g in range(group)]
        k_inv = [[blockdiag((k[g] * jnp.exp(-cum[g][d])).astype(BF16)) for d in dirs] for g in range(group)]
        k_end = [[blockdiag((k[g] * jnp.exp(tot_row[g][d] - cum[g][d])).astype(BF16)) for d in dirs]
                 for g in range(group)]
        scores = [[_bdot_nt(q_dec[g][d], k_inv[g][d]) for d in dirs] for g in range(group)]
        v_t = [_bdot_nt(eye_tiled, vb).astype(BF16) for vb in v_bd]
        upd = [[_bdot(v_t[g], k_end[g][d]) for d in dirs] for g in range(group)]
        att = [[jnp.where(keep[d], scores[g][d], 0.0).astype(BF16) for d in dirs] for g in range(group)]
        for g in range(group):
            both = _bdot(jnp.concatenate(att[g], axis=0), v_bd[g])
            o_ref[0, rows[g], :] = both[:c] + both[c:]
            for d in dirs:
                qd_scr[d, rows[g], :] = q_dec[g][d]
                dec_scr[d, pl.ds(ns[g], 1), :] = jnp.exp(tot_row[g][d])
                st_scr[d, ns[g]] = upd[g][d]
        return carry

    lax.fori_loop(0, n_chunks // group, phase1, 0)

    def init_state(d):
        if zero_init:
            return jnp.zeros((HEAD_DIM, W_GRP), F32)
        return head_transpose_exact(jnp.concatenate([s0_ref[0, 0, d, h] for h in range(N_HEADS)], axis=1))

    per_seq = n_chunks // n_seq
    finals = []
    for q in range(n_seq):
        finals.append([])
        for d in range(2):
            def scan_body(i, st, d=d, q=q):
                n = q * per_seq + (i if d == 0 else per_seq - 1 - i)
                upd = st_scr[d, n]
                st_scr[d, n] = st
                return st * dec_scr[d, pl.ds(n, 1), :] + upd
            finals[q].append(lax.fori_loop(0, per_seq, scan_body, init_state(d)))

    def phase3(i, carry):
        ns = [i * group + g for g in range(group)]
        rows = [chunk_rows(n) for n in ns]
        st_bd = [[blockdiag(st_scr[d, n].astype(BF16)) for d in dirs] for n in ns]
        inter = [[_bdot_nt(qd_scr[d, rows[g], :], st_bd[g][d]) for d in dirs] for g in range(group)]
        o = jnp.concatenate([o_ref[0, rows[g], :] + inter[g][0] + inter[g][1] for g in range(group)], axis=0)
        o = _head_rms(o, seg_ref[...], gn_ref[0])
        for g in range(group):
            o_ref[0, rows[g], :] = o[g * c:(g + 1) * c]
        return carry

    lax.fori_loop(0, n_chunks // group, phase3, 0)
    if sfin_ref is not None:
        for q in range(n_seq):
            for d in range(2):
                s_fin = head_transpose_exact(finals[q][d])
                for h, hs in enumerate(heads):
                    sfin_ref[q, 0, d, h] = s_fin[:, hs]


def _gla(p32, p16, gate_up, gate_b, gn, seg, layer, state, *, zero_init):
    b, s, _ = p32.shape
    n_seq = max(1, min(b, GLA_CHUNKS_PER_TRIP * CHUNK // s)) if zero_init else 1
    assert b % n_seq == 0
    rows = n_seq * s
    n_chunks = rows // CHUNK
    group = min(GLA_CHUNKS_PER_TRIP, n_chunks)
    assert n_chunks % group == 0
    fold = lambda a: a.reshape(b // n_seq, rows, a.shape[-1])
    col = lambda g: pl.BlockSpec((1, rows, W_GRP), lambda i, g=g: (i, 0, g))
    lay = lambda shape: pl.BlockSpec((1,) + shape, lambda i: (layer,) + (0,) * len(shape))
    in_specs = [col(P32[G_QD]), col(P32[G_KD]), col(P16[G_VD]),
                pl.BlockSpec((1, rows, LANES), lambda i: (i, 0, GATE_BLK)),
                lay((LANES, 2 * W_GRP)), lay((1, 2 * W_GRP)),
                lay((1, W_GRP)), pl.BlockSpec((W_GRP, W_GRP), lambda i: (0, 0))]
    args = [fold(p32), fold(p32), fold(p16), fold(p16), gate_up, gate_b, gn, seg, state]
    state_spec = pl.BlockSpec((n_seq, 1, 2, N_HEADS, HEAD_DIM, HEAD_DIM), lambda i: (i, layer, 0, 0, 0, 0))
    out_specs = [pl.BlockSpec((1, rows, W_GRP), lambda i: (i, 0, 0))]
    out_shape = [jax.ShapeDtypeStruct((b // n_seq, rows, W_GRP), F32)]
    if zero_init:
        in_specs.append(pl.BlockSpec(memory_space=pl.ANY))
        out_specs.append(state_spec)
        out_shape.append(jax.ShapeDtypeStruct(state.shape, state.dtype))
    else:
        in_specs.append(state_spec)
    o, *rest = pl.pallas_call(
        functools.partial(_gla_kernel, n_chunks=n_chunks, n_seq=n_seq, group=group, zero_init=zero_init),
        grid=(b // n_seq,),
        in_specs=in_specs,
        out_specs=out_specs,
        out_shape=out_shape,
        input_output_aliases={len(args) - 1: 1} if zero_init else {},
        scratch_shapes=[pltpu.VMEM((2, rows, W_GRP), BF16),
                        pltpu.VMEM((2, n_chunks, HEAD_DIM, W_GRP), F32),
                        pltpu.VMEM((2, max(n_chunks, 8), W_GRP), F32)],
        compiler_params=_cparams("arbitrary"),
        name="gla_zero" if zero_init else "gla",
    )(*args)
    return (o.reshape(b, s, W_GRP), *rest)


def _outproj_kernel(x_ref, mod_ref, oa_ref, oc_ref, od_ref, ga_ref, bb_ref, cb_ref, hb_ref, gb_ref,
                    gc_ref, gd_ref, cprev_ref, hprev_ref, cnext_ref, hnext_ref, cw_ref, w_ref, o_ref, *, seq_len):
    tm = x_ref.shape[0]
    f32 = lambda ref: ref[...].astype(F32)
    last = cprev_ref.shape[0] - 1
    u = f32(cb_ref) * f32(hb_ref)
    rid = lax.broadcasted_iota(jnp.int32, u.shape, 0)
    pos = (pl.program_id(0) * tm + rid) % seq_len
    u_prev = jnp.where(rid == 0, f32(cprev_ref)[last:, :] * f32(hprev_ref)[last:, :], pltpu.roll(u, 1, axis=0))
    u_next = jnp.where(rid == tm - 1, f32(cnext_ref)[0:1, :] * f32(hnext_ref)[0:1, :], pltpu.roll(u, tm - 1, axis=0))
    u_prev = jnp.where(pos == 0, 0.0, u_prev)
    u_next = jnp.where(pos == seq_len - 1, 0.0, u_next)
    cw = cw_ref[0]
    conv = u_prev * cw[0:1, :] + u * cw[1:2, :] + u_next * cw[2:3, :]
    y_a = oa_ref[...] * _silu(f32(ga_ref))
    y_b = f32(bb_ref) * conv * _silu(f32(gb_ref))
    y_c = oc_ref[...] * _silu(f32(gc_ref))
    y_d = od_ref[...] * _silu(f32(gd_ref))
    y = jnp.concatenate([y_a, y_b, y_c, y_d], axis=1).astype(BF16)
    gate = mod_ref[0, :, 2 * D_MODEL:3 * D_MODEL]
    o_ref[...] = x_ref[...] + gate * jnp.dot(y, w_ref[0], preferred_element_type=F32)


def _outproj(x, mod, layer, o_a, o_c, o_d, p, conv_w, w_out, *, tm):
    b, s, _ = x.shape
    n = b * s
    per_batch = mod.shape[0] > 1
    assert s % tm == 0 or (tm % s == 0 and not per_batch)
    tiles_per_seq = max(s // tm, 1)
    flat = lambda a: a.reshape(n, a.shape[-1])
    tile = lambda width: pl.BlockSpec((tm, width), lambda t: (t, 0))
    col = lambda g: pl.BlockSpec((tm, W_GRP), lambda t, g=g: (t, g))
    halo = 16
    th = tm // halo
    prev = lambda g: pl.BlockSpec((halo, W_GRP), lambda t, g=g: (jnp.maximum(t * th - 1, 0), g))
    nxt = lambda g: pl.BlockSpec((halo, W_GRP), lambda t, g=g: (jnp.minimum((t + 1) * th, n // halo - 1), g))
    out = pl.pallas_call(
        functools.partial(_outproj_kernel, seq_len=s),
        grid=(n // tm,),
        in_specs=[tile(D_MODEL),
                  pl.BlockSpec((1, 1, 3 * D_MODEL),
                               (lambda t: (t // tiles_per_seq, 0, 0)) if per_batch else (lambda t: (0, 0, 0))),
                  tile(W_GRP), tile(W_GRP), tile(W_GRP),
                  *[col(P16[g]) for g in (G_GA, G_BB, G_CB, G_HB, G_GB, G_GC, G_GD)],
                  prev(P16[G_CB]), prev(P16[G_HB]), nxt(P16[G_CB]), nxt(P16[G_HB]),
                  pl.BlockSpec((1, 3, W_GRP), lambda t: (layer, 0, 0)),
                  pl.BlockSpec((1, D_MODEL, D_MODEL), lambda t: (layer, 0, 0))],
        out_specs=tile(D_MODEL),
        out_shape=jax.ShapeDtypeStruct((n, D_MODEL), F32),
        compiler_params=_cparams("arbitrary"),
        name="outproj",
    )(flat(x), mod, flat(o_a), flat(o_c), flat(o_d), *([flat(p)] * 11), conv_w, w_out)
    return out.reshape(b, s, D_MODEL)


def kernel(x_prompt, x_sample, cache_k, cache_v, state_gla, c, c_ctx, norm_g, w_mod, b_mod, w_in, q_norm_g,
           k_norm_g, rpb, conv_w, gla_up_f, gla_bias_f, gla_up_b, gla_bias_b, gla_norm_g, w_out):
    dec_batch = c.shape[0]
    pad_rows = 8 - (1 + dec_batch) % 8
    cvec = jnp.concatenate([c_ctx[None], c, jnp.zeros((pad_rows, D_MODEL), F32)], axis=0)
    mod = _modulation(cvec, w_mod, b_mod)
    mod_ctx = mod[:, 0:1].reshape(DEPTH, 1, 1, 3 * D_MODEL)
    mod_smp = mod[:, 1:1 + dec_batch].reshape(DEPTH, dec_batch, 1, 3 * D_MODEL)

    w_in_b = jnp.pad(w_in, ((0, 0), (0, 0), (0, N_IN_PAD - N_IN))).astype(BF16)
    w_out_b = w_out.astype(BF16)
    seg = _seg_ones()
    norm_g3 = norm_g.reshape(DEPTH, 1, D_MODEL)
    qn = jnp.tile(q_norm_g, (1, N_HEADS)).reshape(DEPTH, 1, W_GRP)
    kn = jnp.tile(k_norm_g, (1, N_HEADS)).reshape(DEPTH, 1, W_GRP)
    gn = jnp.tile(gla_norm_g, (1, N_HEADS)).reshape(DEPTH, 1, W_GRP)
    gate_up = jnp.zeros((DEPTH, LANES, 2 * W_GRP), F32)
    gate_up = gate_up.at[:, :GATE_RANK, :W_GRP].set(gla_up_f).at[:, GATE_RANK:2 * GATE_RANK, W_GRP:].set(gla_up_b)
    gate_b = jnp.concatenate([gla_bias_f, gla_bias_b], axis=-1).reshape(DEPTH, 1, 2 * W_GRP)
    bias = _bias_tables(rpb)
    cache_k_b = cache_k.astype(BF16)
    cache_v_b = jnp.concatenate([cache_v.astype(BF16), jnp.ones(cache_v.shape, BF16)], axis=-1)

    x = x_prompt
    batch, seq, _ = x.shape
    new_cache_k = jnp.zeros((batch, DEPTH, N_HEADS, seq, HEAD_DIM), F32)
    new_cache_v = jnp.zeros((batch, DEPTH, N_HEADS, seq, HEAD_DIM), F32)
    new_state = jnp.zeros((batch, DEPTH, 2, N_HEADS, HEAD_DIM, HEAD_DIM), F32)
    for l in range(DEPTH):
        p32, p16, new_cache_k, new_cache_v = _inproj(x, mod_ctx[l], l, norm_g3, w_in_b, qn, kn, seg,
                                                     tm=TOKEN_TILE, caches=(new_cache_k, new_cache_v))
        o_a = _attn_ctx(p16)
        o_c = _fft_ctx(p32)
        o_d, new_state = _gla(p32, p16, gate_up, gate_b, gn, seg, l, new_state, zero_init=True)
        x = _outproj(x, mod_ctx[l], l, o_a, o_c, o_d, p16, conv_w, w_out_b, tm=OUTPROJ_TILE)
    y_prompt = x

    xs = x_sample
    for l in range(DEPTH):
        p32, p16, k_hm, v_hm = _inproj(xs, mod_smp[l], l, norm_g3, w_in_b, qn, kn, seg, tm=TOKEN_TILE)
        o_a = _attn_nbr(p16, k_hm, v_hm, cache_k_b, cache_v_b, bias, l)
        o_c = _fft_grid(p32)
        (o_d,) = _gla(p32, p16, gate_up, gate_b, gn, seg, l, state_gla, zero_init=False)
        xs = _outproj(xs, mod_smp[l], l, o_a, o_c, o_d, p16, conv_w, w_out_b, tm=OUTPROJ_TILE)
    return (y_prompt, xs, new_cache_k, new_cache_v, new_state)
```

```python
import functools
import math

import numpy as np
import jax
import jax.numpy as jnp
from jax import lax
from jax.experimental import pallas as pl
from jax.experimental.pallas import tpu as pltpu

F32 = jnp.float32
BF16 = jnp.bfloat16

D_MODEL = 1024
DEPTH = 4
GRID_W = 64
W_GRP = 256
HEAD_DIM = 64
N_HEADS = 4
WIN_H = 8
WIN_W = 16
N_FOURIER = 4
GATE_RANK = 16
GLA_TAU = 16.0
CHUNK = 64
RMS_EPS = 1e-6
N_IN = 14 * W_GRP + 2 * GATE_RANK
LANES = 128
N_IN_PAD = -(-N_IN // LANES) * LANES
G_QA, G_KA, G_VA, G_GA, G_BB, G_CB, G_HB, G_GB, G_UC, G_GC, G_QD, G_KD, G_VD, G_GD = range(14)
F32_GROUPS = (G_UC, G_QD, G_KD)
BF16_GROUPS = (G_QA, G_KA, G_VA, G_VD, G_GA, G_BB, G_CB, G_HB, G_GB, G_GC, G_GD)
P32 = {g: i for i, g in enumerate(F32_GROUPS)}
P16 = {g: i for i, g in enumerate(BF16_GROUPS)}
N_P32 = len(F32_GROUPS) * W_GRP
N_P16 = len(BF16_GROUPS) * W_GRP + LANES
GATE_BLK = len(BF16_GROUPS) * W_GRP // LANES
NEG_BIG = -1e30
VMEM_LIMIT = 56 * 1024 * 1024
TOKEN_TILE = 512
OUTPROJ_TILE = 1024


def _cparams(*sem):
    return pltpu.CompilerParams(dimension_semantics=sem, vmem_limit_bytes=VMEM_LIMIT)


def _bdot(a, b):
    return jnp.dot(a.astype(BF16), b.astype(BF16), preferred_element_type=F32)


def _bdot_nt(a, b):
    return lax.dot_general(a.astype(BF16), b.astype(BF16), (((1,), (1,)), ((), ())),
                           preferred_element_type=F32)


def _split2(x):
    hi = x.astype(BF16)
    lo = (x - hi.astype(F32)).astype(BF16)
    return hi, lo


def _split3(x):
    h1 = x.astype(BF16)
    r = x - h1.astype(F32)
    h2 = r.astype(BF16)
    h3 = (r - h2.astype(F32)).astype(BF16)
    return h1, h2, h3


def _silu(x):
    return x / (1.0 + jnp.exp(-x))


def _head_rms(t, seg, g):
    n = t.shape[0]
    both = jnp.dot(jnp.concatenate(_split2(t * t), axis=0), seg, preferred_element_type=F32)
    ss = both[:n] + both[n:]
    return t * lax.rsqrt(ss * (1.0 / HEAD_DIM) + RMS_EPS) * g


def _dft_cos_sin(n):
    k = np.arange(n)
    ang = 2.0 * np.pi * ((k[:, None] * k[None, :]) % n) / n
    return np.cos(ang), np.sin(ang)


def _block_diag(m, reps):
    n = m.shape[0]
    out = np.zeros((n * reps, n * reps), m.dtype)
    for i in range(reps):
        out[i * n:(i + 1) * n, i * n:(i + 1) * n] = m
    return out


def _seg_ones():
    return jnp.asarray(_block_diag(np.ones((HEAD_DIM, HEAD_DIM)), N_HEADS), BF16)


def _mod_kernel(c_ref, w_ref, b_ref, o_ref):
    o_ref[0] = _bdot(_silu(c_ref[...]), w_ref[0]) + b_ref[0]


def _modulation(cvec, w_mod, b_mod):
    rows = cvec.shape[0]
    return pl.pallas_call(
        _mod_kernel,
        grid=(DEPTH,),
        in_specs=[pl.BlockSpec((rows, D_MODEL), lambda l: (0, 0)),
                  pl.BlockSpec((1, D_MODEL, 3 * D_MODEL), lambda l: (l, 0, 0)),
                  pl.BlockSpec((1, 1, 3 * D_MODEL), lambda l: (l, 0, 0))],
        out_specs=pl.BlockSpec((1, rows, 3 * D_MODEL), lambda l: (l, 0, 0)),
        out_shape=jax.ShapeDtypeStruct((DEPTH, rows, 3 * D_MODEL), F32),
        compiler_params=_cparams("arbitrary"),
        name="adaln_mod",
    )(cvec, w_mod, b_mod.reshape(DEPTH, 1, 3 * D_MODEL))


G_GATE = 14
INPROJ_ORDER = (2, 0, 3, 1, 4, 5, 6, 7, 8, 9, 10, 11, 12, 13, G_GATE)
INPROJ_PIECES = 8


def _inproj_kernel(x_ref, mod_ref, g_ref, w_ref, wg_ref, qn_ref, kn_ref, seg_ref, *rest,
                   for_window_attn, n_seq):
    p32_ref, p16_ref, k_ref, v_ref, h_even, h_odd = rest[-6:]
    t = pl.program_id(0)
    rows = p32_ref.shape[0] // n_seq

    def normed(x_ref, mod_ref, rows=slice(None)):
        x = x_ref[rows, :]
        ms = jnp.mean(x * x, axis=-1, keepdims=True)
        y = x * lax.rsqrt(ms + RMS_EPS) * g_ref[0]
        shift = mod_ref[0, :, 0:D_MODEL]
        scale = mod_ref[0, :, D_MODEL:2 * D_MODEL]
        return (y * (1.0 + scale) + shift).astype(BF16)

    def store_heads(ref, val, ones):
        for q in range(n_seq):
            for h in range(N_HEADS):
                th = val[q * rows:(q + 1) * rows, h * HEAD_DIM:(h + 1) * HEAD_DIM]
                if ones:
                    th = jnp.concatenate([th, jnp.ones_like(th)], axis=1)
                if for_window_attn:
                    ref[0, h] = th.astype(ref.dtype)
                else:
                    ref[q, 0, h] = th

    def step(h_cur, h_nxt):
        seg = seg_ref[...]
        piece = p32_ref.shape[0] // INPROJ_PIECES
        first_late = len(INPROJ_ORDER) - INPROJ_PIECES
        for idx, j in enumerate(INPROJ_ORDER):
            if idx >= first_late:
                part = slice((idx - first_late) * piece, (idx - first_late + 1) * piece)
                h_nxt[part, :] = normed(x_ref, mod_ref, part)
            if j == G_GATE:
                p16_ref[:, GATE_BLK * LANES:N_P16] = jnp.dot(h_cur[...], wg_ref[0],
                                                             preferred_element_type=F32).astype(BF16)
                continue
            cs = slice(j * W_GRP, (j + 1) * W_GRP)
            y = jnp.dot(h_cur[...], w_ref[0, :, cs], preferred_element_type=F32)
            if j == G_QA:
                y = _head_rms(y, seg, qn_ref[0]) * (HEAD_DIM ** -0.5)
            elif j == G_KA:
                y = _head_rms(y, seg, kn_ref[0])
            if j in P32:
                p32_ref[:, P32[j] * W_GRP:(P32[j] + 1) * W_GRP] = y
            else:
                p16_ref[:, P16[j] * W_GRP:(P16[j] + 1) * W_GRP] = y.astype(BF16)
            if j == G_KA:
                store_heads(k_ref, y, False)
            elif j == G_VA:
                store_heads(v_ref, y, for_window_attn)

    @pl.when(t == 0)
    def _():
        h_even[...] = normed(x_ref, mod_ref)
        for ref in (p32_ref, p16_ref, k_ref, v_ref):
            ref[...] = jnp.zeros(ref.shape, ref.dtype)

    @pl.when(t % 2 == 1)
    def _():
        step(h_even, h_odd)

    @pl.when(jnp.logical_and(t > 0, t % 2 == 0))
    def _():
        step(h_odd, h_even)


def _inproj(x, mod, layer, norm_g, w_in, qn, kn, seg, *, tm, caches=None):
    b, s, _ = x.shape
    for_window_attn = caches is None
    assert s % tm == 0 if for_window_attn else tm % s == 0
    n_tiles = b * s // tm
    tiles_per_seq = max(s // tm, 1)
    n_seq = max(tm // s, 1)
    per_batch = mod.shape[0] > 1
    prep = lambda t: jnp.minimum(t, n_tiles - 1)
    done = lambda t: jnp.maximum(t - 1, 0)
    lay = lambda *dims: pl.BlockSpec((1,) + dims, lambda t: (layer,) + (0,) * len(dims))
    in_specs = [pl.BlockSpec((tm, D_MODEL), lambda t: (prep(t), 0)),
                pl.BlockSpec((1, 1, 3 * D_MODEL),
                             (lambda t: (prep(t) // tiles_per_seq, 0, 0)) if per_batch else (lambda t: (0, 0, 0))),
                lay(1, D_MODEL),
                pl.BlockSpec((1, D_MODEL, 14 * W_GRP), lambda t: (layer, 0, 0)),
                pl.BlockSpec((1, D_MODEL, LANES), lambda t: (layer, 0, 14 * W_GRP // LANES)),
                lay(1, W_GRP), lay(1, W_GRP),
                pl.BlockSpec((W_GRP, W_GRP), lambda t: (0, 0))]
    x2 = x.reshape(b * s, D_MODEL)
    args = [x2, mod, norm_g, w_in, w_in, qn, kn, seg]
    out_shape = [jax.ShapeDtypeStruct((b * s, N_P32), F32), jax.ShapeDtypeStruct((b * s, N_P16), BF16)]
    out_specs = [pl.BlockSpec((tm, N_P32), lambda t: (done(t), 0)), pl.BlockSpec((tm, N_P16), lambda t: (done(t), 0))]
    aliases = {}
    if for_window_attn:
        for width in (HEAD_DIM, 2 * HEAD_DIM):
            out_shape.append(jax.ShapeDtypeStruct((b, N_HEADS, s, width), BF16))
            out_specs.append(pl.BlockSpec((1, N_HEADS, tm, width),
                                          lambda t: (done(t) // tiles_per_seq, 0, done(t) % tiles_per_seq, 0)))
    else:
        for i, cache in enumerate(caches):
            aliases[len(args)] = 2 + i
            args.append(cache)
            in_specs.append(pl.BlockSpec(memory_space=pl.ANY))
            out_shape.append(jax.ShapeDtypeStruct(cache.shape, cache.dtype))
            out_specs.append(pl.BlockSpec((n_seq, 1, N_HEADS, s, HEAD_DIM), lambda t: (done(t), layer, 0, 0, 0)))
    p32, p16, k, v = pl.pallas_call(
        functools.partial(_inproj_kernel, for_window_attn=for_window_attn, n_seq=n_seq),
        grid=(n_tiles + 1,),
        in_specs=in_specs,
        out_specs=out_specs,
        out_shape=out_shape,
        input_output_aliases=aliases,
        scratch_shapes=[pltpu.VMEM((tm, D_MODEL), BF16)] * 2,
        compiler_params=_cparams("arbitrary"),
        name="inproj_grid" if for_window_attn else "inproj_ctx",
    )(*args)
    return p32.reshape(b, s, N_P32), p16.reshape(b, s, N_P16), k, v


CTX_SEQS_PER_STEP = 4


def _attn_ctx_kernel(q_ref, k_ref, v_ref, o_ref):
    n_seq, s, _ = q_ref.shape
    heads = [slice(h * HEAD_DIM, (h + 1) * HEAD_DIM) for h in range(N_HEADS)]
    units = [(b, h) for b in range(n_seq) for h in range(N_HEADS)]
    low_half = lax.broadcasted_iota(jnp.int32, (s, 2 * HEAD_DIM), 1) < HEAD_DIM
    ones = jnp.ones((s, HEAD_DIM), BF16)
    q, k, v = ([ref[b] for b in range(n_seq)] for ref in (q_ref, k_ref, v_ref))
    scores = [_bdot_nt(q[b][:, heads[h]], k[b][:, heads[h]]) for b, h in units]
    e = [jnp.exp(x - jnp.max(x, axis=-1, keepdims=True)).astype(BF16) for x in scores]
    res = [jnp.dot(e[u], jnp.concatenate([v[b][:, heads[h]], ones], axis=1), preferred_element_type=F32)
           for u, (b, h) in enumerate(units)]
    for b in range(n_seq):
        pairs = []
        for h in range(0, N_HEADS, 2):
            even, odd = res[b * N_HEADS + h], res[b * N_HEADS + h + 1]
            pairs.append(jnp.where(low_half, even / pltpu.roll(even, HEAD_DIM, axis=1),
                                   pltpu.roll(odd, HEAD_DIM, axis=1) / odd))
        o_ref[b] = jnp.concatenate(pairs, axis=1)


def _attn_ctx(p):
    b, s, _ = p.shape
    n_seq = CTX_SEQS_PER_STEP
    assert b % n_seq == 0
    col = lambda g: pl.BlockSpec((n_seq, s, W_GRP), lambda i, g=g: (i, 0, g))
    return pl.pallas_call(
        _attn_ctx_kernel,
        grid=(b // n_seq,),
        in_specs=[col(P16[G_QA]), col(P16[G_KA]), col(P16[G_VA])],
        out_specs=pl.BlockSpec((n_seq, s, W_GRP), lambda i: (i, 0, 0)),
        out_shape=jax.ShapeDtypeStruct((b, s, W_GRP), F32),
        compiler_params=_cparams("arbitrary"),
        name="attn_ctx",
    )(p, p, p)


def _bias_kernel(rpb_ref, o_ref):
    lh = pl.program_id(0)
    q = lax.broadcasted_iota(jnp.int32, (GRID_W, GRID_W), 0)
    kc = lax.broadcasted_iota(jnp.int32, (GRID_W, GRID_W), 1)
    dc = jnp.clip(kc - q + (WIN_W - 1), 0, 2 * WIN_W - 2)
    cs = jnp.clip(q - WIN_W // 2, 0, GRID_W - WIN_W)
    in_win = jnp.where(kc >= cs, jnp.where(kc < cs + WIN_W, 1, 0), 0) == 1
    n_dr, n_dc = 2 * WIN_H - 1, 2 * WIN_W - 1
    tiles = []
    for dr0 in range(0, n_dr, WIN_H):
        drs = range(dr0, min(dr0 + WIN_H, n_dr))
        part = [jnp.zeros((GRID_W, GRID_W), F32) for _ in drs]
        for d in range(n_dc):
            hit = dc == d
            part = [jnp.where(hit, rpb_ref[(lh * n_dr + dr) * n_dc + d], t) for dr, t in zip(drs, part)]
        tiles += [jnp.where(in_win, t, NEG_BIG) for t in part]
    for dl in range(WIN_H):
        for i in range(WIN_H):
            o_ref[0, dl, :, i * GRID_W:(i + 1) * GRID_W] = tiles[dl + i]


def _bias_tables(rpb):
    out = pl.pallas_call(
        _bias_kernel,
        grid=(DEPTH * N_HEADS,),
        in_specs=[pl.BlockSpec(memory_space=pltpu.SMEM)],
        out_specs=pl.BlockSpec((1, WIN_H, GRID_W, WIN_H * GRID_W), lambda i: (i, 0, 0, 0)),
        out_shape=jax.ShapeDtypeStruct((DEPTH * N_HEADS, WIN_H, GRID_W, WIN_H * GRID_W), F32),
        compiler_params=_cparams("arbitrary"),
        name="rpb_tables",
    )(rpb.reshape(-1))
    return out.reshape(DEPTH, N_HEADS, WIN_H, GRID_W, WIN_H * GRID_W)


ATTN_ROW_GROUP = 8


def _attn_nbr_kernel(q_ref, k_ref, v_ref, ck_ref, cv_ref, bias_ref, o_ref, *, rows_per_step):
    rb = pl.program_id(1)
    n_rows = k_ref.shape[2] // GRID_W
    kh = min(WIN_H, n_rows)
    n_lat = kh * GRID_W
    heads = [slice(h * HEAD_DIM, (h + 1) * HEAD_DIM) for h in range(N_HEADS)]
    units = [(g, h) for g in range(ATTN_ROW_GROUP) for h in range(N_HEADS)]
    low_half = lax.broadcasted_iota(jnp.int32, (GRID_W, 2 * HEAD_DIM), 1) < HEAD_DIM

    def rows_body(jj, carry):
        q0, k0, dl, q = [], [], [], []
        for g in range(ATTN_ROW_GROUP):
            j = jj * ATTN_ROW_GROUP + g
            r = rb * rows_per_step + j
            rs = jnp.clip(r - kh // 2, 0, n_rows - kh)
            dl.append(rs - r + (WIN_H - 1))
            k0.append(pl.multiple_of(rs * GRID_W, GRID_W))
            q0.append(pl.multiple_of(j * GRID_W, GRID_W))
            q.append(q_ref[0, pl.ds(q0[g], GRID_W), :].astype(BF16))
        s_lat = [_bdot_nt(q[g][:, heads[h]], k_ref[0, h, pl.ds(k0[g], n_lat), :]) + bias_ref[0, h, dl[g]]
                 for g, h in units]
        q_all = jnp.concatenate(q, axis=0)
        s_ctx_all = [_bdot_nt(q_all[:, hs], ck_ref[0, 0, h]) for h, hs in enumerate(heads)]
        s_ctx = [s_ctx_all[h][g * GRID_W:(g + 1) * GRID_W] for g, h in units]
        m = [jnp.maximum(jnp.max(a, axis=-1, keepdims=True), jnp.max(c, axis=-1, keepdims=True))
             for a, c in zip(s_lat, s_ctx)]
        e_lat = [jnp.exp(a - mu).astype(BF16) for a, mu in zip(s_lat, m)]
        e_ctx = [jnp.exp(c - mu).astype(BF16) for c, mu in zip(s_ctx, m)]
        pv_ctx = [jnp.dot(jnp.concatenate([e_ctx[g * N_HEADS + h] for g in range(ATTN_ROW_GROUP)], axis=0),
                          cv_ref[0, 0, h], preferred_element_type=F32) for h in range(N_HEADS)]
        res = [jnp.dot(e_lat[u], v_ref[0, h, pl.ds(k0[g], n_lat), :], preferred_element_type=F32)
               + pv_ctx[h][g * GRID_W:(g + 1) * GRID_W]
               for u, (g, h) in enumerate(units)]
        for g in range(ATTN_ROW_GROUP):
            pairs = []
            for h in range(0, N_HEADS, 2):
                even, odd = res[g * N_HEADS + h], res[g * N_HEADS + h + 1]
                pairs.append(jnp.where(low_half, even / pltpu.roll(even, HEAD_DIM, axis=1),
                                       pltpu.roll(odd, HEAD_DIM, axis=1) / odd))
            o_ref[0, pl.ds(q0[g], GRID_W), :] = jnp.concatenate(pairs, axis=1)
        return carry

    lax.fori_loop(0, rows_per_step // ATTN_ROW_GROUP, rows_body, 0)


def _attn_nbr(p, k_hm, v_hm, cache_k, cache_v, bias, layer, *, rows_per_step=32):
    b, s, _ = p.shape
    tq = rows_per_step * GRID_W
    past = cache_v.shape[3]
    ctx_spec = lambda *dims: pl.BlockSpec((1, 1, N_HEADS) + dims, lambda i, j: (i, layer, 0, 0, 0))
    return pl.pallas_call(
        functools.partial(_attn_nbr_kernel, rows_per_step=rows_per_step),
        grid=(b, s // tq),
        in_specs=[pl.BlockSpec((1, tq, W_GRP), lambda i, j: (i, j, P16[G_QA])),
                  pl.BlockSpec((1, N_HEADS, s, HEAD_DIM), lambda i, j: (i, 0, 0, 0)),
                  pl.BlockSpec((1, N_HEADS, s, 2 * HEAD_DIM), lambda i, j: (i, 0, 0, 0)),
                  ctx_spec(past, HEAD_DIM), ctx_spec(past, 2 * HEAD_DIM),
                  pl.BlockSpec((1, N_HEADS, WIN_H, GRID_W, WIN_H * GRID_W), lambda i, j: (layer, 0, 0, 0, 0))],
        out_specs=pl.BlockSpec((1, tq, W_GRP), lambda i, j: (i, j, 0)),
        out_shape=jax.ShapeDtypeStruct((b, s, W_GRP), F32),
        compiler_params=_cparams("arbitrary", "arbitrary"),
        name="attn_nbr",
    )(p, k_hm, v_hm, cache_k, cache_v, bias)


def _fft_ctx_consts(s):
    c, sn = _dft_cos_sin(s)
    cc, sc = _dft_cos_sin(W_GRP // N_FOURIER)
    m1 = np.concatenate([c, -sn], axis=0)
    m2 = np.concatenate([_block_diag(cc, N_FOURIER), _block_diag(sc, N_FOURIER)], axis=0)
    return jnp.asarray(m1, F32), jnp.asarray(m2, F32)


def _fft_ctx_kernel(u_ref, m1_ref, m2_ref, o_ref, *, scale):
    n_seq, s, _ = u_ref.shape
    m1 = m1_ref[...].astype(BF16)
    t = [_bdot(m1, u_ref[b]) for b in range(n_seq)]
    x = jnp.concatenate([jnp.concatenate([tb[:s], tb[s:]], axis=1) for tb in t], axis=0)
    out = _bdot(x, m2_ref[...]) * scale
    for b in range(n_seq):
        o_ref[b] = out[b * s:(b + 1) * s]


def _fft_ctx(p):
    b, s, _ = p.shape
    n_seq = CTX_SEQS_PER_STEP
    assert b % n_seq == 0
    consts = _fft_ctx_consts(s)
    full = lambda a: pl.BlockSpec(a.shape, lambda i: (0,) * a.ndim)
    return pl.pallas_call(
        functools.partial(_fft_ctx_kernel, scale=1.0 / math.sqrt(s * (W_GRP // N_FOURIER))),
        grid=(b // n_seq,),
        in_specs=[pl.BlockSpec((n_seq, s, W_GRP), lambda i: (i, 0, P32[G_UC]))] + [full(a) for a in consts],
        out_specs=pl.BlockSpec((n_seq, s, W_GRP), lambda i: (i, 0, 0)),
        out_shape=jax.ShapeDtypeStruct((b, s, W_GRP), F32),
        compiler_params=_cparams("arbitrary"),
        name="fft_ctx",
    )(p, *consts)


def _fft_grid_consts():
    n = GRID_W
    c, sn = _dft_cos_sin(n)
    cc, sc = _dft_cos_sin(W_GRP // N_FOURIER)
    m1 = np.concatenate([c, -sn], axis=0)
    m2 = np.block([[c, sn], [-sn, c]])
    reps = LANES // (W_GRP // N_FOURIER)
    m3 = np.concatenate([_block_diag(cc, reps), _block_diag(sc, reps)], axis=0)
    k = np.arange(n)
    ang = 2.0 * np.pi * (k[:, None] * k[None, :]) / (n * n)
    twr = np.broadcast_to(np.cos(ang)[:, :, None], (n, n, LANES))
    twi = np.broadcast_to(-np.sin(ang)[:, :, None], (n, n, LANES))
    return tuple(jnp.asarray(m, F32) for m in (m1, m2, m3, twr, twi))


FFT_UNROLL = 8
FFT_PITCH = 72


def _fft_grid_kernel(u_ref, m1_ref, m2_ref, m3_ref, twr_ref, twi_ref, o_ref, tr_scr, ti_scr, *, scale):
    n = GRID_W
    m1, m2, m3 = (r[...].astype(BF16) for r in (m1_ref, m2_ref, m3_ref))

    def stage1(b, carry):
        a_rows = u_ref[0, pl.ds(b, n, stride=n), :]
        t = _bdot(m1, a_rows)
        tr, ti = t[:n], t[n:]
        wr, wi = twr_ref[b], twi_ref[b]
        tr_scr[pl.ds(b, n, stride=FFT_PITCH), :] = tr * wr - ti * wi
        ti_scr[pl.ds(b, n, stride=FFT_PITCH), :] = tr * wi + ti * wr
        return carry

    lax.fori_loop(0, n, stage1, 0, unroll=FFT_UNROLL)

    def stage2(dd, carry):
        xcs = []
        for u in range(FFT_UNROLL):
            r0 = pl.multiple_of((dd * FFT_UNROLL + u) * FFT_PITCH, 8)
            g = jnp.concatenate([tr_scr[pl.ds(r0, n), :], ti_scr[pl.ds(r0, n), :]], axis=0)
            x = _bdot(m2, g)
            xcs.append(jnp.concatenate([x[:n], x[n:]], axis=1))
        out = _bdot(jnp.concatenate(xcs, axis=0), m3) * scale
        for u in range(FFT_UNROLL):
            o_ref[0, pl.ds(dd * FFT_UNROLL + u, n, stride=n), :] = out[u * n:(u + 1) * n]
        return carry

    lax.fori_loop(0, n // FFT_UNROLL, stage2, 0)


def _fft_grid(p):
    b, s, _ = p.shape
    assert s == GRID_W * GRID_W
    consts = _fft_grid_consts()
    full = lambda a: pl.BlockSpec(a.shape, lambda i, j: (0,) * a.ndim)
    blk0 = P32[G_UC] * W_GRP // LANES
    return pl.pallas_call(
        functools.partial(_fft_grid_kernel, scale=1.0 / math.sqrt(s * (W_GRP // N_FOURIER))),
        grid=(b, W_GRP // LANES),
        in_specs=[pl.BlockSpec((1, s, LANES), lambda i, j: (i, 0, blk0 + j))] + [full(a) for a in consts],
        out_specs=pl.BlockSpec((1, s, LANES), lambda i, j: (i, 0, j)),
        out_shape=jax.ShapeDtypeStruct((b, s, W_GRP), F32),
        scratch_shapes=[pltpu.VMEM((GRID_W * FFT_PITCH, LANES), F32)] * 2,
        compiler_params=_cparams("arbitrary", "arbitrary"),
        name="fft_grid",
    )(p, *consts)


GLA_CHUNKS_PER_TRIP = 8


def _gla_kernel(*refs, n_chunks, n_seq, group, zero_init):
    if zero_init:
        (q_ref, k_ref, v_ref, a_ref, up_ref, gb_ref, gn_ref, seg_ref, _,
         o_ref, sfin_ref, qd_scr, st_scr, dec_scr) = refs
        s0_ref = None
    else:
        (q_ref, k_ref, v_ref, a_ref, up_ref, gb_ref, gn_ref, seg_ref, s0_ref,
         o_ref, qd_scr, st_scr, dec_scr) = refs
        sfin_ref = None
    c = CHUNK
    row = lax.broadcasted_iota(jnp.int32, (c, c), 0)
    col = lax.broadcasted_iota(jnp.int32, (c, c), 1)
    heads = [slice(h * HEAD_DIM, (h + 1) * HEAD_DIM) for h in range(N_HEADS)]
    wrow = lax.broadcasted_iota(jnp.int32, (W_GRP, W_GRP), 0)
    wcol = lax.broadcasted_iota(jnp.int32, (W_GRP, W_GRP), 1)
    head_diag = (wrow // HEAD_DIM) == (wcol // HEAD_DIM)

    def blockdiag(x):
        return jnp.where(head_diag, jnp.concatenate([x] * N_HEADS, axis=0), jnp.zeros((), x.dtype))

    trow = lax.broadcasted_iota(jnp.int32, (c, W_GRP), 0)
    tcol = lax.broadcasted_iota(jnp.int32, (c, W_GRP), 1) % c
    keep = [trow >= tcol, trow <= tcol]
    eye_tiled = jnp.where(trow == tcol, 1.0, 0.0).astype(BF16)

    def head_transpose_exact(x):
        return sum(_bdot_nt(eye_tiled, blockdiag(part)) for part in _split3(x))

    sum_ops = [jnp.where(row >= col, 1.0, 0.0).astype(BF16),
               jnp.where(row <= col, 1.0, 0.0).astype(BF16)]
    dirs = (0, 1)

    def chunk_rows(n):
        return pl.ds(pl.multiple_of(n * c, c), c)

    def phase1(i, carry):
        ns = [i * group + g for g in range(group)]
        rows = [chunk_rows(n) for n in ns]
        x_all = _bdot(jnp.concatenate([a_ref[0, r, :] for r in rows], axis=0), up_ref[0]) + gb_ref[0]
        x = [[x_all[g * c:(g + 1) * c, d * W_GRP:(d + 1) * W_GRP] for d in dirs] for g in range(group)]
        la = [[(jnp.minimum(xd, 0.0) - jnp.log(1.0 + jnp.exp(-jnp.abs(xd)))) * (1.0 / GLA_TAU) for xd in xg]
              for xg in x]
        cum = [[sum(jnp.dot(sum_ops[d], part, preferred_element_type=F32) for part in _split3(lg[d]))
                for d in dirs] for lg in la]
        tot_row = [[cg[0][c - 1:c, :], cg[1][0:1, :]] for cg in cum]
        q = [q_ref[0, r, :] * (HEAD_DIM ** -0.5) for r in rows]
        k = [k_ref[0, r, :] for r in rows]
        v_bd = [blockdiag(v_ref[0, r, :].astype(BF16)) for r in rows]
        q_dec = [[(q[g] * jnp.exp(cum[g][d])).astype(BF16) for d in dirs] for g in range(group)]
        k_inv = [[blockdiag((k[g] * jnp.exp(-cum[g][d])).astype(BF16)) for d in dirs] for g in range(group)]
        k_end = [[blockdiag((k[g] * jnp.exp(tot_row[g][d] - cum[g][d])).astype(BF16)) for d in dirs]
                 for g in range(group)]
        scores = [[_bdot_nt(q_dec[g][d], k_inv[g][d]) for d in dirs] for g in range(group)]
        v_t = [_bdot_nt(eye_tiled, vb).astype(BF16) for vb in v_bd]
        upd = [[_bdot(v_t[g], k_end[g][d]) for d in dirs] for g in range(group)]
        att = [[jnp.where(keep[d], scores[g][d], 0.0).astype(BF16) for d in dirs] for g in range(group)]
        for g in range(group):
            both = _bdot(jnp.concatenate(att[g], axis=0), v_bd[g])
            o_ref[0, rows[g], :] = both[:c] + both[c:]
            for d in dirs:
                qd_scr[d, rows[g], :] = q_dec[g][d]
                dec_scr[d, pl.ds(ns[g], 1), :] = jnp.exp(tot_row[g][d])
                st_scr[d, ns[g]] = upd[g][d]
        return carry

    lax.fori_loop(0, n_chunks // group, phase1, 0)

    def init_state(d):
        if zero_init:
            return jnp.zeros((HEAD_DIM, W_GRP), F32)
        return head_transpose_exact(jnp.concatenate([s0_ref[0, 0, d, h] for h in range(N_HEADS)], axis=1))

    per_seq = n_chunks // n_seq
    finals = []
    for q in range(n_seq):
        finals.append([])
        for d in range(2):
            def scan_body(i, st, d=d, q=q):
                n = q * per_seq + (i if d == 0 else per_seq - 1 - i)
                upd = st_scr[d, n]
                st_scr[d, n] = st
                return st * dec_scr[d, pl.ds(n, 1), :] + upd
            finals[q].append(lax.fori_loop(0, per_seq, scan_body, init_state(d)))

    def phase3(i, carry):
        ns = [i * group + g for g in range(group)]
        rows = [chunk_rows(n) for n in ns]
        st_bd = [[blockdiag(st_scr[d, n].astype(BF16)) for d in dirs] for n in ns]
        inter = [[_bdot_nt(qd_scr[d, rows[g], :], st_bd[g][d]) for d in dirs] for g in range(group)]
        o = jnp.concatenate([o_ref[0, rows[g], :] + inter[g][0] + inter[g][1] for g in range(group)], axis=0)
        o = _head_rms(o, seg_ref[...], gn_ref[0])
        for g in range(group):
            o_ref[0, rows[g], :] = o[g * c:(g + 1) * c]
        return carry

    lax.fori_loop(0, n_chunks // group, phase3, 0)
    if sfin_ref is not None:
        for q in range(n_seq):
            for d in range(2):
                s_fin = head_transpose_exact(finals[q][d])
                for h, hs in enumerate(heads):
                    sfin_ref[q, 0, d, h] = s_fin[:, hs]


def _gla(p32, p16, gate_up, gate_b, gn, seg, layer, state, *, zero_init):
    b, s, _ = p32.shape
    n_seq = max(1, min(b, GLA_CHUNKS_PER_TRIP * CHUNK // s)) if zero_init else 1
    assert b % n_seq == 0
    rows = n_seq * s
    n_chunks = rows // CHUNK
    group = min(GLA_CHUNKS_PER_TRIP, n_chunks)
    assert n_chunks % group == 0
    fold = lambda a: a.reshape(b // n_seq, rows, a.shape[-1])
    col = lambda g: pl.BlockSpec((1, rows, W_GRP), lambda i, g=g: (i, 0, g))
    lay = lambda shape: pl.BlockSpec((1,) + shape, lambda i: (layer,) + (0,) * len(shape))
    in_specs = [col(P32[G_QD]), col(P32[G_KD]), col(P16[G_VD]),
                pl.BlockSpec((1, rows, LANES), lambda i: (i, 0, GATE_BLK)),
                lay((LANES, 2 * W_GRP)), lay((1, 2 * W_GRP)),
                lay((1, W_GRP)), pl.BlockSpec((W_GRP, W_GRP), lambda i: (0, 0))]
    args = [fold(p32), fold(p32), fold(p16), fold(p16), gate_up, gate_b, gn, seg, state]
    state_spec = pl.BlockSpec((n_seq, 1, 2, N_HEADS, HEAD_DIM, HEAD_DIM), lambda i: (i, layer, 0, 0, 0, 0))
    out_specs = [pl.BlockSpec((1, rows, W_GRP), lambda i: (i, 0, 0))]
    out_shape = [jax.ShapeDtypeStruct((b // n_seq, rows, W_GRP), F32)]
    if zero_init:
        in_specs.append(pl.BlockSpec(memory_space=pl.ANY))
        out_specs.append(state_spec)
        out_shape.append(jax.ShapeDtypeStruct(state.shape, state.dtype))
    else:
        in_specs.append(state_spec)
    o, *rest = pl.pallas_call(
        functools.partial(_gla_kernel, n_chunks=n_chunks, n_seq=n_seq, group=group, zero_init=zero_init),
        grid=(b // n_seq,),
        in_specs=in_specs,
        out_specs=out_specs,
        out_shape=out_shape,
        input_output_aliases={len(args) - 1: 1} if zero_init else {},
        scratch_shapes=[pltpu.VMEM((2, rows, W_GRP), BF16),
                        pltpu.VMEM((2, n_chunks, HEAD_DIM, W_GRP), F32),
                        pltpu.VMEM((2, max(n_chunks, 8), W_GRP), F32)],
        compiler_params=_cparams("arbitrary"),
        name="gla_zero" if zero_init else "gla",
    )(*args)
    return (o.reshape(b, s, W_GRP), *rest)


def _outproj_kernel(x_ref, mod_ref, oa_ref, oc_ref, od_ref, ga_ref, bb_ref, cb_ref, hb_ref, gb_ref,
                    gc_ref, gd_ref, cprev_ref, hprev_ref, cnext_ref, hnext_ref, cw_ref, w_ref, o_ref, *, seq_len):
    tm = x_ref.shape[0]
    f32 = lambda ref: ref[...].astype(F32)
    last = cprev_ref.shape[0] - 1
    u = f32(cb_ref) * f32(hb_ref)
    rid = lax.broadcasted_iota(jnp.int32, u.shape, 0)
    pos = (pl.program_id(0) * tm + rid) % seq_len
    u_prev = jnp.where(rid == 0, f32(cprev_ref)[last:, :] * f32(hprev_ref)[last:, :], pltpu.roll(u, 1, axis=0))
    u_next = jnp.where(rid == tm - 1, f32(cnext_ref)[0:1, :] * f32(hnext_ref)[0:1, :], pltpu.roll(u, tm - 1, axis=0))
    u_prev = jnp.where(pos == 0, 0.0, u_prev)
    u_next = jnp.where(pos == seq_len - 1, 0.0, u_next)
    cw = cw_ref[0]
    conv = u_prev * cw[0:1, :] + u * cw[1:2, :] + u_next * cw[2:3, :]
    y_a = oa_ref[...] * _silu(f32(ga_ref))
    y_b = f32(bb_ref) * conv * _silu(f32(gb_ref))
    y_c = oc_ref[...] * _silu(f32(gc_ref))
    y_d = od_ref[...] * _silu(f32(gd_ref))
    y = jnp.concatenate([y_a, y_b, y_c, y_d], axis=1).astype(BF16)
    gate = mod_ref[0, :, 2 * D_MODEL:3 * D_MODEL]
    o_ref[...] = x_ref[...] + gate * jnp.dot(y, w_ref[0], preferred_element_type=F32)


def _outproj(x, mod, layer, o_a, o_c, o_d, p, conv_w, w_out, *, tm):
    b, s, _ = x.shape
    n = b * s
    per_batch = mod.shape[0] > 1
    assert s % tm == 0 or (tm % s == 0 and not per_batch)
    tiles_per_seq = max(s // tm, 1)
    flat = lambda a: a.reshape(n, a.shape[-1])
    tile = lambda width: pl.BlockSpec((tm, width), lambda t: (t, 0))
    col = lambda g: pl.BlockSpec((tm, W_GRP), lambda t, g=g: (t, g))
    halo = 16
    th = tm // halo
    prev = lambda g: pl.BlockSpec((halo, W_GRP), lambda t, g=g: (jnp.maximum(t * th - 1, 0), g))
    nxt = lambda g: pl.BlockSpec((halo, W_GRP), lambda t, g=g: (jnp.minimum((t + 1) * th, n // halo - 1), g))
    out = pl.pallas_call(
        functools.partial(_outproj_kernel, seq_len=s),
        grid=(n // tm,),
        in_specs=[tile(D_MODEL),
                  pl.BlockSpec((1, 1, 3 * D_MODEL),
                               (lambda t: (t // tiles_per_seq, 0, 0)) if per_batch else (lambda t: (0, 0, 0))),
                  tile(W_GRP), tile(W_GRP), tile(W_GRP),
                  *[col(P16[g]) for g in (G_GA, G_BB, G_CB, G_HB, G_GB, G_GC, G_GD)],
                  prev(P16[G_CB]), prev(P16[G_HB]), nxt(P16[G_CB]), nxt(P16[G_HB]),
                  pl.BlockSpec((1, 3, W_GRP), lambda t: (layer, 0, 0)),
                  pl.BlockSpec((1, D_MODEL, D_MODEL), lambda t: (layer, 0, 0))],
        out_specs=tile(D_MODEL),
        out_shape=jax.ShapeDtypeStruct((n, D_MODEL), F32),
        compiler_params=_cparams("arbitrary"),
        name="outproj",
    )(flat(x), mod, flat(o_a), flat(o_c), flat(o_d), *([flat(p)] * 11), conv_w, w_out)
    return out.reshape(b, s, D_MODEL)


def kernel(x_prompt, x_sample, cache_k, cache_v, state_gla, c, c_ctx, norm_g, w_mod, b_mod, w_in, q_norm_g,
           k_norm_g, rpb, conv_w, gla_up_f, gla_bias_f, gla_up_b, gla_bias_b, gla_norm_g, w_out):
    dec_batch = c.shape[0]
    pad_rows = 8 - (1 + dec_batch) % 8
    cvec = jnp.concatenate([c_ctx[None], c, jnp.zeros((pad_rows, D_MODEL), F32)], axis=0)
    mod = _modulation(cvec, w_mod, b_mod)
    mod_ctx = mod[:, 0:1].reshape(DEPTH, 1, 1, 3 * D_MODEL)
    mod_smp = mod[:, 1:1 + dec_batch].reshape(DEPTH, dec_batch, 1, 3 * D_MODEL)

    w_in_b = jnp.pad(w_in, ((0, 0), (0, 0), (0, N_IN_PAD - N_IN))).astype(BF16)
    w_out_b = w_out.astype(BF16)
    seg = _seg_ones()
    norm_g3 = norm_g.reshape(DEPTH, 1, D_MODEL)
    qn = jnp.tile(q_norm_g, (1, N_HEADS)).reshape(DEPTH, 1, W_GRP)
    kn = jnp.tile(k_norm_g, (1, N_HEADS)).reshape(DEPTH, 1, W_GRP)
    gn = jnp.tile(gla_norm_g, (1, N_HEADS)).reshape(DEPTH, 1, W_GRP)
    gate_up = jnp.zeros((DEPTH, LANES, 2 * W_GRP), F32)
    gate_up = gate_up.at[:, :GATE_RANK, :W_GRP].set(gla_up_f).at[:, GATE_RANK:2 * GATE_RANK, W_GRP:].set(gla_up_b)
    gate_b = jnp.concatenate([gla_bias_f, gla_bias_b], axis=-1).reshape(DEPTH, 1, 2 * W_GRP)
    bias = _bias_tables(rpb)
    cache_k_b = cache_k.astype(BF16)
    cache_v_b = jnp.concatenate([cache_v.astype(BF16), jnp.ones(cache_v.shape, BF16)], axis=-1)

    x = x_prompt
    batch, seq, _ = x.shape
    new_cache_k = jnp.zeros((batch, DEPTH, N_HEADS, seq, HEAD_DIM), F32)
    new_cache_v = jnp.zeros((batch, DEPTH, N_HEADS, seq, HEAD_DIM), F32)
    new_state = jnp.zeros((batch, DEPTH, 2, N_HEADS, HEAD_DIM, HEAD_DIM), F32)
    for l in range(DEPTH):
        p32, p16, new_cache_k, new_cache_v = _inproj(x, mod_ctx[l], l, norm_g3, w_in_b, qn, kn, seg,
                                                     tm=TOKEN_TILE, caches=(new_cache_k, new_cache_v))
        o_a = _attn_ctx(p16)
        o_c = _fft_ctx(p32)
        o_d, new_state = _gla(p32, p16, gate_up, gate_b, gn, seg, l, new_state, zero_init=True)
        x = _outproj(x, mod_ctx[l], l, o_a, o_c, o_d, p16, conv_w, w_out_b, tm=OUTPROJ_TILE)
    y_prompt = x

    xs = x_sample
    for l in range(DEPTH):
        p32, p16, k_hm, v_hm = _inproj(xs, mod_smp[l], l, norm_g3, w_in_b, qn, kn, seg, tm=TOKEN_TILE)
        o_a = _attn_nbr(p16, k_hm, v_hm, cache_k_b, cache_v_b, bias, l)
        o_c = _fft_grid(p32)
        (o_d,) = _gla(p32, p16, gate_up, gate_b, gn, seg, l, state_gla, zero_init=False)
        xs = _outproj(xs, mod_smp[l], l, o_a, o_c, o_d, p16, conv_w, w_out_b, tm=OUTPROJ_TILE)
    return (y_prompt, xs, new_cache_k, new_cache_v, new_state)
```

```python
import functools
import math

import numpy as np
import jax
import jax.numpy as jnp
from jax import lax
from jax.experimental import pallas as pl
from jax.experimental.pallas import tpu as pltpu

F32 = jnp.float32
BF16 = jnp.bfloat16

D_MODEL = 1024
DEPTH = 4
GRID_W = 64
W_GRP = 256
HEAD_DIM = 64
N_HEADS = 4
WIN_H = 8
WIN_W = 16
N_FOURIER = 4
GATE_RANK = 16
GLA_TAU = 16.0
CHUNK = 64
RMS_EPS = 1e-6
N_IN = 14 * W_GRP + 2 * GATE_RANK
LANES = 128
N_IN_PAD = -(-N_IN // LANES) * LANES
G_QA, G_KA, G_VA, G_GA, G_BB, G_CB, G_HB, G_GB, G_UC, G_GC, G_QD, G_KD, G_VD, G_GD = range(14)
F32_GROUPS = (G_UC, G_QD, G_KD)
BF16_GROUPS = (G_QA, G_KA, G_VA, G_VD, G_GA, G_BB, G_CB, G_HB, G_GB, G_GC, G_GD)
P32 = {g: i for i, g in enumerate(F32_GROUPS)}
P16 = {g: i for i, g in enumerate(BF16_GROUPS)}
N_P32 = len(F32_GROUPS) * W_GRP
N_P16 = len(BF16_GROUPS) * W_GRP + LANES
GATE_BLK = len(BF16_GROUPS) * W_GRP // LANES
NEG_BIG = -1e30
VMEM_LIMIT = 56 * 1024 * 1024
TOKEN_TILE = 512
OUTPROJ_TILE = 1024


def _cparams(*sem):
    return pltpu.CompilerParams(dimension_semantics=sem, vmem_limit_bytes=VMEM_LIMIT)


def _bdot(a, b):
    return jnp.dot(a.astype(BF16), b.astype(BF16), preferred_element_type=F32)


def _bdot_nt(a, b):
    return lax.dot_general(a.astype(BF16), b.astype(BF16), (((1,), (1,)), ((), ())),
                           preferred_element_type=F32)


def _split2(x):
    hi = x.astype(BF16)
    lo = (x - hi.astype(F32)).astype(BF16)
    return hi, lo


def _split3(x):
    h1 = x.astype(BF16)
    r = x - h1.astype(F32)
    h2 = r.astype(BF16)
    h3 = (r - h2.astype(F32)).astype(BF16)
    return h1, h2, h3


def _silu(x):
    return x / (1.0 + jnp.exp(-x))


def _head_rms(t, seg, g):
    n = t.shape[0]
    both = jnp.dot(jnp.concatenate(_split2(t * t), axis=0), seg, preferred_element_type=F32)
    ss = both[:n] + both[n:]
    return t * lax.rsqrt(ss * (1.0 / HEAD_DIM) + RMS_EPS) * g


def _dft_cos_sin(n):
    k = np.arange(n)
    ang = 2.0 * np.pi * ((k[:, None] * k[None, :]) % n) / n
    return np.cos(ang), np.sin(ang)


def _block_diag(m, reps):
    n = m.shape[0]
    out = np.zeros((n * reps, n * reps), m.dtype)
    for i in range(reps):
        out[i * n:(i + 1) * n, i * n:(i + 1) * n] = m
    return out


def _seg_ones():
    return jnp.asarray(_block_diag(np.ones((HEAD_DIM, HEAD_DIM)), N_HEADS), BF16)


def _mod_kernel(c_ref, w_ref, b_ref, o_ref):
    o_ref[0] = _bdot(_silu(c_ref[...]), w_ref[0]) + b_ref[0]


def _modulation(cvec, w_mod, b_mod):
    rows = cvec.shape[0]
    return pl.pallas_call(
        _mod_kernel,
        grid=(DEPTH,),
        in_specs=[pl.BlockSpec((rows, D_MODEL), lambda l: (0, 0)),
                  pl.BlockSpec((1, D_MODEL, 3 * D_MODEL), lambda l: (l, 0, 0)),
                  pl.BlockSpec((1, 1, 3 * D_MODEL), lambda l: (l, 0, 0))],
        out_specs=pl.BlockSpec((1, rows, 3 * D_MODEL), lambda l: (l, 0, 0)),
        out_shape=jax.ShapeDtypeStruct((DEPTH, rows, 3 * D_MODEL), F32),
        compiler_params=_cparams("arbitrary"),
        name="adaln_mod",
    )(cvec, w_mod, b_mod.reshape(DEPTH, 1, 3 * D_MODEL))


G_GATE = 14
INPROJ_ORDER = (2, 0, 3, 1, 4, 5, 6, 7, 8, 9, 10, 11, 12, 13, G_GATE)
INPROJ_PIECES = 8


def _inproj_kernel(x_ref, mod_ref, g_ref, w_ref, wg_ref, qn_ref, kn_ref, seg_ref, *rest,
                   for_window_attn, n_seq, layer):
    p32_ref, p16_ref, k_ref, v_ref, h_even, h_odd = rest[-6:]
    t = pl.program_id(0)
    rows = p32_ref.shape[0] // n_seq

    def normed(x_ref, mod_ref, rows=slice(None)):
        x = x_ref[rows, :]
        ms = jnp.mean(x * x, axis=-1, keepdims=True)
        y = x * lax.rsqrt(ms + RMS_EPS) * g_ref[0]
        shift = mod_ref[0, :, 0:D_MODEL]
        scale = mod_ref[0, :, D_MODEL:2 * D_MODEL]
        return (y * (1.0 + scale) + shift).astype(BF16)

    def store_heads(ref, val, ones):
        for q in range(n_seq):
            for h in range(N_HEADS):
                th = val[q * rows:(q + 1) * rows, h * HEAD_DIM:(h + 1) * HEAD_DIM]
                if ones:
                    th = jnp.concatenate([th, jnp.ones_like(th)], axis=1)
                if for_window_attn:
                    ref[0, h] = th.astype(ref.dtype)
                else:
                    ref[q, layer if ref.shape[1] > 1 else 0, h] = th
        if not for_window_attn and ref.shape[1] > 1:
            for other in range(ref.shape[1]):
                if other != layer:
                    ref[:, other] = jnp.zeros((n_seq,) + ref.shape[2:], ref.dtype)

    def step(h_cur, h_nxt):
        seg = seg_ref[...]
        piece = p32_ref.shape[0] // INPROJ_PIECES
        first_late = len(INPROJ_ORDER) - INPROJ_PIECES
        for idx, j in enumerate(INPROJ_ORDER):
            if idx >= first_late:
                part = slice((idx - first_late) * piece, (idx - first_late + 1) * piece)
                h_nxt[part, :] = normed(x_ref, mod_ref, part)
            if j == G_GATE:
                p16_ref[:, GATE_BLK * LANES:N_P16] = jnp.dot(h_cur[...], wg_ref[0],
                                                             preferred_element_type=F32).astype(BF16)
                continue
            cs = slice(j * W_GRP, (j + 1) * W_GRP)
            y = jnp.dot(h_cur[...], w_ref[0, :, cs], preferred_element_type=F32)
            if j == G_QA:
                y = _head_rms(y, seg, qn_ref[0]) * (HEAD_DIM ** -0.5)
            elif j == G_KA:
                y = _head_rms(y, seg, kn_ref[0])
            if j in P32:
                p32_ref[:, P32[j] * W_GRP:(P32[j] + 1) * W_GRP] = y
            else:
                p16_ref[:, P16[j] * W_GRP:(P16[j] + 1) * W_GRP] = y.astype(BF16)
            if j == G_KA:
                store_heads(k_ref, y, False)
            elif j == G_VA:
                store_heads(v_ref, y, for_window_attn)

    @pl.when(t == 0)
    def _():
        h_even[...] = normed(x_ref, mod_ref)
        for ref in (p32_ref, p16_ref, k_ref, v_ref):
            ref[...] = jnp.zeros(ref.shape, ref.dtype)

    @pl.when(t % 2 == 1)
    def _():
        step(h_even, h_odd)

    @pl.when(jnp.logical_and(t > 0, t % 2 == 0))
    def _():
        step(h_odd, h_even)


def _inproj(x, mod, layer, norm_g, w_in, qn, kn, seg, *, tm, caches=None):
    b, s, _ = x.shape
    for_window_attn = caches is None
    assert s % tm == 0 if for_window_attn else tm % s == 0
    n_tiles = b * s // tm
    tiles_per_seq = max(s // tm, 1)
    n_seq = max(tm // s, 1)
    per_batch = mod.shape[0] > 1
    prep = lambda t: jnp.minimum(t, n_tiles - 1)
    done = lambda t: jnp.maximum(t - 1, 0)
    lay = lambda *dims: pl.BlockSpec((1,) + dims, lambda t: (layer,) + (0,) * len(dims))
    in_specs = [pl.BlockSpec((tm, D_MODEL), lambda t: (prep(t), 0)),
                pl.BlockSpec((1, 1, 3 * D_MODEL),
                             (lambda t: (prep(t) // tiles_per_seq, 0, 0)) if per_batch else (lambda t: (0, 0, 0))),
                lay(1, D_MODEL),
                pl.BlockSpec((1, D_MODEL, 14 * W_GRP), lambda t: (layer, 0, 0)),
                pl.BlockSpec((1, D_MODEL, LANES), lambda t: (layer, 0, 14 * W_GRP // LANES)),
                lay(1, W_GRP), lay(1, W_GRP),
                pl.BlockSpec((W_GRP, W_GRP), lambda t: (0, 0))]
    x2 = x.reshape(b * s, D_MODEL)
    args = [x2, mod, norm_g, w_in, w_in, qn, kn, seg]
    out_shape = [jax.ShapeDtypeStruct((b * s, N_P32), F32), jax.ShapeDtypeStruct((b * s, N_P16), BF16)]
    out_specs = [pl.BlockSpec((tm, N_P32), lambda t: (done(t), 0)), pl.BlockSpec((tm, N_P16), lambda t: (done(t), 0))]
    aliases = {}
    if for_window_attn:
        for width in (HEAD_DIM, 2 * HEAD_DIM):
            out_shape.append(jax.ShapeDtypeStruct((b, N_HEADS, s, width), BF16))
            out_specs.append(pl.BlockSpec((1, N_HEADS, tm, width),
                                          lambda t: (done(t) // tiles_per_seq, 0, done(t) % tiles_per_seq, 0)))
    else:
        for i, cache in enumerate(caches):
            aliases[len(args)] = 2 + i
            args.append(cache)
            in_specs.append(pl.BlockSpec(memory_space=pl.ANY))
            out_shape.append(jax.ShapeDtypeStruct(cache.shape, cache.dtype))
            out_specs.append(pl.BlockSpec((n_seq, 1, N_HEADS, s, HEAD_DIM), lambda t: (done(t), layer, 0, 0, 0)))
        if not caches:
            for _ in range(2):
                out_shape.append(jax.ShapeDtypeStruct((b, DEPTH, N_HEADS, s, HEAD_DIM), F32))
                out_specs.append(pl.BlockSpec((n_seq, DEPTH, N_HEADS, s, HEAD_DIM), lambda t: (done(t), 0, 0, 0, 0)))
    p32, p16, k, v = pl.pallas_call(
        functools.partial(_inproj_kernel, for_window_attn=for_window_attn, n_seq=n_seq, layer=layer),
        grid=(n_tiles + 1,),
        in_specs=in_specs,
        out_specs=out_specs,
        out_shape=out_shape,
        input_output_aliases=aliases,
        scratch_shapes=[pltpu.VMEM((tm, D_MODEL), BF16)] * 2,
        compiler_params=_cparams("arbitrary"),
        name="inproj_grid" if for_window_attn else "inproj_ctx",
    )(*args)
    return p32.reshape(b, s, N_P32), p16.reshape(b, s, N_P16), k, v


CTX_SEQS_PER_STEP = 4


def _attn_ctx_kernel(q_ref, k_ref, v_ref, o_ref):
    n_seq, s, _ = q_ref.shape
    heads = [slice(h * HEAD_DIM, (h + 1) * HEAD_DIM) for h in range(N_HEADS)]
    units = [(b, h) for b in range(n_seq) for h in range(N_HEADS)]
    low_half = lax.broadcasted_iota(jnp.int32, (s, 2 * HEAD_DIM), 1) < HEAD_DIM
    ones = jnp.ones((s, HEAD_DIM), BF16)
    q, k, v = ([ref[b] for b in range(n_seq)] for ref in (q_ref, k_ref, v_ref))
    scores = [_bdot_nt(q[b][:, heads[h]], k[b][:, heads[h]]) for b, h in units]
    e = [jnp.exp(x - jnp.max(x, axis=-1, keepdims=True)).astype(BF16) for x in scores]
    res = [jnp.dot(e[u], jnp.concatenate([v[b][:, heads[h]], ones], axis=1), preferred_element_type=F32)
           for u, (b, h) in enumerate(units)]
    for b in range(n_seq):
        pairs = []
        for h in range(0, N_HEADS, 2):
            even, odd = res[b * N_HEADS + h], res[b * N_HEADS + h + 1]
            pairs.append(jnp.where(low_half, even / pltpu.roll(even, HEAD_DIM, axis=1),
                                   pltpu.roll(odd, HEAD_DIM, axis=1) / odd))
        o_ref[b] = jnp.concatenate(pairs, axis=1)


def _attn_ctx(p):
    b, s, _ = p.shape
    n_seq = CTX_SEQS_PER_STEP
    assert b % n_seq == 0
    col = lambda g: pl.BlockSpec((n_seq, s, W_GRP), lambda i, g=g: (i, 0, g))
    return pl.pallas_call(
        _attn_ctx_kernel,
        grid=(b // n_seq,),
        in_specs=[col(P16[G_QA]), col(P16[G_KA]), col(P16[G_VA])],
        out_specs=pl.BlockSpec((n_seq, s, W_GRP), lambda i: (i, 0, 0)),
        out_shape=jax.ShapeDtypeStruct((b, s, W_GRP), F32),
        compiler_params=_cparams("arbitrary"),
        name="attn_ctx",
    )(p, p, p)


def _bias_kernel(rpb_ref, o_ref):
    lh = pl.program_id(0)
    q = lax.broadcasted_iota(jnp.int32, (GRID_W, GRID_W), 0)
    kc = lax.broadcasted_iota(jnp.int32, (GRID_W, GRID_W), 1)
    dc = jnp.clip(kc - q + (WIN_W - 1), 0, 2 * WIN_W - 2)
    cs = jnp.clip(q - WIN_W // 2, 0, GRID_W - WIN_W)
    in_win = jnp.where(kc >= cs, jnp.where(kc < cs + WIN_W, 1, 0), 0) == 1
    n_dr, n_dc = 2 * WIN_H - 1, 2 * WIN_W - 1
    tiles = []
    for dr0 in range(0, n_dr, WIN_H):
        drs = range(dr0, min(dr0 + WIN_H, n_dr))
        part = [jnp.zeros((GRID_W, GRID_W), F32) for _ in drs]
        for d in range(n_dc):
            hit = dc == d
            part = [jnp.where(hit, rpb_ref[(lh * n_dr + dr) * n_dc + d], t) for dr, t in zip(drs, part)]
        tiles += [jnp.where(in_win, t, NEG_BIG) for t in part]
    for dl in range(WIN_H):
        for i in range(WIN_H):
            o_ref[0, dl, :, i * GRID_W:(i + 1) * GRID_W] = tiles[dl + i]


def _bias_tables(rpb):
    out = pl.pallas_call(
        _bias_kernel,
        grid=(DEPTH * N_HEADS,),
        in_specs=[pl.BlockSpec(memory_space=pltpu.SMEM)],
        out_specs=pl.BlockSpec((1, WIN_H, GRID_W, WIN_H * GRID_W), lambda i: (i, 0, 0, 0)),
        out_shape=jax.ShapeDtypeStruct((DEPTH * N_HEADS, WIN_H, GRID_W, WIN_H * GRID_W), F32),
        compiler_params=_cparams("arbitrary"),
        name="rpb_tables",
    )(rpb.reshape(-1))
    return out.reshape(DEPTH, N_HEADS, WIN_H, GRID_W, WIN_H * GRID_W)


ATTN_ROW_GROUP = 8


def _attn_nbr_kernel(q_ref, k_ref, v_ref, ck_ref, cv_ref, bias_ref, o_ref, *, rows_per_step):
    rb = pl.program_id(1)
    n_rows = k_ref.shape[2] // GRID_W
    kh = min(WIN_H, n_rows)
    n_lat = kh * GRID_W
    heads = [slice(h * HEAD_DIM, (h + 1) * HEAD_DIM) for h in range(N_HEADS)]
    units = [(g, h) for g in range(ATTN_ROW_GROUP) for h in range(N_HEADS)]
    low_half = lax.broadcasted_iota(jnp.int32, (GRID_W, 2 * HEAD_DIM), 1) < HEAD_DIM
    ctx_k = [ck_ref[0, 0, h].astype(BF16) for h in range(N_HEADS)]
    ctx_ones = jnp.ones((cv_ref.shape[3], HEAD_DIM), BF16)
    ctx_v = [jnp.concatenate([cv_ref[0, 0, h].astype(BF16), ctx_ones], axis=1) for h in range(N_HEADS)]

    def rows_body(jj, carry):
        q0, k0, dl, q = [], [], [], []
        for g in range(ATTN_ROW_GROUP):
            j = jj * ATTN_ROW_GROUP + g
            r = rb * rows_per_step + j
            rs = jnp.clip(r - kh // 2, 0, n_rows - kh)
            dl.append(rs - r + (WIN_H - 1))
            k0.append(pl.multiple_of(rs * GRID_W, GRID_W))
            q0.append(pl.multiple_of(j * GRID_W, GRID_W))
            q.append(q_ref[0, pl.ds(q0[g], GRID_W), :].astype(BF16))
        s_lat = [_bdot_nt(q[g][:, heads[h]], k_ref[0, h, pl.ds(k0[g], n_lat), :]) + bias_ref[0, h, dl[g]]
                 for g, h in units]
        q_all = jnp.concatenate(q, axis=0)
        s_ctx_all = [_bdot_nt(q_all[:, hs], ctx_k[h]) for h, hs in enumerate(heads)]
        s_ctx = [s_ctx_all[h][g * GRID_W:(g + 1) * GRID_W] for g, h in units]
        m = [jnp.maximum(jnp.max(a, axis=-1, keepdims=True), jnp.max(c, axis=-1, keepdims=True))
             for a, c in zip(s_lat, s_ctx)]
        e_lat = [jnp.exp(a - mu).astype(BF16) for a, mu in zip(s_lat, m)]
        e_ctx = [jnp.exp(c - mu).astype(BF16) for c, mu in zip(s_ctx, m)]
        pv_ctx = [jnp.dot(jnp.concatenate([e_ctx[g * N_HEADS + h] for g in range(ATTN_ROW_GROUP)], axis=0),
                          ctx_v[h], preferred_element_type=F32) for h in range(N_HEADS)]
        res = [jnp.dot(e_lat[u], v_ref[0, h, pl.ds(k0[g], n_lat), :], preferred_element_type=F32)
               + pv_ctx[h][g * GRID_W:(g + 1) * GRID_W]
               for u, (g, h) in enumerate(units)]
        for g in range(ATTN_ROW_GROUP):
            pairs = []
            for h in range(0, N_HEADS, 2):
                even, odd = res[g * N_HEADS + h], res[g * N_HEADS + h + 1]
                pairs.append(jnp.where(low_half, even / pltpu.roll(even, HEAD_DIM, axis=1),
                                       pltpu.roll(odd, HEAD_DIM, axis=1) / odd))
            o_ref[0, pl.ds(q0[g], GRID_W), :] = jnp.concatenate(pairs, axis=1)
        return carry

    lax.fori_loop(0, rows_per_step // ATTN_ROW_GROUP, rows_body, 0)


def _attn_nbr(p, k_hm, v_hm, cache_k, cache_v, bias, layer, *, rows_per_step=32):
    b, s, _ = p.shape
    tq = rows_per_step * GRID_W
    past = cache_v.shape[3]
    ctx_spec = lambda *dims: pl.BlockSpec((1, 1, N_HEADS) + dims, lambda i, j: (i, layer, 0, 0, 0))
    return pl.pallas_call(
        functools.partial(_attn_nbr_kernel, rows_per_step=rows_per_step),
        grid=(b, s // tq),
        in_specs=[pl.BlockSpec((1, tq, W_GRP), lambda i, j: (i, j, P16[G_QA])),
                  pl.BlockSpec((1, N_HEADS, s, HEAD_DIM), lambda i, j: (i, 0, 0, 0)),
                  pl.BlockSpec((1, N_HEADS, s, 2 * HEAD_DIM), lambda i, j: (i, 0, 0, 0)),
                  ctx_spec(past, HEAD_DIM), ctx_spec(past, HEAD_DIM),
                  pl.BlockSpec((1, N_HEADS, WIN_H, GRID_W, WIN_H * GRID_W), lambda i, j: (layer, 0, 0, 0, 0))],
        out_specs=pl.BlockSpec((1, tq, W_GRP), lambda i, j: (i, j, 0)),
        out_shape=jax.ShapeDtypeStruct((b, s, W_GRP), F32),
        compiler_params=_cparams("arbitrary", "arbitrary"),
        name="attn_nbr",
    )(p, k_hm, v_hm, cache_k, cache_v, bias)


def _fft_ctx_consts(s):
    c, sn = _dft_cos_sin(s)
    cc, sc = _dft_cos_sin(W_GRP // N_FOURIER)
    m1 = np.concatenate([c, -sn], axis=0)
    m2 = np.concatenate([_block_diag(cc, N_FOURIER), _block_diag(sc, N_FOURIER)], axis=0)
    return jnp.asarray(m1, F32), jnp.asarray(m2, F32)


def _fft_ctx_kernel(u_ref, m1_ref, m2_ref, o_ref, *, scale):
    n_seq, s, _ = u_ref.shape
    m1 = m1_ref[...].astype(BF16)
    t = [_bdot(m1, u_ref[b]) for b in range(n_seq)]
    x = jnp.concatenate([jnp.concatenate([tb[:s], tb[s:]], axis=1) for tb in t], axis=0)
    out = _bdot(x, m2_ref[...]) * scale
    for b in range(n_seq):
        o_ref[b] = out[b * s:(b + 1) * s]


def _fft_ctx(p):
    b, s, _ = p.shape
    n_seq = CTX_SEQS_PER_STEP
    assert b % n_seq == 0
    consts = _fft_ctx_consts(s)
    full = lambda a: pl.BlockSpec(a.shape, lambda i: (0,) * a.ndim)
    return pl.pallas_call(
        functools.partial(_fft_ctx_kernel, scale=1.0 / math.sqrt(s * (W_GRP // N_FOURIER))),
        grid=(b // n_seq,),
        in_specs=[pl.BlockSpec((n_seq, s, W_GRP), lambda i: (i, 0, P32[G_UC]))] + [full(a) for a in consts],
        out_specs=pl.BlockSpec((n_seq, s, W_GRP), lambda i: (i, 0, 0)),
        out_shape=jax.ShapeDtypeStruct((b, s, W_GRP), F32),
        compiler_params=_cparams("arbitrary"),
        name="fft_ctx",
    )(p, *consts)


def _fft_grid_consts():
    n = GRID_W
    c, sn = _dft_cos_sin(n)
    cc, sc = _dft_cos_sin(W_GRP // N_FOURIER)
    m1 = np.concatenate([c, -sn], axis=0)
    m2 = np.block([[c, sn], [-sn, c]])
    reps = LANES // (W_GRP // N_FOURIER)
    m3 = np.concatenate([_block_diag(cc, reps), _block_diag(sc, reps)], axis=0)
    k = np.arange(n)
    ang = 2.0 * np.pi * (k[:, None] * k[None, :]) / (n * n)
    twr = np.broadcast_to(np.cos(ang)[:, :, None], (n, n, LANES))
    twi = np.broadcast_to(-np.sin(ang)[:, :, None], (n, n, LANES))
    return tuple(jnp.asarray(m, F32) for m in (m1, m2, m3, twr, twi))


FFT_UNROLL = 8
FFT_PITCH = 72


def _fft_grid_kernel(u_ref, m1_ref, m2_ref, m3_ref, twr_ref, twi_ref, o_ref, tr_scr, ti_scr, *, scale):
    n = GRID_W
    m1, m2, m3 = (r[...].astype(BF16) for r in (m1_ref, m2_ref, m3_ref))

    def stage1(b, carry):
        a_rows = u_ref[0, pl.ds(b, n, stride=n), :]
        t = _bdot(m1, a_rows)
        tr, ti = t[:n], t[n:]
        wr, wi = twr_ref[b], twi_ref[b]
        tr_scr[pl.ds(b, n, stride=FFT_PITCH), :] = tr * wr - ti * wi
        ti_scr[pl.ds(b, n, stride=FFT_PITCH), :] = tr * wi + ti * wr
        return carry

    lax.fori_loop(0, n, stage1, 0, unroll=FFT_UNROLL)

    def stage2(dd, carry):
        xcs = []
        for u in range(FFT_UNROLL):
            r0 = pl.multiple_of((dd * FFT_UNROLL + u) * FFT_PITCH, 8)
            g = jnp.concatenate([tr_scr[pl.ds(r0, n), :], ti_scr[pl.ds(r0, n), :]], axis=0)
            x = _bdot(m2, g)
            xcs.append(jnp.concatenate([x[:n], x[n:]], axis=1))
        out = _bdot(jnp.concatenate(xcs, axis=0), m3) * scale
        for u in range(FFT_UNROLL):
            o_ref[0, pl.ds(dd * FFT_UNROLL + u, n, stride=n), :] = out[u * n:(u + 1) * n]
        return carry

    lax.fori_loop(0, n // FFT_UNROLL, stage2, 0)


def _fft_grid(p):
    b, s, _ = p.shape
    assert s == GRID_W * GRID_W
    consts = _fft_grid_consts()
    full = lambda a: pl.BlockSpec(a.shape, lambda i, j: (0,) * a.ndim)
    blk0 = P32[G_UC] * W_GRP // LANES
    return pl.pallas_call(
        functools.partial(_fft_grid_kernel, scale=1.0 / math.sqrt(s * (W_GRP // N_FOURIER))),
        grid=(b, W_GRP // LANES),
        in_specs=[pl.BlockSpec((1, s, LANES), lambda i, j: (i, 0, blk0 + j))] + [full(a) for a in consts],
        out_specs=pl.BlockSpec((1, s, LANES), lambda i, j: (i, 0, j)),
        out_shape=jax.ShapeDtypeStruct((b, s, W_GRP), F32),
        scratch_shapes=[pltpu.VMEM((GRID_W * FFT_PITCH, LANES), F32)] * 2,
        compiler_params=_cparams("arbitrary", "arbitrary"),
        name="fft_grid",
    )(p, *consts)


GLA_CHUNKS_PER_TRIP = 8


def _gla_kernel(*refs, n_chunks, n_seq, group, zero_init, layer):
    q_ref, k_ref, v_ref, a_ref, up_ref, gb_ref, gn_ref, seg_ref = refs[:8]
    if zero_init:
        o_ref, sfin_ref, qd_scr, st_scr, dec_scr = refs[-5:]
        s0_ref = None
    else:
        s0_ref, o_ref, qd_scr, st_scr, dec_scr = refs[8:]
        sfin_ref = None
    c = CHUNK
    row = lax.broadcasted_iota(jnp.int32, (c, c), 0)
    col = lax.broadcasted_iota(jnp.int32, (c, c), 1)
    heads = [slice(h * HEAD_DIM, (h + 1) * HEAD_DIM) for h in range(N_HEADS)]
    wrow = lax.broadcasted_iota(jnp.int32, (W_GRP, W_GRP), 0)
    wcol = lax.broadcasted_iota(jnp.int32, (W_GRP, W_GRP), 1)
    head_diag = (wrow // HEAD_DIM) == (wcol // HEAD_DIM)

    def blockdiag(x):
        return jnp.where(head_diag, jnp.concatenate([x] * N_HEADS, axis=0), jnp.zeros((), x.dtype))

    trow = lax.broadcasted_iota(jnp.int32, (c, W_GRP), 0)
    tcol = lax.broadcasted_iota(jnp.int32, (c, W_GRP), 1) % c
    keep = [trow >= tcol, trow <= tcol]
    eye_tiled = jnp.where(trow == tcol, 1.0, 0.0).astype(BF16)

    def head_transpose_exact(x):
        return sum(_bdot_nt(eye_tiled, blockdiag(part)) for part in _split3(x))

    sum_ops = [jnp.where(row >= col, 1.0, 0.0).astype(BF16),
               jnp.where(row <= col, 1.0, 0.0).astype(BF16)]
    dirs = (0, 1)

    def chunk_rows(n):
        return pl.ds(pl.multiple_of(n * c, c), c)

    def phase1(i, carry):
        ns = [i * group + g for g in range(group)]
        rows = [chunk_rows(n) for n in ns]
        x_all = _bdot(jnp.concatenate([a_ref[0, r, :] for r in rows], axis=0), up_ref[0]) + gb_ref[0]
        x = [[x_all[g * c:(g + 1) * c, d * W_GRP:(d + 1) * W_GRP] for d in dirs] for g in range(group)]
        la = [[(jnp.minimum(xd, 0.0) - jnp.log(1.0 + jnp.exp(-jnp.abs(xd)))) * (1.0 / GLA_TAU) for xd in xg]
              for xg in x]
        cum = [[sum(jnp.dot(sum_ops[d], part, preferred_element_type=F32) for part in _split3(lg[d]))
                for d in dirs] for lg in la]
        tot_row = [[cg[0][c - 1:c, :], cg[1][0:1, :]] for cg in cum]
        q = [q_ref[0, r, :] * (HEAD_DIM ** -0.5) for r in rows]
        k = [k_ref[0, r, :] for r in rows]
        v_bd = [blockdiag(v_ref[0, r, :].astype(BF16)) for r in rows]
        q_dec = [[(q[g] * jnp.exp(cum[g][d])).astype(BF16) for d in dirs] for g in range(group)]
        k_inv = [[blockdiag((k[g] * jnp.exp(-cum[g][d])).astype(BF16)) for d in dirs] for g in range(group)]
        k_end = [[blockdiag((k[g] * jnp.exp(tot_row[g][d] - cum[g][d])).astype(BF16)) for d in dirs]
                 for g in range(group)]
        scores = [[_bdot_nt(q_dec[g][d], k_inv[g][d]) for d in dirs] for g in range(group)]
        v_t = [_bdot_nt(eye_tiled, vb).astype(BF16) for vb in v_bd]
        upd = [[_bdot(v_t[g], k_end[g][d]) for d in dirs] for g in range(group)]
        att = [[jnp.where(keep[d], scores[g][d], 0.0).astype(BF16) for d in dirs] for g in range(group)]
        for g in range(group):
            both = _bdot(jnp.concatenate(att[g], axis=0), v_bd[g])
            o_ref[0, rows[g], :] = both[:c] + both[c:]
            for d in dirs:
                qd_scr[d, rows[g], :] = q_dec[g][d]
                dec_scr[d, pl.ds(ns[g], 1), :] = jnp.exp(tot_row[g][d])
                st_scr[d, ns[g]] = upd[g][d]
        return carry

    lax.fori_loop(0, n_chunks // group, phase1, 0)

    def init_state(d):
        if zero_init:
            return jnp.zeros((HEAD_DIM, W_GRP), F32)
        return head_transpose_exact(jnp.concatenate([s0_ref[0, 0, d, h] for h in range(N_HEADS)], axis=1))

    per_seq = n_chunks // n_seq
    finals = []
    for q in range(n_seq):
        finals.append([])
        for d in range(2):
            def scan_body(i, st, d=d, q=q):
                n = q * per_seq + (i if d == 0 else per_seq - 1 - i)
                upd = st_scr[d, n]
                st_scr[d, n] = st
                return st * dec_scr[d, pl.ds(n, 1), :] + upd
            finals[q].append(lax.fori_loop(0, per_seq, scan_body, init_state(d)))

    def phase3(i, carry):
        ns = [i * group + g for g in range(group)]
        rows = [chunk_rows(n) for n in ns]
        st_bd = [[blockdiag(st_scr[d, n].astype(BF16)) for d in dirs] for n in ns]
        inter = [[_bdot_nt(qd_scr[d, rows[g], :], st_bd[g][d]) for d in dirs] for g in range(group)]
        o = jnp.concatenate([o_ref[0, rows[g], :] + inter[g][0] + inter[g][1] for g in range(group)], axis=0)
        o = _head_rms(o, seg_ref[...], gn_ref[0])
        for g in range(group):
            o_ref[0, rows[g], :] = o[g * c:(g + 1) * c]
        return carry

    lax.fori_loop(0, n_chunks // group, phase3, 0)
    if sfin_ref is not None:
        for q in range(n_seq):
            for d in range(2):
                s_fin = head_transpose_exact(finals[q][d])
                for h, hs in enumerate(heads):
                    sfin_ref[q, layer if sfin_ref.shape[1] > 1 else 0, d, h] = s_fin[:, hs]
        for other in range(sfin_ref.shape[1]):
            if sfin_ref.shape[1] > 1 and other != layer:
                sfin_ref[:, other] = jnp.zeros((n_seq,) + sfin_ref.shape[2:], sfin_ref.dtype)


def _gla(p32, p16, gate_up, gate_b, gn, seg, layer, state, *, zero_init):
    b, s, _ = p32.shape
    n_seq = max(1, min(b, GLA_CHUNKS_PER_TRIP * CHUNK // s)) if zero_init else 1
    assert b % n_seq == 0
    rows = n_seq * s
    n_chunks = rows // CHUNK
    group = min(GLA_CHUNKS_PER_TRIP, n_chunks)
    assert n_chunks % group == 0
    fold = lambda a: a.reshape(b // n_seq, rows, a.shape[-1])
    col = lambda g: pl.BlockSpec((1, rows, W_GRP), lambda i, g=g: (i, 0, g))
    lay = lambda shape: pl.BlockSpec((1,) + shape, lambda i: (layer,) + (0,) * len(shape))
    in_specs = [col(P32[G_QD]), col(P32[G_KD]), col(P16[G_VD]),
                pl.BlockSpec((1, rows, LANES), lambda i: (i, 0, GATE_BLK)),
                lay((LANES, 2 * W_GRP)), lay((1, 2 * W_GRP)),
                lay((1, W_GRP)), pl.BlockSpec((W_GRP, W_GRP), lambda i: (0, 0))]
    args = [fold(p32), fold(p32), fold(p16), fold(p16), gate_up, gate_b, gn, seg, state]
    state_spec = pl.BlockSpec((n_seq, 1, 2, N_HEADS, HEAD_DIM, HEAD_DIM), lambda i: (i, layer, 0, 0, 0, 0))
    out_specs = [pl.BlockSpec((1, rows, W_GRP), lambda i: (i, 0, 0))]
    out_shape = [jax.ShapeDtypeStruct((b // n_seq, rows, W_GRP), F32)]
    aliases = {}
    if not zero_init:
        in_specs.append(state_spec)
    elif state is None:
        args.pop()
        out_specs.append(pl.BlockSpec((n_seq, DEPTH, 2, N_HEADS, HEAD_DIM, HEAD_DIM), lambda i: (i, 0, 0, 0, 0, 0)))
        out_shape.append(jax.ShapeDtypeStruct((b, DEPTH, 2, N_HEADS, HEAD_DIM, HEAD_DIM), F32))
    else:
        aliases = {len(args) - 1: 1}
        in_specs.append(pl.BlockSpec(memory_space=pl.ANY))
        out_specs.append(state_spec)
        out_shape.append(jax.ShapeDtypeStruct(state.shape, state.dtype))
    o, *rest = pl.pallas_call(
        functools.partial(_gla_kernel, n_chunks=n_chunks, n_seq=n_seq, group=group, zero_init=zero_init,
                          layer=layer),
        grid=(b // n_seq,),
        in_specs=in_specs,
        out_specs=out_specs,
        out_shape=out_shape,
        input_output_aliases=aliases,
        scratch_shapes=[pltpu.VMEM((2, rows, W_GRP), BF16),
                        pltpu.VMEM((2, n_chunks, HEAD_DIM, W_GRP), F32),
                        pltpu.VMEM((2, max(n_chunks, 8), W_GRP), F32)],
        compiler_params=_cparams("arbitrary"),
        name="gla_zero" if zero_init else "gla",
    )(*args)
    return (o.reshape(b, s, W_GRP), *rest)


def _outproj_kernel(x_ref, mod_ref, oa_ref, oc_ref, od_ref, ga_ref, bb_ref, cb_ref, hb_ref, gb_ref,
                    gc_ref, gd_ref, cprev_ref, hprev_ref, cnext_ref, hnext_ref, cw_ref, w_ref, o_ref, *, seq_len):
    tm = x_ref.shape[0]
    f32 = lambda ref: ref[...].astype(F32)
    last = cprev_ref.shape[0] - 1
    u = f32(cb_ref) * f32(hb_ref)
    rid = lax.broadcasted_iota(jnp.int32, u.shape, 0)
    pos = (pl.program_id(0) * tm + rid) % seq_len
    u_prev = jnp.where(rid == 0, f32(cprev_ref)[last:, :] * f32(hprev_ref)[last:, :], pltpu.roll(u, 1, axis=0))
    u_next = jnp.where(rid == tm - 1, f32(cnext_ref)[0:1, :] * f32(hnext_ref)[0:1, :], pltpu.roll(u, tm - 1, axis=0))
    u_prev = jnp.where(pos == 0, 0.0, u_prev)
    u_next = jnp.where(pos == seq_len - 1, 0.0, u_next)
    cw = cw_ref[0]
    conv = u_prev * cw[0:1, :] + u * cw[1:2, :] + u_next * cw[2:3, :]
    y_a = oa_ref[...] * _silu(f32(ga_ref))
    y_b = f32(bb_ref) * conv * _silu(f32(gb_ref))
    y_c = oc_ref[...] * _silu(f32(gc_ref))
    y_d = od_ref[...] * _silu(f32(gd_ref))
    y = jnp.concatenate([y_a, y_b, y_c, y_d], axis=1).astype(BF16)
    gate = mod_ref[0, :, 2 * D_MODEL:3 * D_MODEL]
    o_ref[...] = x_ref[...] + gate * jnp.dot(y, w_ref[0], preferred_element_type=F32)


def _outproj(x, mod, layer, o_a, o_c, o_d, p, conv_w, w_out, *, tm):
    b, s, _ = x.shape
    n = b * s
    per_batch = mod.shape[0] > 1
    assert s % tm == 0 or (tm % s == 0 and not per_batch)
    tiles_per_seq = max(s // tm, 1)
    flat = lambda a: a.reshape(n, a.shape[-1])
    tile = lambda width: pl.BlockSpec((tm, width), lambda t: (t, 0))
    col = lambda g: pl.BlockSpec((tm, W_GRP), lambda t, g=g: (t, g))
    halo = 16
    th = tm // halo
    prev = lambda g: pl.BlockSpec((halo, W_GRP), lambda t, g=g: (jnp.maximum(t * th - 1, 0), g))
    nxt = lambda g: pl.BlockSpec((halo, W_GRP), lambda t, g=g: (jnp.minimum((t + 1) * th, n // halo - 1), g))
    out = pl.pallas_call(
        functools.partial(_outproj_kernel, seq_len=s),
        grid=(n // tm,),
        in_specs=[tile(D_MODEL),
                  pl.BlockSpec((1, 1, 3 * D_MODEL),
                               (lambda t: (t // tiles_per_seq, 0, 0)) if per_batch else (lambda t: (0, 0, 0))),
                  tile(W_GRP), tile(W_GRP), tile(W_GRP),
                  *[col(P16[g]) for g in (G_GA, G_BB, G_CB, G_HB, G_GB, G_GC, G_GD)],
                  prev(P16[G_CB]), prev(P16[G_HB]), nxt(P16[G_CB]), nxt(P16[G_HB]),
                  pl.BlockSpec((1, 3, W_GRP), lambda t: (layer, 0, 0)),
                  pl.BlockSpec((1, D_MODEL, D_MODEL), lambda t: (layer, 0, 0))],
        out_specs=tile(D_MODEL),
        out_shape=jax.ShapeDtypeStruct((n, D_MODEL), F32),
        compiler_params=_cparams("arbitrary"),
        name="outproj",
    )(flat(x), mod, flat(o_a), flat(o_c), flat(o_d), *([flat(p)] * 11), conv_w, w_out)
    return out.reshape(b, s, D_MODEL)


def kernel(x_prompt, x_sample, cache_k, cache_v, state_gla, c, c_ctx, norm_g, w_mod, b_mod, w_in, q_norm_g,
           k_norm_g, rpb, conv_w, gla_up_f, gla_bias_f, gla_up_b, gla_bias_b, gla_norm_g, w_out):
    dec_batch = c.shape[0]
    pad_rows = 8 - (1 + dec_batch) % 8
    cvec = jnp.concatenate([c_ctx[None], c, jnp.zeros((pad_rows, D_MODEL), F32)], axis=0)
    mod = _modulation(cvec, w_mod, b_mod)
    mod_ctx = mod[:, 0:1].reshape(DEPTH, 1, 1, 3 * D_MODEL)
    mod_smp = mod[:, 1:1 + dec_batch].reshape(DEPTH, dec_batch, 1, 3 * D_MODEL)

    w_in_b = jnp.pad(w_in, ((0, 0), (0, 0), (0, N_IN_PAD - N_IN))).astype(BF16)
    w_out_b = w_out.astype(BF16)
    seg = _seg_ones()
    norm_g3 = norm_g.reshape(DEPTH, 1, D_MODEL)
    qn = jnp.tile(q_norm_g, (1, N_HEADS)).reshape(DEPTH, 1, W_GRP)
    kn = jnp.tile(k_norm_g, (1, N_HEADS)).reshape(DEPTH, 1, W_GRP)
    gn = jnp.tile(gla_norm_g, (1, N_HEADS)).reshape(DEPTH, 1, W_GRP)
    gate_up = jnp.zeros((DEPTH, LANES, 2 * W_GRP), F32)
    gate_up = gate_up.at[:, :GATE_RANK, :W_GRP].set(gla_up_f).at[:, GATE_RANK:2 * GATE_RANK, W_GRP:].set(gla_up_b)
    gate_b = jnp.concatenate([gla_bias_f, gla_bias_b], axis=-1).reshape(DEPTH, 1, 2 * W_GRP)
    bias = _bias_tables(rpb)

    x = x_prompt
    batch, seq, _ = x.shape
    caches, new_state = (), None
    for l in range(DEPTH):
        p32, p16, *caches = _inproj(x, mod_ctx[l], l, norm_g3, w_in_b, qn, kn, seg, tm=TOKEN_TILE, caches=tuple(caches))
        o_a = _attn_ctx(p16)
        o_c = _fft_ctx(p32)
        o_d, new_state = _gla(p32, p16, gate_up, gate_b, gn, seg, l, new_state, zero_init=True)
        x = _outproj(x, mod_ctx[l], l, o_a, o_c, o_d, p16, conv_w, w_out_b, tm=OUTPROJ_TILE)
    y_prompt = x

    xs = x_sample
    for l in range(DEPTH):
        p32, p16, k_hm, v_hm = _inproj(xs, mod_smp[l], l, norm_g3, w_in_b, qn, kn, seg, tm=TOKEN_TILE)
        o_a = _attn_nbr(p16, k_hm, v_hm, cache_k, cache_v, bias, l)
        o_c = _fft_grid(p32)
        (o_d,) = _gla(p32, p16, gate_up, gate_b, gn, seg, l, state_gla, zero_init=False)
        xs = _outproj(xs, mod_smp[l], l, o_a, o_c, o_d, p16, conv_w, w_out_b, tm=OUTPROJ_TILE)
    return (y_prompt, xs, caches[0], caches[1], new_state)
```

```python
import functools
import math

import numpy as np
import jax
import jax.numpy as jnp
from jax import lax
from jax.experimental import pallas as pl
from jax.experimental.pallas import tpu as pltpu

F32 = jnp.float32
BF16 = jnp.bfloat16

D_MODEL = 1024
DEPTH = 4
GRID_W = 64
W_GRP = 256
HEAD_DIM = 64
N_HEADS = 4
WIN_H = 8
WIN_W = 16
N_FOURIER = 4
GATE_RANK = 16
GLA_TAU = 16.0
CHUNK = 64
RMS_EPS = 1e-6
N_IN = 14 * W_GRP + 2 * GATE_RANK
LANES = 128
N_IN_PAD = -(-N_IN // LANES) * LANES
G_QA, G_KA, G_VA, G_GA, G_BB, G_CB, G_HB, G_GB, G_UC, G_GC, G_QD, G_KD, G_VD, G_GD = range(14)
F32_GROUPS = (G_UC, G_QD, G_KD)
BF16_GROUPS = (G_QA, G_KA, G_VA, G_VD, G_GA, G_BB, G_CB, G_HB, G_GB, G_GC, G_GD)
P32 = {g: i for i, g in enumerate(F32_GROUPS)}
P16 = {g: i for i, g in enumerate(BF16_GROUPS)}
N_P32 = len(F32_GROUPS) * W_GRP
N_P16 = len(BF16_GROUPS) * W_GRP + LANES
GATE_BLK = len(BF16_GROUPS) * W_GRP // LANES
NEG_BIG = -1e30
VMEM_LIMIT = 56 * 1024 * 1024
TOKEN_TILE = 512
OUTPROJ_TILE = 1024


def _cparams(*sem):
    return pltpu.CompilerParams(dimension_semantics=sem, vmem_limit_bytes=VMEM_LIMIT)


def _bdot(a, b):
    return jnp.dot(a.astype(BF16), b.astype(BF16), preferred_element_type=F32)


def _bdot_nt(a, b):
    return lax.dot_general(a.astype(BF16), b.astype(BF16), (((1,), (1,)), ((), ())),
                           preferred_element_type=F32)


def _split2(x):
    hi = x.astype(BF16)
    lo = (x - hi.astype(F32)).astype(BF16)
    return hi, lo


def _split3(x):
    h1 = x.astype(BF16)
    r = x - h1.astype(F32)
    h2 = r.astype(BF16)
    h3 = (r - h2.astype(F32)).astype(BF16)
    return h1, h2, h3


def _silu(x):
    return x / (1.0 + jnp.exp(-x))


def _head_rms(t, seg, g):
    n = t.shape[0]
    both = jnp.dot(jnp.concatenate(_split2(t * t), axis=0), seg, preferred_element_type=F32)
    ss = both[:n] + both[n:]
    return t * lax.rsqrt(ss * (1.0 / HEAD_DIM) + RMS_EPS) * g


def _dft_cos_sin(n):
    k = np.arange(n)
    ang = 2.0 * np.pi * ((k[:, None] * k[None, :]) % n) / n
    return np.cos(ang), np.sin(ang)


def _block_diag(m, reps):
    n = m.shape[0]
    out = np.zeros((n * reps, n * reps), m.dtype)
    for i in range(reps):
        out[i * n:(i + 1) * n, i * n:(i + 1) * n] = m
    return out


def _seg_ones():
    return jnp.asarray(_block_diag(np.ones((HEAD_DIM, HEAD_DIM)), N_HEADS), BF16)


def _mod_kernel(c_ref, w_ref, b_ref, o_ref):
    o_ref[0] = _bdot(_silu(c_ref[...]), w_ref[0]) + b_ref[0]


def _modulation(cvec, w_mod, b_mod):
    rows = cvec.shape[0]
    return pl.pallas_call(
        _mod_kernel,
        grid=(DEPTH,),
        in_specs=[pl.BlockSpec((rows, D_MODEL), lambda l: (0, 0)),
                  pl.BlockSpec((1, D_MODEL, 3 * D_MODEL), lambda l: (l, 0, 0)),
                  pl.BlockSpec((1, 1, 3 * D_MODEL), lambda l: (l, 0, 0))],
        out_specs=pl.BlockSpec((1, rows, 3 * D_MODEL), lambda l: (l, 0, 0)),
        out_shape=jax.ShapeDtypeStruct((DEPTH, rows, 3 * D_MODEL), F32),
        compiler_params=_cparams("arbitrary"),
        name="adaln_mod",
    )(cvec, w_mod, b_mod.reshape(DEPTH, 1, 3 * D_MODEL))


G_GATE = 14
INPROJ_ORDER = (2, 0, 3, 1, 4, 5, 6, 7, 8, 9, 10, 11, 12, 13, G_GATE)
INPROJ_PIECES = 8


def _inproj_kernel(x_ref, mod_ref, g_ref, w_ref, wg_ref, qn_ref, kn_ref, seg_ref, *rest,
                   for_window_attn, n_seq, layer):
    p32_ref, p16_ref, k_ref, v_ref, h_even, h_odd = rest[-6:]
    t = pl.program_id(0)
    rows = p32_ref.shape[0] // n_seq

    def normed(x_ref, mod_ref, rows=slice(None)):
        x = x_ref[rows, :]
        ms = jnp.mean(x * x, axis=-1, keepdims=True)
        y = x * lax.rsqrt(ms + RMS_EPS) * g_ref[0]
        shift = mod_ref[0, :, 0:D_MODEL]
        scale = mod_ref[0, :, D_MODEL:2 * D_MODEL]
        return (y * (1.0 + scale) + shift).astype(BF16)

    def store_heads(ref, val, ones):
        for q in range(n_seq):
            for h in range(N_HEADS):
                th = val[q * rows:(q + 1) * rows, h * HEAD_DIM:(h + 1) * HEAD_DIM]
                if ones:
                    th = jnp.concatenate([th, jnp.ones_like(th)], axis=1)
                if for_window_attn:
                    ref[0, h] = th.astype(ref.dtype)
                else:
                    ref[q, layer if ref.shape[1] > 1 else 0, h] = th
        if not for_window_attn and ref.shape[1] > 1:
            for other in range(ref.shape[1]):
                if other != layer:
                    ref[:, other] = jnp.zeros((n_seq,) + ref.shape[2:], ref.dtype)

    def step(h_cur, h_nxt):
        seg = seg_ref[...]
        piece = p32_ref.shape[0] // INPROJ_PIECES
        first_late = len(INPROJ_ORDER) - INPROJ_PIECES
        for idx, j in enumerate(INPROJ_ORDER):
            if idx >= first_late:
                part = slice((idx - first_late) * piece, (idx - first_late + 1) * piece)
                h_nxt[part, :] = normed(x_ref, mod_ref, part)
            if j == G_GATE:
                p16_ref[:, GATE_BLK * LANES:N_P16] = jnp.dot(h_cur[...], wg_ref[0],
                                                             preferred_element_type=F32).astype(BF16)
                continue
            cs = slice(j * W_GRP, (j + 1) * W_GRP)
            y = jnp.dot(h_cur[...], w_ref[0, :, cs], preferred_element_type=F32)
            if j == G_QA:
                y = _head_rms(y, seg, qn_ref[0]) * (HEAD_DIM ** -0.5)
            elif j == G_KA:
                y = _head_rms(y, seg, kn_ref[0])
            if j in P32:
                p32_ref[:, P32[j] * W_GRP:(P32[j] + 1) * W_GRP] = y
            else:
                p16_ref[:, P16[j] * W_GRP:(P16[j] + 1) * W_GRP] = y.astype(BF16)
            if j == G_KA:
                store_heads(k_ref, y, False)
            elif j == G_VA:
                store_heads(v_ref, y, for_window_attn)

    @pl.when(t == 0)
    def _():
        h_even[...] = normed(x_ref, mod_ref)
        for ref in (p32_ref, p16_ref, k_ref, v_ref):
            ref[...] = jnp.zeros(ref.shape, ref.dtype)

    @pl.when(t % 2 == 1)
    def _():
        step(h_even, h_odd)

    @pl.when(jnp.logical_and(t > 0, t % 2 == 0))
    def _():
        step(h_odd, h_even)


def _inproj(x, mod, layer, norm_g, w_in, qn, kn, seg, *, tm, caches=None):
    b, s, _ = x.shape
    for_window_attn = caches is None
    assert s % tm == 0 if for_window_attn else tm % s == 0
    n_tiles = b * s // tm
    tiles_per_seq = max(s // tm, 1)
    n_seq = max(tm // s, 1)
    per_batch = mod.shape[0] > 1
    prep = lambda t: jnp.minimum(t, n_tiles - 1)
    done = lambda t: jnp.maximum(t - 1, 0)
    lay = lambda *dims: pl.BlockSpec((1,) + dims, lambda t: (layer,) + (0,) * len(dims))
    in_specs = [pl.BlockSpec((tm, D_MODEL), lambda t: (prep(t), 0)),
                pl.BlockSpec((1, 1, 3 * D_MODEL),
                             (lambda t: (prep(t) // tiles_per_seq, 0, 0)) if per_batch else (lambda t: (0, 0, 0))),
                lay(1, D_MODEL),
                pl.BlockSpec((1, D_MODEL, 14 * W_GRP), lambda t: (layer, 0, 0)),
                pl.BlockSpec((1, D_MODEL, LANES), lambda t: (layer, 0, 14 * W_GRP // LANES)),
                lay(1, W_GRP), lay(1, W_GRP),
                pl.BlockSpec((W_GRP, W_GRP), lambda t: (0, 0))]
    x2 = x.reshape(b * s, D_MODEL)
    args = [x2, mod, norm_g, w_in, w_in, qn, kn, seg]
    out_shape = [jax.ShapeDtypeStruct((b * s, N_P32), F32), jax.ShapeDtypeStruct((b * s, N_P16), BF16)]
    out_specs = [pl.BlockSpec((tm, N_P32), lambda t: (done(t), 0)), pl.BlockSpec((tm, N_P16), lambda t: (done(t), 0))]
    aliases = {}
    if for_window_attn:
        for width in (HEAD_DIM, 2 * HEAD_DIM):
            out_shape.append(jax.ShapeDtypeStruct((b, N_HEADS, s, width), BF16))
            out_specs.append(pl.BlockSpec((1, N_HEADS, tm, width),
                                          lambda t: (done(t) // tiles_per_seq, 0, done(t) % tiles_per_seq, 0)))
    else:
        for i, cache in enumerate(caches):
            aliases[len(args)] = 2 + i
            args.append(cache)
            in_specs.append(pl.BlockSpec(memory_space=pl.ANY))
            out_shape.append(jax.ShapeDtypeStruct(cache.shape, cache.dtype))
            out_specs.append(pl.BlockSpec((n_seq, 1, N_HEADS, s, HEAD_DIM), lambda t: (done(t), layer, 0, 0, 0)))
        if not caches:
            for _ in range(2):
                out_shape.append(jax.ShapeDtypeStruct((b, DEPTH, N_HEADS, s, HEAD_DIM), F32))
                out_specs.append(pl.BlockSpec((n_seq, DEPTH, N_HEADS, s, HEAD_DIM), lambda t: (done(t), 0, 0, 0, 0)))
    p32, p16, k, v = pl.pallas_call(
        functools.partial(_inproj_kernel, for_window_attn=for_window_attn, n_seq=n_seq, layer=layer),
        grid=(n_tiles + 1,),
        in_specs=in_specs,
        out_specs=out_specs,
        out_shape=out_shape,
        input_output_aliases=aliases,
        scratch_shapes=[pltpu.VMEM((tm, D_MODEL), BF16)] * 2,
        compiler_params=_cparams("arbitrary"),
        name="inproj_grid" if for_window_attn else "inproj_ctx",
    )(*args)
    return p32.reshape(b, s, N_P32), p16.reshape(b, s, N_P16), k, v


CTX_SEQS_PER_STEP = 4


def _attn_ctx_kernel(q_ref, k_ref, v_ref, o_ref):
    n_seq, s, _ = q_ref.shape
    heads = [slice(h * HEAD_DIM, (h + 1) * HEAD_DIM) for h in range(N_HEADS)]
    units = [(b, h) for b in range(n_seq) for h in range(N_HEADS)]
    low_half = lax.broadcasted_iota(jnp.int32, (s, 2 * HEAD_DIM), 1) < HEAD_DIM
    ones = jnp.ones((s, HEAD_DIM), BF16)
    q, k, v = ([ref[b] for b in range(n_seq)] for ref in (q_ref, k_ref, v_ref))
    scores = [_bdot_nt(q[b][:, heads[h]], k[b][:, heads[h]]) for b, h in units]
    e = [jnp.exp(x - jnp.max(x, axis=-1, keepdims=True)).astype(BF16) for x in scores]
    res = [jnp.dot(e[u], jnp.concatenate([v[b][:, heads[h]], ones], axis=1), preferred_element_type=F32)
           for u, (b, h) in enumerate(units)]
    for b in range(n_seq):
        pairs = []
        for h in range(0, N_HEADS, 2):
            even, odd = res[b * N_HEADS + h], res[b * N_HEADS + h + 1]
            pairs.append(jnp.where(low_half, even / pltpu.roll(even, HEAD_DIM, axis=1),
                                   pltpu.roll(odd, HEAD_DIM, axis=1) / odd))
        o_ref[b] = jnp.concatenate(pairs, axis=1)


def _attn_ctx(p):
    b, s, _ = p.shape
    n_seq = CTX_SEQS_PER_STEP
    assert b % n_seq == 0
    col = lambda g: pl.BlockSpec((n_seq, s, W_GRP), lambda i, g=g: (i, 0, g))
    return pl.pallas_call(
        _attn_ctx_kernel,
        grid=(b // n_seq,),
        in_specs=[col(P16[G_QA]), col(P16[G_KA]), col(P16[G_VA])],
        out_specs=pl.BlockSpec((n_seq, s, W_GRP), lambda i: (i, 0, 0)),
        out_shape=jax.ShapeDtypeStruct((b, s, W_GRP), F32),
        compiler_params=_cparams("arbitrary"),
        name="attn_ctx",
    )(p, p, p)


def _bias_kernel(rpb_ref, o_ref):
    lh = pl.program_id(0)
    q = lax.broadcasted_iota(jnp.int32, (GRID_W, GRID_W), 0)
    kc = lax.broadcasted_iota(jnp.int32, (GRID_W, GRID_W), 1)
    dc = jnp.clip(kc - q + (WIN_W - 1), 0, 2 * WIN_W - 2)
    cs = jnp.clip(q - WIN_W // 2, 0, GRID_W - WIN_W)
    in_win = jnp.where(kc >= cs, jnp.where(kc < cs + WIN_W, 1, 0), 0) == 1
    n_dr, n_dc = 2 * WIN_H - 1, 2 * WIN_W - 1
    tiles = []
    for dr0 in range(0, n_dr, WIN_H):
        drs = range(dr0, min(dr0 + WIN_H, n_dr))
        part = [jnp.zeros((GRID_W, GRID_W), F32) for _ in drs]
        for d in range(n_dc):
            hit = dc == d
            part = [jnp.where(hit, rpb_ref[(lh * n_dr + dr) * n_dc + d], t) for dr, t in zip(drs, part)]
        tiles += [jnp.where(in_win, t, NEG_BIG) for t in part]
    for dl in range(WIN_H):
        for i in range(WIN_H):
            o_ref[0, dl, :, i * GRID_W:(i + 1) * GRID_W] = tiles[dl + i]


def _bias_tables(rpb):
    out = pl.pallas_call(
        _bias_kernel,
        grid=(DEPTH * N_HEADS,),
        in_specs=[pl.BlockSpec(memory_space=pltpu.SMEM)],
        out_specs=pl.BlockSpec((1, WIN_H, GRID_W, WIN_H * GRID_W), lambda i: (i, 0, 0, 0)),
        out_shape=jax.ShapeDtypeStruct((DEPTH * N_HEADS, WIN_H, GRID_W, WIN_H * GRID_W), F32),
        compiler_params=_cparams("arbitrary"),
        name="rpb_tables",
    )(rpb.reshape(-1))
    return out.reshape(DEPTH, N_HEADS, WIN_H, GRID_W, WIN_H * GRID_W)


ATTN_ROW_GROUP = 8


def _attn_nbr_kernel(q_ref, k_ref, v_ref, ck_ref, cv_ref, bias_ref, o_ref, *, rows_per_step):
    rb = pl.program_id(1)
    n_rows = k_ref.shape[2] // GRID_W
    kh = min(WIN_H, n_rows)
    n_lat = kh * GRID_W
    heads = [slice(h * HEAD_DIM, (h + 1) * HEAD_DIM) for h in range(N_HEADS)]
    units = [(g, h) for g in range(ATTN_ROW_GROUP) for h in range(N_HEADS)]
    low_half = lax.broadcasted_iota(jnp.int32, (GRID_W, 2 * HEAD_DIM), 1) < HEAD_DIM
    ctx_k = [ck_ref[0, 0, h].astype(BF16) for h in range(N_HEADS)]
    ctx_ones = jnp.ones((cv_ref.shape[3], HEAD_DIM), BF16)
    ctx_v = [jnp.concatenate([cv_ref[0, 0, h].astype(BF16), ctx_ones], axis=1) for h in range(N_HEADS)]

    def rows_body(jj, carry):
        q0, k0, dl, q = [], [], [], []
        for g in range(ATTN_ROW_GROUP):
            j = jj * ATTN_ROW_GROUP + g
            r = rb * rows_per_step + j
            rs = jnp.clip(r - kh // 2, 0, n_rows - kh)
            dl.append(rs - r + (WIN_H - 1))
            k0.append(pl.multiple_of(rs * GRID_W, GRID_W))
            q0.append(pl.multiple_of(j * GRID_W, GRID_W))
            q.append(q_ref[0, pl.ds(q0[g], GRID_W), :].astype(BF16))
        s_lat = [_bdot_nt(q[g][:, heads[h]], k_ref[0, h, pl.ds(k0[g], n_lat), :]) + bias_ref[0, h, dl[g]]
                 for g, h in units]
        q_all = jnp.concatenate(q, axis=0)
        s_ctx_all = [_bdot_nt(q_all[:, hs], ctx_k[h]) for h, hs in enumerate(heads)]
        s_ctx = [s_ctx_all[h][g * GRID_W:(g + 1) * GRID_W] for g, h in units]
        m = [jnp.maximum(jnp.max(a, axis=-1, keepdims=True), jnp.max(c, axis=-1, keepdims=True))
             for a, c in zip(s_lat, s_ctx)]
        e_lat = [jnp.exp(a - mu).astype(BF16) for a, mu in zip(s_lat, m)]
        e_ctx = [jnp.exp(c - mu).astype(BF16) for c, mu in zip(s_ctx, m)]
        pv_ctx = [jnp.dot(jnp.concatenate([e_ctx[g * N_HEADS + h] for g in range(ATTN_ROW_GROUP)], axis=0),
                          ctx_v[h], preferred_element_type=F32) for h in range(N_HEADS)]
        res = [jnp.dot(e_lat[u], v_ref[0, h, pl.ds(k0[g], n_lat), :], preferred_element_type=F32)
               + pv_ctx[h][g * GRID_W:(g + 1) * GRID_W]
               for u, (g, h) in enumerate(units)]
        for g in range(ATTN_ROW_GROUP):
            pairs = []
            for h in range(0, N_HEADS, 2):
                even, odd = res[g * N_HEADS + h], res[g * N_HEADS + h + 1]
                pairs.append(jnp.where(low_half, even / pltpu.roll(even, HEAD_DIM, axis=1),
                                       pltpu.roll(odd, HEAD_DIM, axis=1) / odd))
            o_ref[0, pl.ds(q0[g], GRID_W), :] = jnp.concatenate(pairs, axis=1)
        return carry

    lax.fori_loop(0, rows_per_step // ATTN_ROW_GROUP, rows_body, 0)


def _attn_nbr(p, k_hm, v_hm, cache_k, cache_v, bias, layer, *, rows_per_step=32):
    b, s, _ = p.shape
    tq = rows_per_step * GRID_W
    past = cache_v.shape[3]
    ctx_spec = lambda *dims: pl.BlockSpec((1, 1, N_HEADS) + dims, lambda i, j: (i, layer, 0, 0, 0))
    return pl.pallas_call(
        functools.partial(_attn_nbr_kernel, rows_per_step=rows_per_step),
        grid=(b, s // tq),
        in_specs=[pl.BlockSpec((1, tq, W_GRP), lambda i, j: (i, j, P16[G_QA])),
                  pl.BlockSpec((1, N_HEADS, s, HEAD_DIM), lambda i, j: (i, 0, 0, 0)),
                  pl.BlockSpec((1, N_HEADS, s, 2 * HEAD_DIM), lambda i, j: (i, 0, 0, 0)),
                  ctx_spec(past, HEAD_DIM), ctx_spec(past, HEAD_DIM),
                  pl.BlockSpec((1, N_HEADS, WIN_H, GRID_W, WIN_H * GRID_W), lambda i, j: (layer, 0, 0, 0, 0))],
        out_specs=pl.BlockSpec((1, tq, W_GRP), lambda i, j: (i, j, 0)),
        out_shape=jax.ShapeDtypeStruct((b, s, W_GRP), F32),
        compiler_params=_cparams("arbitrary", "arbitrary"),
        name="attn_nbr",
    )(p, k_hm, v_hm, cache_k, cache_v, bias)


def _fft_ctx_consts(s):
    c, sn = _dft_cos_sin(s)
    cc, sc = _dft_cos_sin(W_GRP // N_FOURIER)
    m1 = np.concatenate([c, -sn], axis=0)
    m2 = np.concatenate([_block_diag(cc, N_FOURIER), _block_diag(sc, N_FOURIER)], axis=0)
    return jnp.asarray(m1, F32), jnp.asarray(m2, F32)


def _fft_ctx_kernel(u_ref, m1_ref, m2_ref, o_ref, *, scale):
    n_seq, s, _ = u_ref.shape
    m1 = m1_ref[...].astype(BF16)
    t = [_bdot(m1, u_ref[b]) for b in range(n_seq)]
    x = jnp.concatenate([jnp.concatenate([tb[:s], tb[s:]], axis=1) for tb in t], axis=0)
    out = _bdot(x, m2_ref[...]) * scale
    for b in range(n_seq):
        o_ref[b] = out[b * s:(b + 1) * s]


def _fft_ctx(p):
    b, s, _ = p.shape
    n_seq = CTX_SEQS_PER_STEP
    assert b % n_seq == 0
    consts = _fft_ctx_consts(s)
    full = lambda a: pl.BlockSpec(a.shape, lambda i: (0,) * a.ndim)
    return pl.pallas_call(
        functools.partial(_fft_ctx_kernel, scale=1.0 / math.sqrt(s * (W_GRP // N_FOURIER))),
        grid=(b // n_seq,),
        in_specs=[pl.BlockSpec((n_seq, s, W_GRP), lambda i: (i, 0, P32[G_UC]))] + [full(a) for a in consts],
        out_specs=pl.BlockSpec((n_seq, s, W_GRP), lambda i: (i, 0, 0)),
        out_shape=jax.ShapeDtypeStruct((b, s, W_GRP), F32),
        compiler_params=_cparams("arbitrary"),
        name="fft_ctx",
    )(p, *consts)


def _fft_grid_consts():
    n = GRID_W
    c, sn = _dft_cos_sin(n)
    cc, sc = _dft_cos_sin(W_GRP // N_FOURIER)
    m1 = np.concatenate([c, -sn], axis=0)
    m2 = np.block([[c, sn], [-sn, c]])
    reps = LANES // (W_GRP // N_FOURIER)
    m3 = np.concatenate([_block_diag(cc, reps), _block_diag(sc, reps)], axis=0)
    k = np.arange(n)
    ang = 2.0 * np.pi * (k[:, None] * k[None, :]) / (n * n)
    twr = np.broadcast_to(np.cos(ang)[:, :, None], (n, n, LANES))
    twi = np.broadcast_to(-np.sin(ang)[:, :, None], (n, n, LANES))
    return tuple(jnp.asarray(m, F32) for m in (m1, m2, m3, twr, twi))


FFT_UNROLL = 8
FFT_PITCH = 72


def _fft_grid_kernel(u_ref, m1_ref, m2_ref, m3_ref, twr_ref, twi_ref, o_ref, tr_scr, ti_scr, *, scale):
    n = GRID_W
    m1, m2, m3 = (r[...].astype(BF16) for r in (m1_ref, m2_ref, m3_ref))

    def stage1(b, carry):
        a_rows = u_ref[0, pl.ds(b, n, stride=n), :]
        t = _bdot(m1, a_rows)
        tr, ti = t[:n], t[n:]
        wr, wi = twr_ref[b], twi_ref[b]
        tr_scr[pl.ds(b, n, stride=FFT_PITCH), :] = tr * wr - ti * wi
        ti_scr[pl.ds(b, n, stride=FFT_PITCH), :] = tr * wi + ti * wr
        return carry

    lax.fori_loop(0, n, stage1, 0, unroll=FFT_UNROLL)

    def stage2(dd, carry):
        xcs = []
        for u in range(FFT_UNROLL):
            r0 = pl.multiple_of((dd * FFT_UNROLL + u) * FFT_PITCH, 8)
            g = jnp.concatenate([tr_scr[pl.ds(r0, n), :], ti_scr[pl.ds(r0, n), :]], axis=0)
            x = _bdot(m2, g)
            xcs.append(jnp.concatenate([x[:n], x[n:]], axis=1))
        out = _bdot(jnp.concatenate(xcs, axis=0), m3) * scale
        for u in range(FFT_UNROLL):
            o_ref[0, pl.ds(dd * FFT_UNROLL + u, n, stride=n), :] = out[u * n:(u + 1) * n]
        return carry

    lax.fori_loop(0, n // FFT_UNROLL, stage2, 0)


def _fft_grid(p):
    b, s, _ = p.shape
    assert s == GRID_W * GRID_W
    consts = _fft_grid_consts()
    full = lambda a: pl.BlockSpec(a.shape, lambda i, j: (0,) * a.ndim)
    blk0 = P32[G_UC] * W_GRP // LANES
    return pl.pallas_call(
        functools.partial(_fft_grid_kernel, scale=1.0 / math.sqrt(s * (W_GRP // N_FOURIER))),
        grid=(b, W_GRP // LANES),
        in_specs=[pl.BlockSpec((1, s, LANES), lambda i, j: (i, 0, blk0 + j))] + [full(a) for a in consts],
        out_specs=pl.BlockSpec((1, s, LANES), lambda i, j: (i, 0, j)),
        out_shape=jax.ShapeDtypeStruct((b, s, W_GRP), F32),
        scratch_shapes=[pltpu.VMEM((GRID_W * FFT_PITCH, LANES), F32)] * 2,
        compiler_params=_cparams("arbitrary", "arbitrary"),
        name="fft_grid",
    )(p, *consts)


GLA_CHUNKS_PER_TRIP = 16


def _gla_kernel(*refs, n_chunks, n_seq, group, zero_init, layer):
    q_ref, k_ref, v_ref, a_ref, up_ref, gb_ref, gn_ref, seg_ref = refs[:8]
    if zero_init:
        o_ref, sfin_ref, qd_scr, st_scr, dec_scr = refs[-5:]
        s0_ref = None
    else:
        s0_ref, o_ref, qd_scr, st_scr, dec_scr = refs[8:]
        sfin_ref = None
    c = CHUNK
    row = lax.broadcasted_iota(jnp.int32, (c, c), 0)
    col = lax.broadcasted_iota(jnp.int32, (c, c), 1)
    heads = [slice(h * HEAD_DIM, (h + 1) * HEAD_DIM) for h in range(N_HEADS)]
    wrow = lax.broadcasted_iota(jnp.int32, (W_GRP, W_GRP), 0)
    wcol = lax.broadcasted_iota(jnp.int32, (W_GRP, W_GRP), 1)
    head_diag = (wrow // HEAD_DIM) == (wcol // HEAD_DIM)

    def blockdiag(x):
        return jnp.where(head_diag, jnp.concatenate([x] * N_HEADS, axis=0), jnp.zeros((), x.dtype))

    trow = lax.broadcasted_iota(jnp.int32, (c, W_GRP), 0)
    tcol = lax.broadcasted_iota(jnp.int32, (c, W_GRP), 1) % c
    keep = [trow >= tcol, trow <= tcol]
    eye_tiled = jnp.where(trow == tcol, 1.0, 0.0).astype(BF16)

    def head_transpose_exact(x):
        return sum(_bdot_nt(eye_tiled, blockdiag(part)) for part in _split3(x))

    sum_ops = [jnp.where(row >= col, 1.0, 0.0).astype(BF16),
               jnp.where(row <= col, 1.0, 0.0).astype(BF16)]
    dirs = (0, 1)

    def chunk_rows(n):
        return pl.ds(pl.multiple_of(n * c, c), c)

    def phase1(i, carry):
        ns = [i * group + g for g in range(group)]
        rows = [chunk_rows(n) for n in ns]
        x_all = _bdot(jnp.concatenate([a_ref[0, r, :] for r in rows], axis=0), up_ref[0]) + gb_ref[0]
        x = [[x_all[g * c:(g + 1) * c, d * W_GRP:(d + 1) * W_GRP] for d in dirs] for g in range(group)]
        la = [[(jnp.minimum(xd, 0.0) - jnp.log(1.0 + jnp.exp(-jnp.abs(xd)))) * (1.0 / GLA_TAU) for xd in xg]
              for xg in x]
        cum = [[sum(jnp.dot(sum_ops[d], part, preferred_element_type=F32) for part in _split3(lg[d]))
                for d in dirs] for lg in la]
        tot_row = [[cg[0][c - 1:c, :], cg[1][0:1, :]] for cg in cum]
        q = [q_ref[0, r, :] * (HEAD_DIM ** -0.5) for r in rows]
        k = [k_ref[0, r, :] for r in rows]
        v_bd = [blockdiag(v_ref[0, r, :].astype(BF16)) for r in rows]
        q_dec = [[(q[g] * jnp.exp(cum[g][d])).astype(BF16) for d in dirs] for g in range(group)]
        k_inv = [[blockdiag((k[g] * jnp.exp(-cum[g][d])).astype(BF16)) for d in dirs] for g in range(group)]
        k_end = [[blockdiag((k[g] * jnp.exp(tot_row[g][d] - cum[g][d])).astype(BF16)) for d in dirs]
                 for g in range(group)]
        scores = [[_bdot_nt(q_dec[g][d], k_inv[g][d]) for d in dirs] for g in range(group)]
        v_t = [_bdot_nt(eye_tiled, vb).astype(BF16) for vb in v_bd]
        upd = [[_bdot(v_t[g], k_end[g][d]) for d in dirs] for g in range(group)]
        att = [[jnp.where(keep[d], scores[g][d], 0.0).astype(BF16) for d in dirs] for g in range(group)]
        for g in range(group):
            both = _bdot(jnp.concatenate(att[g], axis=0), v_bd[g])
            o_ref[0, rows[g], :] = both[:c] + both[c:]
            for d in dirs:
                qd_scr[d, rows[g], :] = q_dec[g][d]
                dec_scr[d, pl.ds(ns[g], 1), :] = jnp.exp(tot_row[g][d])
                st_scr[d, ns[g]] = upd[g][d]
        return carry

    lax.fori_loop(0, n_chunks // group, phase1, 0)

    def init_state(d):
        if zero_init:
            return jnp.zeros((HEAD_DIM, W_GRP), F32)
        return head_transpose_exact(jnp.concatenate([s0_ref[0, 0, d, h] for h in range(N_HEADS)], axis=1))

    per_seq = n_chunks // n_seq
    finals = []
    for q in range(n_seq):
        finals.append([])
        for d in range(2):
            def scan_body(i, st, d=d, q=q):
                n = q * per_seq + (i if d == 0 else per_seq - 1 - i)
                upd = st_scr[d, n]
                st_scr[d, n] = st
                return st * dec_scr[d, pl.ds(n, 1), :] + upd
            finals[q].append(lax.fori_loop(0, per_seq, scan_body, init_state(d)))

    def phase3(i, carry):
        ns = [i * group + g for g in range(group)]
        rows = [chunk_rows(n) for n in ns]
        st_bd = [[blockdiag(st_scr[d, n].astype(BF16)) for d in dirs] for n in ns]
        inter = [[_bdot_nt(qd_scr[d, rows[g], :], st_bd[g][d]) for d in dirs] for g in range(group)]
        o = jnp.concatenate([o_ref[0, rows[g], :] + inter[g][0] + inter[g][1] for g in range(group)], axis=0)
        o = _head_rms(o, seg_ref[...], gn_ref[0])
        for g in range(group):
            o_ref[0, rows[g], :] = o[g * c:(g + 1) * c]
        return carry

    lax.fori_loop(0, n_chunks // group, phase3, 0)
    if sfin_ref is not None:
        for q in range(n_seq):
            for d in range(2):
                s_fin = head_transpose_exact(finals[q][d])
                for h, hs in enumerate(heads):
                    sfin_ref[q, layer if sfin_ref.shape[1] > 1 else 0, d, h] = s_fin[:, hs]
        for other in range(sfin_ref.shape[1]):
            if sfin_ref.shape[1] > 1 and other != layer:
                sfin_ref[:, other] = jnp.zeros((n_seq,) + sfin_ref.shape[2:], sfin_ref.dtype)


def _gla(p32, p16, gate_up, gate_b, gn, seg, layer, state, *, zero_init):
    b, s, _ = p32.shape
    n_seq = max(1, min(b, GLA_CHUNKS_PER_TRIP * CHUNK // s)) if zero_init else 1
    assert b % n_seq == 0
    rows = n_seq * s
    n_chunks = rows // CHUNK
    group = min(GLA_CHUNKS_PER_TRIP, n_chunks)
    assert n_chunks % group == 0
    fold = lambda a: a.reshape(b // n_seq, rows, a.shape[-1])
    col = lambda g: pl.BlockSpec((1, rows, W_GRP), lambda i, g=g: (i, 0, g))
    lay = lambda shape: pl.BlockSpec((1,) + shape, lambda i: (layer,) + (0,) * len(shape))
    in_specs = [col(P32[G_QD]), col(P32[G_KD]), col(P16[G_VD]),
                pl.BlockSpec((1, rows, LANES), lambda i: (i, 0, GATE_BLK)),
                lay((LANES, 2 * W_GRP)), lay((1, 2 * W_GRP)),
                lay((1, W_GRP)), pl.BlockSpec((W_GRP, W_GRP), lambda i: (0, 0))]
    args = [fold(p32), fold(p32), fold(p16), fold(p16), gate_up, gate_b, gn, seg, state]
    state_spec = pl.BlockSpec((n_seq, 1, 2, N_HEADS, HEAD_DIM, HEAD_DIM), lambda i: (i, layer, 0, 0, 0, 0))
    out_specs = [pl.BlockSpec((1, rows, W_GRP), lambda i: (i, 0, 0))]
    out_shape = [jax.ShapeDtypeStruct((b // n_seq, rows, W_GRP), F32)]
    aliases = {}
    if not zero_init:
        in_specs.append(state_spec)
    elif state is None:
        args.pop()
        out_specs.append(pl.BlockSpec((n_seq, DEPTH, 2, N_HEADS, HEAD_DIM, HEAD_DIM), lambda i: (i, 0, 0, 0, 0, 0)))
        out_shape.append(jax.ShapeDtypeStruct((b, DEPTH, 2, N_HEADS, HEAD_DIM, HEAD_DIM), F32))
    else:
        aliases = {len(args) - 1: 1}
        in_specs.append(pl.BlockSpec(memory_space=pl.ANY))
        out_specs.append(state_spec)
        out_shape.append(jax.ShapeDtypeStruct(state.shape, state.dtype))
    o, *rest = pl.pallas_call(
        functools.partial(_gla_kernel, n_chunks=n_chunks, n_seq=n_seq, group=group, zero_init=zero_init,
                          layer=layer),
        grid=(b // n_seq,),
        in_specs=in_specs,
        out_specs=out_specs,
        out_shape=out_shape,
        input_output_aliases=aliases,
        scratch_shapes=[pltpu.VMEM((2, rows, W_GRP), BF16),
                        pltpu.VMEM((2, n_chunks, HEAD_DIM, W_GRP), F32),
                        pltpu.VMEM((2, max(n_chunks, 8), W_GRP), F32)],
        compiler_params=_cparams("arbitrary"),
        name="gla_zero" if zero_init else "gla",
    )(*args)
    return (o.reshape(b, s, W_GRP), *rest)


def _outproj_kernel(x_ref, mod_ref, oa_ref, oc_ref, od_ref, ga_ref, bb_ref, cb_ref, hb_ref, gb_ref,
                    gc_ref, gd_ref, cprev_ref, hprev_ref, cnext_ref, hnext_ref, cw_ref, w_ref, o_ref, *, seq_len):
    tm = x_ref.shape[0]
    f32 = lambda ref: ref[...].astype(F32)
    last = cprev_ref.shape[0] - 1
    u = f32(cb_ref) * f32(hb_ref)
    rid = lax.broadcasted_iota(jnp.int32, u.shape, 0)
    pos = (pl.program_id(0) * tm + rid) % seq_len
    u_prev = jnp.where(rid == 0, f32(cprev_ref)[last:, :] * f32(hprev_ref)[last:, :], pltpu.roll(u, 1, axis=0))
    u_next = jnp.where(rid == tm - 1, f32(cnext_ref)[0:1, :] * f32(hnext_ref)[0:1, :], pltpu.roll(u, tm - 1, axis=0))
    u_prev = jnp.where(pos == 0, 0.0, u_prev)
    u_next = jnp.where(pos == seq_len - 1, 0.0, u_next)
    cw = cw_ref[0]
    conv = u_prev * cw[0:1, :] + u * cw[1:2, :] + u_next * cw[2:3, :]
    y_a = oa_ref[...] * _silu(f32(ga_ref))
    y_b = f32(bb_ref) * conv * _silu(f32(gb_ref))
    y_c = oc_ref[...] * _silu(f32(gc_ref))
    y_d = od_ref[...] * _silu(f32(gd_ref))
    y = jnp.concatenate([y_a, y_b, y_c, y_d], axis=1).astype(BF16)
    gate = mod_ref[0, :, 2 * D_MODEL:3 * D_MODEL]
    o_ref[...] = x_ref[...] + gate * jnp.dot(y, w_ref[0], preferred_element_type=F32)


def _outproj(x, mod, layer, o_a, o_c, o_d, p, conv_w, w_out, *, tm):
    b, s, _ = x.shape
    n = b * s
    per_batch = mod.shape[0] > 1
    assert s % tm == 0 or (tm % s == 0 and not per_batch)
    tiles_per_seq = max(s // tm, 1)
    flat = lambda a: a.reshape(n, a.shape[-1])
    tile = lambda width: pl.BlockSpec((tm, width), lambda t: (t, 0))
    col = lambda g: pl.BlockSpec((tm, W_GRP), lambda t, g=g: (t, g))
    halo = 16
    th = tm // halo
    prev = lambda g: pl.BlockSpec((halo, W_GRP), lambda t, g=g: (jnp.maximum(t * th - 1, 0), g))
    nxt = lambda g: pl.BlockSpec((halo, W_GRP), lambda t, g=g: (jnp.minimum((t + 1) * th, n // halo - 1), g))
    out = pl.pallas_call(
        functools.partial(_outproj_kernel, seq_len=s),
        grid=(n // tm,),
        in_specs=[tile(D_MODEL),
                  pl.BlockSpec((1, 1, 3 * D_MODEL),
                               (lambda t: (t // tiles_per_seq, 0, 0)) if per_batch else (lambda t: (0, 0, 0))),
                  tile(W_GRP), tile(W_GRP), tile(W_GRP),
                  *[col(P16[g]) for g in (G_GA, G_BB, G_CB, G_HB, G_GB, G_GC, G_GD)],
                  prev(P16[G_CB]), prev(P16[G_HB]), nxt(P16[G_CB]), nxt(P16[G_HB]),
                  pl.BlockSpec((1, 3, W_GRP), lambda t: (layer, 0, 0)),
                  pl.BlockSpec((1, D_MODEL, D_MODEL), lambda t: (layer, 0, 0))],
        out_specs=tile(D_MODEL),
        out_shape=jax.ShapeDtypeStruct((n, D_MODEL), F32),
        compiler_params=_cparams("arbitrary"),
        name="outproj",
    )(flat(x), mod, flat(o_a), flat(o_c), flat(o_d), *([flat(p)] * 11), conv_w, w_out)
    return out.reshape(b, s, D_MODEL)


def kernel(x_prompt, x_sample, cache_k, cache_v, state_gla, c, c_ctx, norm_g, w_mod, b_mod, w_in, q_norm_g,
           k_norm_g, rpb, conv_w, gla_up_f, gla_bias_f, gla_up_b, gla_bias_b, gla_norm_g, w_out):
    dec_batch = c.shape[0]
    pad_rows = 8 - (1 + dec_batch) % 8
    cvec = jnp.concatenate([c_ctx[None], c, jnp.zeros((pad_rows, D_MODEL), F32)], axis=0)
    mod = _modulation(cvec, w_mod, b_mod)
    mod_ctx = mod[:, 0:1].reshape(DEPTH, 1, 1, 3 * D_MODEL)
    mod_smp = mod[:, 1:1 + dec_batch].reshape(DEPTH, dec_batch, 1, 3 * D_MODEL)

    w_in_b = jnp.pad(w_in, ((0, 0), (0, 0), (0, N_IN_PAD - N_IN))).astype(BF16)
    w_out_b = w_out.astype(BF16)
    seg = _seg_ones()
    norm_g3 = norm_g.reshape(DEPTH, 1, D_MODEL)
    qn = jnp.tile(q_norm_g, (1, N_HEADS)).reshape(DEPTH, 1, W_GRP)
    kn = jnp.tile(k_norm_g, (1, N_HEADS)).reshape(DEPTH, 1, W_GRP)
    gn = jnp.tile(gla_norm_g, (1, N_HEADS)).reshape(DEPTH, 1, W_GRP)
    gate_up = jnp.zeros((DEPTH, LANES, 2 * W_GRP), F32)
    gate_up = gate_up.at[:, :GATE_RANK, :W_GRP].set(gla_up_f).at[:, GATE_RANK:2 * GATE_RANK, W_GRP:].set(gla_up_b)
    gate_b = jnp.concatenate([gla_bias_f, gla_bias_b], axis=-1).reshape(DEPTH, 1, 2 * W_GRP)
    bias = _bias_tables(rpb)

    x = x_prompt
    batch, seq, _ = x.shape
    caches, new_state = (), None
    for l in range(DEPTH):
        p32, p16, *caches = _inproj(x, mod_ctx[l], l, norm_g3, w_in_b, qn, kn, seg, tm=TOKEN_TILE, caches=tuple(caches))
        o_a = _attn_ctx(p16)
        o_c = _fft_ctx(p32)
        o_d, new_state = _gla(p32, p16, gate_up, gate_b, gn, seg, l, new_state, zero_init=True)
        x = _outproj(x, mod_ctx[l], l, o_a, o_c, o_d, p16, conv_w, w_out_b, tm=OUTPROJ_TILE)
    y_prompt = x

    xs = x_sample
    for l in range(DEPTH):
        p32, p16, k_hm, v_hm = _inproj(xs, mod_smp[l], l, norm_g3, w_in_b, qn, kn, seg, tm=TOKEN_TILE)
        o_a = _attn_nbr(p16, k_hm, v_hm, cache_k, cache_v, bias, l)
        o_c = _fft_grid(p32)
        (o_d,) = _gla(p32, p16, gate_up, gate_b, gn, seg, l, state_gla, zero_init=False)
        xs = _outproj(xs, mod_smp[l], l, o_a, o_c, o_d, p16, conv_w, w_out_b, tm=OUTPROJ_TILE)
    return (y_prompt, xs, caches[0], caches[1], new_state)
```

```python
import functools
import math

import numpy as np
import jax
import jax.numpy as jnp
from jax import lax
from jax.experimental import pallas as pl
from jax.experimental.pallas import tpu as pltpu

F32 = jnp.float32
BF16 = jnp.bfloat16

D_MODEL = 1024
DEPTH = 4
GRID_W = 64
W_GRP = 256
HEAD_DIM = 64
N_HEADS = 4
WIN_H = 8
WIN_W = 16
N_FOURIER = 4
GATE_RANK = 16
GLA_TAU = 16.0
CHUNK = 64
RMS_EPS = 1e-6
N_IN = 14 * W_GRP + 2 * GATE_RANK
LANES = 128
N_IN_PAD = -(-N_IN // LANES) * LANES
G_QA, G_KA, G_VA, G_GA, G_BB, G_CB, G_HB, G_GB, G_UC, G_GC, G_QD, G_KD, G_VD, G_GD = range(14)
F32_GROUPS = (G_UC, G_QD, G_KD)
BF16_GROUPS = (G_QA, G_KA, G_VA, G_VD, G_GA, G_BB, G_CB, G_HB, G_GB, G_GC, G_GD)
P32 = {g: i for i, g in enumerate(F32_GROUPS)}
P16 = {g: i for i, g in enumerate(BF16_GROUPS)}
N_P32 = len(F32_GROUPS) * W_GRP
N_P16 = len(BF16_GROUPS) * W_GRP + LANES
GATE_BLK = len(BF16_GROUPS) * W_GRP // LANES
NEG_BIG = -1e30
VMEM_LIMIT = 56 * 1024 * 1024
TOKEN_TILE = 512
OUTPROJ_TILE = 1024


def _cparams(*sem):
    return pltpu.CompilerParams(dimension_semantics=sem, vmem_limit_bytes=VMEM_LIMIT)


def _bdot(a, b):
    return jnp.dot(a.astype(BF16), b.astype(BF16), preferred_element_type=F32)


def _bdot_nt(a, b):
    return lax.dot_general(a.astype(BF16), b.astype(BF16), (((1,), (1,)), ((), ())),
                           preferred_element_type=F32)


def _split2(x):
    hi = x.astype(BF16)
    lo = (x - hi.astype(F32)).astype(BF16)
    return hi, lo


def _split3(x):
    h1 = x.astype(BF16)
    r = x - h1.astype(F32)
    h2 = r.astype(BF16)
    h3 = (r - h2.astype(F32)).astype(BF16)
    return h1, h2, h3


def _silu(x):
    return x / (1.0 + jnp.exp(-x))


def _head_rms(t, seg, g):
    n = t.shape[0]
    both = jnp.dot(jnp.concatenate(_split2(t * t), axis=0), seg, preferred_element_type=F32)
    ss = both[:n] + both[n:]
    return t * lax.rsqrt(ss * (1.0 / HEAD_DIM) + RMS_EPS) * g


def _dft_cos_sin(n):
    k = np.arange(n)
    ang = 2.0 * np.pi * ((k[:, None] * k[None, :]) % n) / n
    return np.cos(ang), np.sin(ang)


def _block_diag(m, reps):
    n = m.shape[0]
    out = np.zeros((n * reps, n * reps), m.dtype)
    for i in range(reps):
        out[i * n:(i + 1) * n, i * n:(i + 1) * n] = m
    return out


def _seg_ones():
    return jnp.asarray(_block_diag(np.ones((HEAD_DIM, HEAD_DIM)), N_HEADS), BF16)


def _mod_kernel(c_ref, w_ref, b_ref, o_ref):
    o_ref[0] = _bdot(_silu(c_ref[...]), w_ref[0]) + b_ref[0]


def _modulation(cvec, w_mod, b_mod):
    rows = cvec.shape[0]
    return pl.pallas_call(
        _mod_kernel,
        grid=(DEPTH,),
        in_specs=[pl.BlockSpec((rows, D_MODEL), lambda l: (0, 0)),
                  pl.BlockSpec((1, D_MODEL, 3 * D_MODEL), lambda l: (l, 0, 0)),
                  pl.BlockSpec((1, 1, 3 * D_MODEL), lambda l: (l, 0, 0))],
        out_specs=pl.BlockSpec((1, rows, 3 * D_MODEL), lambda l: (l, 0, 0)),
        out_shape=jax.ShapeDtypeStruct((DEPTH, rows, 3 * D_MODEL), F32),
        compiler_params=_cparams("arbitrary"),
        name="adaln_mod",
    )(cvec, w_mod, b_mod.reshape(DEPTH, 1, 3 * D_MODEL))


G_GATE = 14
INPROJ_ORDER = (2, 0, 3, 1, 4, 5, 6, 7, 8, 9, 10, 11, 12, 13, G_GATE)
INPROJ_PIECES = 8


def _inproj_kernel(x_ref, mod_ref, g_ref, w_ref, wg_ref, qn_ref, kn_ref, seg_ref, *rest,
                   for_window_attn, n_seq, layer):
    p32_ref, p16_ref, k_ref, v_ref, h_even, h_odd = rest[-6:]
    t = pl.program_id(0)
    rows = p32_ref.shape[0] // n_seq

    def normed(x_ref, mod_ref, rows=slice(None)):
        x = x_ref[rows, :]
        ms = jnp.mean(x * x, axis=-1, keepdims=True)
        y = x * lax.rsqrt(ms + RMS_EPS) * g_ref[0]
        shift = mod_ref[0, :, 0:D_MODEL]
        scale = mod_ref[0, :, D_MODEL:2 * D_MODEL]
        return (y * (1.0 + scale) + shift).astype(BF16)

    def store_heads(ref, val, ones):
        for q in range(n_seq):
            for h in range(N_HEADS):
                th = val[q * rows:(q + 1) * rows, h * HEAD_DIM:(h + 1) * HEAD_DIM]
                if ones:
                    th = jnp.concatenate([th, jnp.ones_like(th)], axis=1)
                if for_window_attn:
                    ref[0, h] = th.astype(ref.dtype)
                else:
                    ref[q, layer if ref.shape[1] > 1 else 0, h] = th
        if not for_window_attn and ref.shape[1] > 1:
            for other in range(ref.shape[1]):
                if other != layer:
                    ref[:, other] = jnp.zeros((n_seq,) + ref.shape[2:], ref.dtype)

    def step(h_cur, h_nxt):
        seg = seg_ref[...]
        piece = p32_ref.shape[0] // INPROJ_PIECES
        first_late = len(INPROJ_ORDER) - INPROJ_PIECES
        for idx, j in enumerate(INPROJ_ORDER):
            if idx >= first_late:
                part = slice((idx - first_late) * piece, (idx - first_late + 1) * piece)
                h_nxt[part, :] = normed(x_ref, mod_ref, part)
            if j == G_GATE:
                p16_ref[:, GATE_BLK * LANES:N_P16] = jnp.dot(h_cur[...], wg_ref[0],
                                                             preferred_element_type=F32).astype(BF16)
                continue
            cs = slice(j * W_GRP, (j + 1) * W_GRP)
            y = jnp.dot(h_cur[...], w_ref[0, :, cs], preferred_element_type=F32)
            if j == G_QA:
                y = _head_rms(y, seg, qn_ref[0]) * (HEAD_DIM ** -0.5)
            elif j == G_KA:
                y = _head_rms(y, seg, kn_ref[0])
            if j in P32:
                p32_ref[:, P32[j] * W_GRP:(P32[j] + 1) * W_GRP] = y
            else:
                p16_ref[:, P16[j] * W_GRP:(P16[j] + 1) * W_GRP] = y.astype(BF16)
            if j == G_KA:
                store_heads(k_ref, y, False)
            elif j == G_VA:
                store_heads(v_ref, y, for_window_attn)

    @pl.when(t == 0)
    def _():
        h_even[...] = normed(x_ref, mod_ref)
        for ref in (p32_ref, p16_ref, k_ref, v_ref):
            ref[...] = jnp.zeros(ref.shape, ref.dtype)

    @pl.when(t % 2 == 1)
    def _():
        step(h_even, h_odd)

    @pl.when(jnp.logical_and(t > 0, t % 2 == 0))
    def _():
        step(h_odd, h_even)


def _inproj(x, mod, layer, norm_g, w_in, qn, kn, seg, *, tm, caches=None):
    b, s, _ = x.shape
    for_window_attn = caches is None
    assert s % tm == 0 if for_window_attn else tm % s == 0
    n_tiles = b * s // tm
    tiles_per_seq = max(s // tm, 1)
    n_seq = max(tm // s, 1)
    per_batch = mod.shape[0] > 1
    prep = lambda t: jnp.minimum(t, n_tiles - 1)
    done = lambda t: jnp.maximum(t - 1, 0)
    lay = lambda *dims: pl.BlockSpec((1,) + dims, lambda t: (layer,) + (0,) * len(dims))
    in_specs = [pl.BlockSpec((tm, D_MODEL), lambda t: (prep(t), 0)),
                pl.BlockSpec((1, 1, 3 * D_MODEL),
                             (lambda t: (prep(t) // tiles_per_seq, 0, 0)) if per_batch else (lambda t: (0, 0, 0))),
                lay(1, D_MODEL),
                pl.BlockSpec((1, D_MODEL, 14 * W_GRP), lambda t: (layer, 0, 0)),
                pl.BlockSpec((1, D_MODEL, LANES), lambda t: (layer, 0, 14 * W_GRP // LANES)),
                lay(1, W_GRP), lay(1, W_GRP),
                pl.BlockSpec((W_GRP, W_GRP), lambda t: (0, 0))]
    x2 = x.reshape(b * s, D_MODEL)
    args = [x2, mod, norm_g, w_in, w_in, qn, kn, seg]
    out_shape = [jax.ShapeDtypeStruct((b * s, N_P32), F32), jax.ShapeDtypeStruct((b * s, N_P16), BF16)]
    out_specs = [pl.BlockSpec((tm, N_P32), lambda t: (done(t), 0)), pl.BlockSpec((tm, N_P16), lambda t: (done(t), 0))]
    aliases = {}
    if for_window_attn:
        for width in (HEAD_DIM, 2 * HEAD_DIM):
            out_shape.append(jax.ShapeDtypeStruct((b, N_HEADS, s, width), BF16))
            out_specs.append(pl.BlockSpec((1, N_HEADS, tm, width),
                                          lambda t: (done(t) // tiles_per_seq, 0, done(t) % tiles_per_seq, 0)))
    else:
        for i, cache in enumerate(caches):
            aliases[len(args)] = 2 + i
            args.append(cache)
            in_specs.append(pl.BlockSpec(memory_space=pl.ANY))
            out_shape.append(jax.ShapeDtypeStruct(cache.shape, cache.dtype))
            out_specs.append(pl.BlockSpec((n_seq, 1, N_HEADS, s, HEAD_DIM), lambda t: (done(t), layer, 0, 0, 0)))
        if not caches:
            for _ in range(2):
                out_shape.append(jax.ShapeDtypeStruct((b, DEPTH, N_HEADS, s, HEAD_DIM), F32))
                out_specs.append(pl.BlockSpec((n_seq, DEPTH, N_HEADS, s, HEAD_DIM), lambda t: (done(t), 0, 0, 0, 0)))
    p32, p16, k, v = pl.pallas_call(
        functools.partial(_inproj_kernel, for_window_attn=for_window_attn, n_seq=n_seq, layer=layer),
        grid=(n_tiles + 1,),
        in_specs=in_specs,
        out_specs=out_specs,
        out_shape=out_shape,
        input_output_aliases=aliases,
        scratch_shapes=[pltpu.VMEM((tm, D_MODEL), BF16)] * 2,
        compiler_params=_cparams("arbitrary"),
        name="inproj_grid" if for_window_attn else "inproj_ctx",
    )(*args)
    return p32.reshape(b, s, N_P32), p16.reshape(b, s, N_P16), k, v


CTX_SEQS_PER_STEP = 4


def _attn_ctx_kernel(q_ref, k_ref, v_ref, o_ref):
    n_seq, s, _ = q_ref.shape
    heads = [slice(h * HEAD_DIM, (h + 1) * HEAD_DIM) for h in range(N_HEADS)]
    units = [(b, h) for b in range(n_seq) for h in range(N_HEADS)]
    low_half = lax.broadcasted_iota(jnp.int32, (s, 2 * HEAD_DIM), 1) < HEAD_DIM
    ones = jnp.ones((s, HEAD_DIM), BF16)
    q, k, v = ([ref[b] for b in range(n_seq)] for ref in (q_ref, k_ref, v_ref))
    scores = [_bdot_nt(q[b][:, heads[h]], k[b][:, heads[h]]) for b, h in units]
    e = [jnp.exp(x - jnp.max(x, axis=-1, keepdims=True)).astype(BF16) for x in scores]
    res = [jnp.dot(e[u], jnp.concatenate([v[b][:, heads[h]], ones], axis=1), preferred_element_type=F32)
           for u, (b, h) in enumerate(units)]
    for b in range(n_seq):
        pairs = []
        for h in range(0, N_HEADS, 2):
            even, odd = res[b * N_HEADS + h], res[b * N_HEADS + h + 1]
            pairs.append(jnp.where(low_half, even / pltpu.roll(even, HEAD_DIM, axis=1),
                                   pltpu.roll(odd, HEAD_DIM, axis=1) / odd))
        o_ref[b] = jnp.concatenate(pairs, axis=1)


def _attn_ctx(p):
    b, s, _ = p.shape
    n_seq = CTX_SEQS_PER_STEP
    assert b % n_seq == 0
    col = lambda g: pl.BlockSpec((n_seq, s, W_GRP), lambda i, g=g: (i, 0, g))
    return pl.pallas_call(
        _attn_ctx_kernel,
        grid=(b // n_seq,),
        in_specs=[col(P16[G_QA]), col(P16[G_KA]), col(P16[G_VA])],
        out_specs=pl.BlockSpec((n_seq, s, W_GRP), lambda i: (i, 0, 0)),
        out_shape=jax.ShapeDtypeStruct((b, s, W_GRP), F32),
        compiler_params=_cparams("arbitrary"),
        name="attn_ctx",
    )(p, p, p)


def _ctx_attn_fft_kernel(q_ref, k_ref, v_ref, u_ref, m1_ref, m2_ref, oa_ref, oc_ref, *, scale):
    _fft_ctx_kernel(u_ref, m1_ref, m2_ref, oc_ref, scale=scale)
    _attn_ctx_kernel(q_ref, k_ref, v_ref, oa_ref)


def _ctx_attn_fft(p16, p32):
    b, s, _ = p16.shape
    n_seq = CTX_SEQS_PER_STEP
    assert b % n_seq == 0
    consts = _fft_ctx_consts(s)
    col = lambda g: pl.BlockSpec((n_seq, s, W_GRP), lambda i, g=g: (i, 0, g))
    full = lambda a: pl.BlockSpec(a.shape, lambda i: (0,) * a.ndim)
    out = pl.BlockSpec((n_seq, s, W_GRP), lambda i: (i, 0, 0))
    return pl.pallas_call(
        functools.partial(_ctx_attn_fft_kernel, scale=1.0 / math.sqrt(s * (W_GRP // N_FOURIER))),
        grid=(b // n_seq,),
        in_specs=[col(P16[G_QA]), col(P16[G_KA]), col(P16[G_VA]), col(P32[G_UC])] + [full(a) for a in consts],
        out_specs=[out, out],
        out_shape=[jax.ShapeDtypeStruct((b, s, W_GRP), F32)] * 2,
        compiler_params=_cparams("arbitrary"),
        name="ctx_attn_fft",
    )(p16, p16, p16, p32, *consts)


def _bias_kernel(rpb_ref, o_ref):
    lh = pl.program_id(0)
    q = lax.broadcasted_iota(jnp.int32, (GRID_W, GRID_W), 0)
    kc = lax.broadcasted_iota(jnp.int32, (GRID_W, GRID_W), 1)
    dc = jnp.clip(kc - q + (WIN_W - 1), 0, 2 * WIN_W - 2)
    cs = jnp.clip(q - WIN_W // 2, 0, GRID_W - WIN_W)
    in_win = jnp.where(kc >= cs, jnp.where(kc < cs + WIN_W, 1, 0), 0) == 1
    n_dr, n_dc = 2 * WIN_H - 1, 2 * WIN_W - 1
    tiles = []
    for dr0 in range(0, n_dr, WIN_H):
        drs = range(dr0, min(dr0 + WIN_H, n_dr))
        part = [jnp.zeros((GRID_W, GRID_W), F32) for _ in drs]
        for d in range(n_dc):
            hit = dc == d
            part = [jnp.where(hit, rpb_ref[(lh * n_dr + dr) * n_dc + d], t) for dr, t in zip(drs, part)]
        tiles += [jnp.where(in_win, t, NEG_BIG) for t in part]
    for dl in range(WIN_H):
        for i in range(WIN_H):
            o_ref[0, dl, :, i * GRID_W:(i + 1) * GRID_W] = tiles[dl + i]


def _bias_tables(rpb):
    out = pl.pallas_call(
        _bias_kernel,
        grid=(DEPTH * N_HEADS,),
        in_specs=[pl.BlockSpec(memory_space=pltpu.SMEM)],
        out_specs=pl.BlockSpec((1, WIN_H, GRID_W, WIN_H * GRID_W), lambda i: (i, 0, 0, 0)),
        out_shape=jax.ShapeDtypeStruct((DEPTH * N_HEADS, WIN_H, GRID_W, WIN_H * GRID_W), F32),
        compiler_params=_cparams("arbitrary"),
        name="rpb_tables",
    )(rpb.reshape(-1))
    return out.reshape(DEPTH, N_HEADS, WIN_H, GRID_W, WIN_H * GRID_W)


ATTN_ROW_GROUP = 8


def _attn_nbr_kernel(q_ref, k_ref, v_ref, ck_ref, cv_ref, bias_ref, o_ref, *, rows_per_step):
    rb = pl.program_id(1)
    n_rows = k_ref.shape[2] // GRID_W
    kh = min(WIN_H, n_rows)
    n_lat = kh * GRID_W
    heads = [slice(h * HEAD_DIM, (h + 1) * HEAD_DIM) for h in range(N_HEADS)]
    units = [(g, h) for g in range(ATTN_ROW_GROUP) for h in range(N_HEADS)]
    low_half = lax.broadcasted_iota(jnp.int32, (GRID_W, 2 * HEAD_DIM), 1) < HEAD_DIM
    ctx_k = [ck_ref[0, 0, h].astype(BF16) for h in range(N_HEADS)]
    ctx_ones = jnp.ones((cv_ref.shape[3], HEAD_DIM), BF16)
    ctx_v = [jnp.concatenate([cv_ref[0, 0, h].astype(BF16), ctx_ones], axis=1) for h in range(N_HEADS)]

    def rows_body(jj, carry):
        q0, k0, dl, q = [], [], [], []
        for g in range(ATTN_ROW_GROUP):
            j = jj * ATTN_ROW_GROUP + g
            r = rb * rows_per_step + j
            rs = jnp.clip(r - kh // 2, 0, n_rows - kh)
            dl.append(rs - r + (WIN_H - 1))
            k0.append(pl.multiple_of(rs * GRID_W, GRID_W))
            q0.append(pl.multiple_of(j * GRID_W, GRID_W))
            q.append(q_ref[0, pl.ds(q0[g], GRID_W), :].astype(BF16))
        s_lat = [_bdot_nt(q[g][:, heads[h]], k_ref[0, h, pl.ds(k0[g], n_lat), :]) + bias_ref[0, h, dl[g]]
                 for g, h in units]
        q_all = jnp.concatenate(q, axis=0)
        s_ctx_all = [_bdot_nt(q_all[:, hs], ctx_k[h]) for h, hs in enumerate(heads)]
        s_ctx = [s_ctx_all[h][g * GRID_W:(g + 1) * GRID_W] for g, h in units]
        m = [jnp.maximum(jnp.max(a, axis=-1, keepdims=True), jnp.max(c, axis=-1, keepdims=True))
             for a, c in zip(s_lat, s_ctx)]
        e_lat = [jnp.exp(a - mu).astype(BF16) for a, mu in zip(s_lat, m)]
        e_ctx = [jnp.exp(c - mu).astype(BF16) for c, mu in zip(s_ctx, m)]
        pv_ctx = [jnp.dot(jnp.concatenate([e_ctx[g * N_HEADS + h] for g in range(ATTN_ROW_GROUP)], axis=0),
                          ctx_v[h], preferred_element_type=F32) for h in range(N_HEADS)]
        res = [jnp.dot(e_lat[u], v_ref[0, h, pl.ds(k0[g], n_lat), :], preferred_element_type=F32)
               + pv_ctx[h][g * GRID_W:(g + 1) * GRID_W]
               for u, (g, h) in enumerate(units)]
        for g in range(ATTN_ROW_GROUP):
            pairs = []
            for h in range(0, N_HEADS, 2):
                even, odd = res[g * N_HEADS + h], res[g * N_HEADS + h + 1]
                pairs.append(jnp.where(low_half, even / pltpu.roll(even, HEAD_DIM, axis=1),
                                       pltpu.roll(odd, HEAD_DIM, axis=1) / odd))
            o_ref[0, pl.ds(q0[g], GRID_W), :] = jnp.concatenate(pairs, axis=1)
        return carry

    lax.fori_loop(0, rows_per_step // ATTN_ROW_GROUP, rows_body, 0)


def _attn_nbr(p, k_hm, v_hm, cache_k, cache_v, bias, layer, *, rows_per_step=32):
    b, s, _ = p.shape
    tq = rows_per_step * GRID_W
    past = cache_v.shape[3]
    ctx_spec = lambda *dims: pl.BlockSpec((1, 1, N_HEADS) + dims, lambda i, j: (i, layer, 0, 0, 0))
    return pl.pallas_call(
        functools.partial(_attn_nbr_kernel, rows_per_step=rows_per_step),
        grid=(b, s // tq),
        in_specs=[pl.BlockSpec((1, tq, W_GRP), lambda i, j: (i, j, P16[G_QA])),
                  pl.BlockSpec((1, N_HEADS, s, HEAD_DIM), lambda i, j: (i, 0, 0, 0)),
                  pl.BlockSpec((1, N_HEADS, s, 2 * HEAD_DIM), lambda i, j: (i, 0, 0, 0)),
                  ctx_spec(past, HEAD_DIM), ctx_spec(past, HEAD_DIM),
                  pl.BlockSpec((1, N_HEADS, WIN_H, GRID_W, WIN_H * GRID_W), lambda i, j: (layer, 0, 0, 0, 0))],
        out_specs=pl.BlockSpec((1, tq, W_GRP), lambda i, j: (i, j, 0)),
        out_shape=jax.ShapeDtypeStruct((b, s, W_GRP), F32),
        compiler_params=_cparams("arbitrary", "arbitrary"),
        name="attn_nbr",
    )(p, k_hm, v_hm, cache_k, cache_v, bias)


def _fft_ctx_consts(s):
    c, sn = _dft_cos_sin(s)
    cc, sc = _dft_cos_sin(W_GRP // N_FOURIER)
    m1 = np.concatenate([c, -sn], axis=0)
    m2 = np.concatenate([_block_diag(cc, N_FOURIER), _block_diag(sc, N_FOURIER)], axis=0)
    return jnp.asarray(m1, F32), jnp.asarray(m2, F32)


def _fft_ctx_kernel(u_ref, m1_ref, m2_ref, o_ref, *, scale):
    n_seq, s, _ = u_ref.shape
    m1 = m1_ref[...].astype(BF16)
    t = [_bdot(m1, u_ref[b]) for b in range(n_seq)]
    x = jnp.concatenate([jnp.concatenate([tb[:s], tb[s:]], axis=1) for tb in t], axis=0)
    out = _bdot(x, m2_ref[...]) * scale
    for b in range(n_seq):
        o_ref[b] = out[b * s:(b + 1) * s]


def _fft_ctx(p):
    b, s, _ = p.shape
    n_seq = CTX_SEQS_PER_STEP
    assert b % n_seq == 0
    consts = _fft_ctx_consts(s)
    full = lambda a: pl.BlockSpec(a.shape, lambda i: (0,) * a.ndim)
    return pl.pallas_call(
        functools.partial(_fft_ctx_kernel, scale=1.0 / math.sqrt(s * (W_GRP // N_FOURIER))),
        grid=(b // n_seq,),
        in_specs=[pl.BlockSpec((n_seq, s, W_GRP), lambda i: (i, 0, P32[G_UC]))] + [full(a) for a in consts],
        out_specs=pl.BlockSpec((n_seq, s, W_GRP), lambda i: (i, 0, 0)),
        out_shape=jax.ShapeDtypeStruct((b, s, W_GRP), F32),
        compiler_params=_cparams("arbitrary"),
        name="fft_ctx",
    )(p, *consts)


def _fft_grid_consts():
    n = GRID_W
    c, sn = _dft_cos_sin(n)
    cc, sc = _dft_cos_sin(W_GRP // N_FOURIER)
    m1 = np.concatenate([c, -sn], axis=0)
    m2 = np.block([[c, sn], [-sn, c]])
    reps = LANES // (W_GRP // N_FOURIER)
    m3 = np.concatenate([_block_diag(cc, reps), _block_diag(sc, reps)], axis=0)
    k = np.arange(n)
    ang = 2.0 * np.pi * (k[:, None] * k[None, :]) / (n * n)
    twr = np.broadcast_to(np.cos(ang)[:, :, None], (n, n, LANES))
    twi = np.broadcast_to(-np.sin(ang)[:, :, None], (n, n, LANES))
    return tuple(jnp.asarray(m, F32) for m in (m1, m2, m3, twr, twi))


FFT_UNROLL = 8
FFT_PITCH = 72


def _fft_grid_kernel(u_ref, m1_ref, m2_ref, m3_ref, twr_ref, twi_ref, o_ref, tr_scr, ti_scr, *, scale):
    n = GRID_W
    m1, m2, m3 = (r[...].astype(BF16) for r in (m1_ref, m2_ref, m3_ref))

    def stage1(b, carry):
        a_rows = u_ref[0, pl.ds(b, n, stride=n), :]
        t = _bdot(m1, a_rows)
        tr, ti = t[:n], t[n:]
        wr, wi = twr_ref[b], twi_ref[b]
        tr_scr[pl.ds(b, n, stride=FFT_PITCH), :] = tr * wr - ti * wi
        ti_scr[pl.ds(b, n, stride=FFT_PITCH), :] = tr * wi + ti * wr
        return carry

    lax.fori_loop(0, n, stage1, 0, unroll=FFT_UNROLL)

    def stage2(dd, carry):
        xcs = []
        for u in range(FFT_UNROLL):
            r0 = pl.multiple_of((dd * FFT_UNROLL + u) * FFT_PITCH, 8)
            g = jnp.concatenate([tr_scr[pl.ds(r0, n), :], ti_scr[pl.ds(r0, n), :]], axis=0)
            x = _bdot(m2, g)
            xcs.append(jnp.concatenate([x[:n], x[n:]], axis=1))
        out = _bdot(jnp.concatenate(xcs, axis=0), m3) * scale
        for u in range(FFT_UNROLL):
            o_ref[0, pl.ds(dd * FFT_UNROLL + u, n, stride=n), :] = out[u * n:(u + 1) * n]
        return carry

    lax.fori_loop(0, n // FFT_UNROLL, stage2, 0)


def _fft_grid(p):
    b, s, _ = p.shape
    assert s == GRID_W * GRID_W
    consts = _fft_grid_consts()
    full = lambda a: pl.BlockSpec(a.shape, lambda i, j: (0,) * a.ndim)
    blk0 = P32[G_UC] * W_GRP // LANES
    return pl.pallas_call(
        functools.partial(_fft_grid_kernel, scale=1.0 / math.sqrt(s * (W_GRP // N_FOURIER))),
        grid=(b, W_GRP // LANES),
        in_specs=[pl.BlockSpec((1, s, LANES), lambda i, j: (i, 0, blk0 + j))] + [full(a) for a in consts],
        out_specs=pl.BlockSpec((1, s, LANES), lambda i, j: (i, 0, j)),
        out_shape=jax.ShapeDtypeStruct((b, s, W_GRP), F32),
        scratch_shapes=[pltpu.VMEM((GRID_W * FFT_PITCH, LANES), F32)] * 2,
        compiler_params=_cparams("arbitrary", "arbitrary"),
        name="fft_grid",
    )(p, *consts)


GLA_CHUNKS_PER_TRIP = 16


def _gla_kernel(*refs, n_chunks, n_seq, group, zero_init, layer):
    q_ref, k_ref, v_ref, a_ref, up_ref, gb_ref, gn_ref, seg_ref = refs[:8]
    if zero_init:
        o_ref, sfin_ref, qd_scr, st_scr, dec_scr = refs[-5:]
        s0_ref = None
    else:
        s0_ref, o_ref, qd_scr, st_scr, dec_scr = refs[8:]
        sfin_ref = None
    c = CHUNK
    row = lax.broadcasted_iota(jnp.int32, (c, c), 0)
    col = lax.broadcasted_iota(jnp.int32, (c, c), 1)
    heads = [slice(h * HEAD_DIM, (h + 1) * HEAD_DIM) for h in range(N_HEADS)]
    wrow = lax.broadcasted_iota(jnp.int32, (W_GRP, W_GRP), 0)
    wcol = lax.broadcasted_iota(jnp.int32, (W_GRP, W_GRP), 1)
    head_diag = (wrow // HEAD_DIM) == (wcol // HEAD_DIM)

    def blockdiag(x):
        return jnp.where(head_diag, jnp.concatenate([x] * N_HEADS, axis=0), jnp.zeros((), x.dtype))

    trow = lax.broadcasted_iota(jnp.int32, (c, W_GRP), 0)
    tcol = lax.broadcasted_iota(jnp.int32, (c, W_GRP), 1) % c
    keep = [trow >= tcol, trow <= tcol]
    eye_tiled = jnp.where(trow == tcol, 1.0, 0.0).astype(BF16)

    def head_transpose_exact(x):
        return sum(_bdot_nt(eye_tiled, blockdiag(part)) for part in _split3(x))

    sum_ops = [jnp.where(row >= col, 1.0, 0.0).astype(BF16),
               jnp.where(row <= col, 1.0, 0.0).astype(BF16)]
    dirs = (0, 1)

    def chunk_rows(n):
        return pl.ds(pl.multiple_of(n * c, c), c)

    def phase1(i, carry):
        ns = [i * group + g for g in range(group)]
        rows = [chunk_rows(n) for n in ns]
        x_all = _bdot(jnp.concatenate([a_ref[0, r, :] for r in rows], axis=0), up_ref[0]) + gb_ref[0]
        x = [[x_all[g * c:(g + 1) * c, d * W_GRP:(d + 1) * W_GRP] for d in dirs] for g in range(group)]
        la = [[(jnp.minimum(xd, 0.0) - jnp.log(1.0 + jnp.exp(-jnp.abs(xd)))) * (1.0 / GLA_TAU) for xd in xg]
              for xg in x]
        cum = [[sum(jnp.dot(sum_ops[d], part, preferred_element_type=F32) for part in _split3(lg[d]))
                for d in dirs] for lg in la]
        tot_row = [[cg[0][c - 1:c, :], cg[1][0:1, :]] for cg in cum]
        q = [q_ref[0, r, :] * (HEAD_DIM ** -0.5) for r in rows]
        k = [k_ref[0, r, :] for r in rows]
        v_bd = [blockdiag(v_ref[0, r, :].astype(BF16)) for r in rows]
        q_dec = [[(q[g] * jnp.exp(cum[g][d])).astype(BF16) for d in dirs] for g in range(group)]
        k_inv = [[blockdiag((k[g] * jnp.exp(-cum[g][d])).astype(BF16)) for d in dirs] for g in range(group)]
        k_end = [[blockdiag((k[g] * jnp.exp(tot_row[g][d] - cum[g][d])).astype(BF16)) for d in dirs]
                 for g in range(group)]
        scores = [[_bdot_nt(q_dec[g][d], k_inv[g][d]) for d in dirs] for g in range(group)]
        v_t = [_bdot_nt(eye_tiled, vb).astype(BF16) for vb in v_bd]
        upd = [[_bdot(v_t[g], k_end[g][d]) for d in dirs] for g in range(group)]
        att = [[jnp.where(keep[d], scores[g][d], 0.0).astype(BF16) for d in dirs] for g in range(group)]
        for g in range(group):
            both = _bdot(jnp.concatenate(att[g], axis=0), v_bd[g])
            o_ref[0, rows[g], :] = both[:c] + both[c:]
            for d in dirs:
                qd_scr[d, rows[g], :] = q_dec[g][d]
                dec_scr[d, pl.ds(ns[g], 1), :] = jnp.exp(tot_row[g][d])
                st_scr[d, ns[g]] = upd[g][d]
        return carry

    lax.fori_loop(0, n_chunks // group, phase1, 0)

    def init_state(d):
        if zero_init:
            return jnp.zeros((HEAD_DIM, W_GRP), F32)
        return head_transpose_exact(jnp.concatenate([s0_ref[0, 0, d, h] for h in range(N_HEADS)], axis=1))

    per_seq = n_chunks // n_seq
    finals = []
    for q in range(n_seq):
        finals.append([])
        for d in range(2):
            def scan_body(i, st, d=d, q=q):
                n = q * per_seq + (i if d == 0 else per_seq - 1 - i)
                upd = st_scr[d, n]
                st_scr[d, n] = st
                return st * dec_scr[d, pl.ds(n, 1), :] + upd
            finals[q].append(lax.fori_loop(0, per_seq, scan_body, init_state(d)))

    def phase3(i, carry):
        ns = [i * group + g for g in range(group)]
        rows = [chunk_rows(n) for n in ns]
        st_bd = [[blockdiag(st_scr[d, n].astype(BF16)) for d in dirs] for n in ns]
        inter = [[_bdot_nt(qd_scr[d, rows[g], :], st_bd[g][d]) for d in dirs] for g in range(group)]
        o = jnp.concatenate([o_ref[0, rows[g], :] + inter[g][0] + inter[g][1] for g in range(group)], axis=0)
        o = _head_rms(o, seg_ref[...], gn_ref[0])
        for g in range(group):
            o_ref[0, rows[g], :] = o[g * c:(g + 1) * c]
        return carry

    lax.fori_loop(0, n_chunks // group, phase3, 0)
    if sfin_ref is not None:
        for q in range(n_seq):
            for d in range(2):
                s_fin = head_transpose_exact(finals[q][d])
                for h, hs in enumerate(heads):
                    sfin_ref[q, layer if sfin_ref.shape[1] > 1 else 0, d, h] = s_fin[:, hs]
        for other in range(sfin_ref.shape[1]):
            if sfin_ref.shape[1] > 1 and other != layer:
                sfin_ref[:, other] = jnp.zeros((n_seq,) + sfin_ref.shape[2:], sfin_ref.dtype)


def _gla(p32, p16, gate_up, gate_b, gn, seg, layer, state, *, zero_init):
    b, s, _ = p32.shape
    n_seq = max(1, min(b, GLA_CHUNKS_PER_TRIP * CHUNK // s)) if zero_init else 1
    assert b % n_seq == 0
    rows = n_seq * s
    n_chunks = rows // CHUNK
    group = min(GLA_CHUNKS_PER_TRIP, n_chunks)
    assert n_chunks % group == 0
    fold = lambda a: a.reshape(b // n_seq, rows, a.shape[-1])
    col = lambda g: pl.BlockSpec((1, rows, W_GRP), lambda i, g=g: (i, 0, g))
    lay = lambda shape: pl.BlockSpec((1,) + shape, lambda i: (layer,) + (0,) * len(shape))
    in_specs = [col(P32[G_QD]), col(P32[G_KD]), col(P16[G_VD]),
                pl.BlockSpec((1, rows, LANES), lambda i: (i, 0, GATE_BLK)),
                lay((LANES, 2 * W_GRP)), lay((1, 2 * W_GRP)),
                lay((1, W_GRP)), pl.BlockSpec((W_GRP, W_GRP), lambda i: (0, 0))]
    args = [fold(p32), fold(p32), fold(p16), fold(p16), gate_up, gate_b, gn, seg, state]
    state_spec = pl.BlockSpec((n_seq, 1, 2, N_HEADS, HEAD_DIM, HEAD_DIM), lambda i: (i, layer, 0, 0, 0, 0))
    out_specs = [pl.BlockSpec((1, rows, W_GRP), lambda i: (i, 0, 0))]
    out_shape = [jax.ShapeDtypeStruct((b // n_seq, rows, W_GRP), F32)]
    aliases = {}
    if not zero_init:
        in_specs.append(state_spec)
    elif state is None:
        args.pop()
        out_specs.append(pl.BlockSpec((n_seq, DEPTH, 2, N_HEADS, HEAD_DIM, HEAD_DIM), lambda i: (i, 0, 0, 0, 0, 0)))
        out_shape.append(jax.ShapeDtypeStruct((b, DEPTH, 2, N_HEADS, HEAD_DIM, HEAD_DIM), F32))
    else:
        aliases = {len(args) - 1: 1}
        in_specs.append(pl.BlockSpec(memory_space=pl.ANY))
        out_specs.append(state_spec)
        out_shape.append(jax.ShapeDtypeStruct(state.shape, state.dtype))
    o, *rest = pl.pallas_call(
        functools.partial(_gla_kernel, n_chunks=n_chunks, n_seq=n_seq, group=group, zero_init=zero_init,
                          layer=layer),
        grid=(b // n_seq,),
        in_specs=in_specs,
        out_specs=out_specs,
        out_shape=out_shape,
        input_output_aliases=aliases,
        scratch_shapes=[pltpu.VMEM((2, rows, W_GRP), BF16),
                        pltpu.VMEM((2, n_chunks, HEAD_DIM, W_GRP), F32),
                        pltpu.VMEM((2, max(n_chunks, 8), W_GRP), F32)],
        compiler_params=_cparams("arbitrary"),
        name="gla_zero" if zero_init else "gla",
    )(*args)
    return (o.reshape(b, s, W_GRP), *rest)


def _outproj_kernel(x_ref, mod_ref, oa_ref, oc_ref, od_ref, ga_ref, bb_ref, cb_ref, hb_ref, gb_ref,
                    gc_ref, gd_ref, cprev_ref, hprev_ref, cnext_ref, hnext_ref, cw_ref, w_ref, o_ref, *, seq_len):
    tm = x_ref.shape[0]
    f32 = lambda ref: ref[...].astype(F32)
    last = cprev_ref.shape[0] - 1
    u = f32(cb_ref) * f32(hb_ref)
    rid = lax.broadcasted_iota(jnp.int32, u.shape, 0)
    pos = (pl.program_id(0) * tm + rid) % seq_len
    u_prev = jnp.where(rid == 0, f32(cprev_ref)[last:, :] * f32(hprev_ref)[last:, :], pltpu.roll(u, 1, axis=0))
    u_next = jnp.where(rid == tm - 1, f32(cnext_ref)[0:1, :] * f32(hnext_ref)[0:1, :], pltpu.roll(u, tm - 1, axis=0))
    u_prev = jnp.where(pos == 0, 0.0, u_prev)
    u_next = jnp.where(pos == seq_len - 1, 0.0, u_next)
    cw = cw_ref[0]
    conv = u_prev * cw[0:1, :] + u * cw[1:2, :] + u_next * cw[2:3, :]
    y_a = oa_ref[...] * _silu(f32(ga_ref))
    y_b = f32(bb_ref) * conv * _silu(f32(gb_ref))
    y_c = oc_ref[...] * _silu(f32(gc_ref))
    y_d = od_ref[...] * _silu(f32(gd_ref))
    y = jnp.concatenate([y_a, y_b, y_c, y_d], axis=1).astype(BF16)
    gate = mod_ref[0, :, 2 * D_MODEL:3 * D_MODEL]
    o_ref[...] = x_ref[...] + gate * jnp.dot(y, w_ref[0], preferred_element_type=F32)


def _outproj(x, mod, layer, o_a, o_c, o_d, p, conv_w, w_out, *, tm):
    b, s, _ = x.shape
    n = b * s
    per_batch = mod.shape[0] > 1
    assert s % tm == 0 or (tm % s == 0 and not per_batch)
    tiles_per_seq = max(s // tm, 1)
    flat = lambda a: a.reshape(n, a.shape[-1])
    tile = lambda width: pl.BlockSpec((tm, width), lambda t: (t, 0))
    col = lambda g: pl.BlockSpec((tm, W_GRP), lambda t, g=g: (t, g))
    halo = 16
    th = tm // halo
    prev = lambda g: pl.BlockSpec((halo, W_GRP), lambda t, g=g: (jnp.maximum(t * th - 1, 0), g))
    nxt = lambda g: pl.BlockSpec((halo, W_GRP), lambda t, g=g: (jnp.minimum((t + 1) * th, n // halo - 1), g))
    out = pl.pallas_call(
        functools.partial(_outproj_kernel, seq_len=s),
        grid=(n // tm,),
        in_specs=[tile(D_MODEL),
                  pl.BlockSpec((1, 1, 3 * D_MODEL),
                               (lambda t: (t // tiles_per_seq, 0, 0)) if per_batch else (lambda t: (0, 0, 0))),
                  tile(W_GRP), tile(W_GRP), tile(W_GRP),
                  *[col(P16[g]) for g in (G_GA, G_BB, G_CB, G_HB, G_GB, G_GC, G_GD)],
                  prev(P16[G_CB]), prev(P16[G_HB]), nxt(P16[G_CB]), nxt(P16[G_HB]),
                  pl.BlockSpec((1, 3, W_GRP), lambda t: (layer, 0, 0)),
                  pl.BlockSpec((1, D_MODEL, D_MODEL), lambda t: (layer, 0, 0))],
        out_specs=tile(D_MODEL),
        out_shape=jax.ShapeDtypeStruct((n, D_MODEL), F32),
        compiler_params=_cparams("arbitrary"),
        name="outproj",
    )(flat(x), mod, flat(o_a), flat(o_c), flat(o_d), *([flat(p)] * 11), conv_w, w_out)
    return out.reshape(b, s, D_MODEL)


def kernel(x_prompt, x_sample, cache_k, cache_v, state_gla, c, c_ctx, norm_g, w_mod, b_mod, w_in, q_norm_g,
           k_norm_g, rpb, conv_w, gla_up_f, gla_bias_f, gla_up_b, gla_bias_b, gla_norm_g, w_out):
    dec_batch = c.shape[0]
    pad_rows = 8 - (1 + dec_batch) % 8
    cvec = jnp.concatenate([c_ctx[None], c, jnp.zeros((pad_rows, D_MODEL), F32)], axis=0)
    mod = _modulation(cvec, w_mod, b_mod)
    mod_ctx = mod[:, 0:1].reshape(DEPTH, 1, 1, 3 * D_MODEL)
    mod_smp = mod[:, 1:1 + dec_batch].reshape(DEPTH, dec_batch, 1, 3 * D_MODEL)

    w_in_b = jnp.pad(w_in, ((0, 0), (0, 0), (0, N_IN_PAD - N_IN))).astype(BF16)
    w_out_b = w_out.astype(BF16)
    seg = _seg_ones()
    norm_g3 = norm_g.reshape(DEPTH, 1, D_MODEL)
    qn = jnp.tile(q_norm_g, (1, N_HEADS)).reshape(DEPTH, 1, W_GRP)
    kn = jnp.tile(k_norm_g, (1, N_HEADS)).reshape(DEPTH, 1, W_GRP)
    gn = jnp.tile(gla_norm_g, (1, N_HEADS)).reshape(DEPTH, 1, W_GRP)
    gate_up = jnp.zeros((DEPTH, LANES, 2 * W_GRP), F32)
    gate_up = gate_up.at[:, :GATE_RANK, :W_GRP].set(gla_up_f).at[:, GATE_RANK:2 * GATE_RANK, W_GRP:].set(gla_up_b)
    gate_b = jnp.concatenate([gla_bias_f, gla_bias_b], axis=-1).reshape(DEPTH, 1, 2 * W_GRP)
    bias = _bias_tables(rpb)

    x = x_prompt
    batch, seq, _ = x.shape
    caches, new_state = (), None
    for l in range(DEPTH):
        p32, p16, *caches = _inproj(x, mod_ctx[l], l, norm_g3, w_in_b, qn, kn, seg, tm=TOKEN_TILE, caches=tuple(caches))
        o_a, o_c = _ctx_attn_fft(p16, p32)
        o_d, new_state = _gla(p32, p16, gate_up, gate_b, gn, seg, l, new_state, zero_init=True)
        x = _outproj(x, mod_ctx[l], l, o_a, o_c, o_d, p16, conv_w, w_out_b, tm=OUTPROJ_TILE)
    y_prompt = x

    xs = x_sample
    for l in range(DEPTH):
        p32, p16, k_hm, v_hm = _inproj(xs, mod_smp[l], l, norm_g3, w_in_b, qn, kn, seg, tm=TOKEN_TILE)
        o_a = _attn_nbr(p16, k_hm, v_hm, cache_k, cache_v, bias, l)
        o_c = _fft_grid(p32)
        (o_d,) = _gla(p32, p16, gate_up, gate_b, gn, seg, l, state_gla, zero_init=False)
        xs = _outproj(xs, mod_smp[l], l, o_a, o_c, o_d, p16, conv_w, w_out_b, tm=OUTPROJ_TILE)
    return (y_prompt, xs, caches[0], caches[1], new_state)
```
